```python
import math
import jax, jax.numpy as jnp
from jax import lax
import numpy as np

D_MODEL = 1024
BATCH = 4
SEQ = 4096
DEPTH = 1

CHUNK = 64
Q_BLOCK = 128
N_MEM = 256
MIX_WIDTH = D_MODEL
FOX_WIDTH = MIX_WIDTH // 2
HGRN_WIDTH = MIX_WIDTH - FOX_WIDTH
FOX_HEAD_DIM = 64
FOX_HEADS = FOX_WIDTH // FOX_HEAD_DIM
HGRN_KEY_DIM = 128
HGRN_HEADS = HGRN_WIDTH // HGRN_KEY_DIM
HGRN_VAL_DIM = HGRN_WIDTH // HGRN_HEADS
X_HEADS = 4
X_HEAD_DIM = D_MODEL // X_HEADS
D_FF = 4 * D_MODEL
EPS = 1e-6
SPLITS = (FOX_WIDTH, FOX_WIDTH, FOX_WIDTH, FOX_HEADS, HGRN_WIDTH, HGRN_WIDTH, HGRN_WIDTH, HGRN_WIDTH)
IN_COLS = sum(SPLITS)

kernel_name = "hymba_fox_hgrn2_memory_block"


def rmsnorm(x, g):
    xf = x.astype(jnp.float32)
    y = xf * lax.rsqrt(jnp.mean(xf * xf, axis=-1, keepdims=True) + EPS)
    return (y * g.astype(jnp.float32)).astype(x.dtype)


def fox_attention(q, k, v, log_f):
    b_, h_, s_, d_ = q.shape
    nb = s_ // Q_BLOCK
    scale = 1.0 / math.sqrt(d_)
    F = jnp.cumsum(log_f, axis=-1)
    qb = q.reshape(b_, h_, nb, Q_BLOCK, d_).transpose(2, 0, 1, 3, 4)
    Fb = F.reshape(b_, h_, nb, Q_BLOCK).transpose(2, 0, 1, 3)
    pos = jnp.arange(s_, dtype=jnp.int32).reshape(nb, Q_BLOCK)
    key_pos = jnp.arange(s_, dtype=jnp.int32)

    def block(args):
        qi, Fi, pi = args
        s = jnp.einsum('bhqd,bhkd->bhqk', qi, k).astype(jnp.float32) * scale
        s = s + Fi[..., None] - F[:, :, None, :]
        mask = pi[:, None] >= key_pos[None, :]
        s = jnp.where(mask, s, -jnp.inf)
        p = jax.nn.softmax(s, axis=-1)
        return jnp.einsum('bhqk,bhkd->bhqd', p.astype(v.dtype), v)

    out = lax.map(block, (qb, Fb, pos))
    return out.transpose(1, 2, 0, 3, 4).reshape(b_, h_, s_, d_)


def hgrn2_chunkwise(q, f_logit, i, lb):
    b_, s_, _ = q.shape
    n_c = s_ // CHUNK

    def heads(t, d):
        return t.astype(jnp.float32).reshape(b_, n_c, CHUNK, HGRN_HEADS, d).transpose(1, 0, 3, 2, 4)

    lbh = lb.astype(jnp.float32).reshape(HGRN_HEADS, 1, HGRN_KEY_DIM)
    f = lbh + (1.0 - lbh) * jax.nn.sigmoid(heads(f_logit, HGRN_KEY_DIM))
    kk = 1.0 - f
    g = jnp.log(f)
    qq = jax.nn.silu(heads(q, HGRN_KEY_DIM))
    ii = heads(i, HGRN_VAL_DIM)
    causal = jnp.tril(jnp.ones((CHUNK, CHUNK), dtype=bool))

    def step(state, inp):
        qc, kc, ic, gc = inp
        bcum = jnp.cumsum(gc, axis=2)
        diff = bcum[:, :, :, None, :] - bcum[:, :, None, :, :]
        decay = jnp.exp(jnp.where(causal[:, :, None], diff, -jnp.inf))
        attn = jnp.einsum('bhtd,bhsd,bhtsd->bhts', qc, kc, decay)
        intra = jnp.einsum('bhts,bhsv->bhtv', attn, ic)
        inter = jnp.einsum('bhtd,bhdv->bhtv', qc * jnp.exp(bcum), state)
        b_last = bcum[:, :, -1]
        k_dec = kc * jnp.exp(b_last[:, :, None, :] - bcum)
        new_state = jnp.exp(b_last)[..., None] * state + jnp.einsum('bhsd,bhsv->bhdv', k_dec, ic)
        return new_state, intra + inter

    s0 = jnp.zeros((b_, HGRN_HEADS, HGRN_KEY_DIM, HGRN_VAL_DIM), jnp.float32)
    _, outs = lax.scan(step, s0, (qq, kk, ii, g))
    return outs.transpose(1, 0, 3, 2, 4).reshape(b_, s_, HGRN_HEADS, HGRN_VAL_DIM)


def memory_cross_attention(h, mem_n, w_q, w_kv, w_o):
    b_, s_, _ = h.shape
    m_ = mem_n.shape[1]
    q = (h @ w_q).reshape(b_, s_, X_HEADS, X_HEAD_DIM)
    kv = mem_n @ w_kv
    k, v = jnp.split(kv, 2, axis=-1)
    k = k.reshape(b_, m_, X_HEADS, X_HEAD_DIM)
    v = v.reshape(b_, m_, X_HEADS, X_HEAD_DIM)
    s = jnp.einsum('bshd,bmhd->bhsm', q, k).astype(jnp.float32) / math.sqrt(X_HEAD_DIM)
    p = jax.nn.softmax(s, axis=-1)
    o = jnp.einsum('bhsm,bmhd->bshd', p.astype(v.dtype), v).reshape(b_, s_, D_MODEL)
    return o @ w_o


def setup_inputs(seed: int = 0) -> dict:
    key = jax.random.key(seed)
    ks = jax.random.split(key, 20)
    nrm = lambda k, shape, s: jax.random.normal(k, shape, jnp.float32) * s
    gain = lambda k, shape: 1.0 + 0.02 * jax.random.normal(k, shape, jnp.float32)
    return {
        "x": nrm(ks[0], (BATCH, SEQ, D_MODEL), 1.0),
        "mem": nrm(ks[1], (BATCH, N_MEM, D_MODEL), 1.0),
        "norm_mix_g": gain(ks[2], (DEPTH, D_MODEL)),
        "w_in": nrm(ks[3], (DEPTH, D_MODEL, IN_COLS), D_MODEL ** -0.5),
        "fox_f_bias": 1.0 + 0.1 * jax.random.normal(ks[4], (DEPTH, FOX_HEADS), jnp.float32),
        "hgrn_lb_logits": nrm(ks[5], (DEPTH + 1, HGRN_WIDTH), 0.1),
        "hgrn_norm_g": gain(ks[6], (DEPTH, HGRN_VAL_DIM)),
        "w_out": nrm(ks[7], (DEPTH, MIX_WIDTH, D_MODEL), MIX_WIDTH ** -0.5),
        "norm_x_g": gain(ks[8], (DEPTH, D_MODEL)),
        "norm_mem_g": gain(ks[9], (DEPTH, D_MODEL)),
        "w_xq": nrm(ks[10], (DEPTH, D_MODEL, D_MODEL), D_MODEL ** -0.5),
        "w_xkv": nrm(ks[11], (DEPTH, D_MODEL, 2 * D_MODEL), D_MODEL ** -0.5),
        "w_xo": nrm(ks[12], (DEPTH, D_MODEL, D_MODEL), D_MODEL ** -0.5),
        "norm_ff_g": gain(ks[13], (DEPTH, D_MODEL)),
        "w1": nrm(ks[14], (DEPTH, D_MODEL, D_FF), D_MODEL ** -0.5),
        "w2": nrm(ks[15], (DEPTH, D_FF, D_MODEL), D_FF ** -0.5),
        "final_norm_g": gain(ks[16], (D_MODEL,)),
    }


def reference(x, mem, norm_mix_g, w_in, fox_f_bias, hgrn_lb_logits, hgrn_norm_g, w_out,
              norm_x_g, norm_mem_g, w_xq, w_xkv, w_xo, norm_ff_g, w1, w2, final_norm_g):
    b_, s_, _ = x.shape
    lbs = jnp.cumsum(jax.nn.softmax(hgrn_lb_logits.astype(jnp.float32), axis=0), axis=0)
    split_idx = list(np.cumsum(SPLITS)[:-1])
    h = x
    for l in range(DEPTH):
        hn = rmsnorm(h, norm_mix_g[l])
        proj = hn @ w_in[l]
        fq, fk, fv, ff, gq, gf, gi, gg = jnp.split(proj, split_idx, axis=-1)
        to_heads = lambda t: t.reshape(b_, s_, FOX_HEADS, FOX_HEAD_DIM).transpose(0, 2, 1, 3)
        log_f = jax.nn.log_sigmoid((ff + fox_f_bias[l]).astype(jnp.float32)).transpose(0, 2, 1)
        fox_out = fox_attention(to_heads(fq), to_heads(fk), to_heads(fv), log_f)
        fox_out = fox_out.transpose(0, 2, 1, 3).reshape(b_, s_, FOX_WIDTH)
        rec = hgrn2_chunkwise(gq, gf, gi, lbs[l])
        rec = rmsnorm(rec, hgrn_norm_g[l]) * jax.nn.silu(
            gg.astype(jnp.float32).reshape(b_, s_, HGRN_HEADS, HGRN_VAL_DIM))
        rec = rec.reshape(b_, s_, HGRN_WIDTH).astype(h.dtype)
        h = h + jnp.concatenate([fox_out, rec], axis=-1) @ w_out[l]
        mem_n = rmsnorm(mem, norm_mem_g[l])
        h = h + memory_cross_attention(rmsnorm(h, norm_x_g[l]), mem_n, w_xq[l], w_xkv[l], w_xo[l])
        u = rmsnorm(h, norm_ff_g[l]) @ w1[l]
        h = h + jnp.square(jax.nn.relu(u)) @ w2[l]
    return rmsnorm(h, final_norm_g)
```

```python
import functools
import math

import jax
import jax.numpy as jnp
import numpy as np
from jax import lax
from jax.experimental import pallas as pl
from jax.experimental.pallas import tpu as pltpu

EPS = 1e-6
LANES = 128
FOX_HEADS = 8
FOX_HEAD_DIM = 64
FOX_WIDTH = FOX_HEADS * FOX_HEAD_DIM
HGRN_HEADS = 4
HGRN_DIM = 128
HGRN_WIDTH = HGRN_HEADS * HGRN_DIM
X_HEADS = 4
HGRN_STEP = 16
VMEM_LIMIT = 56 * 1024 * 1024

AUG_F = FOX_HEAD_DIM
AUG_ONE = FOX_HEAD_DIM + 3
FCAT_ONE_LANE = 24

BF16 = jnp.bfloat16
F32 = jnp.float32


def _dot(a, b):
    return jnp.dot(a, b, preferred_element_type=F32)


def _dot_nt(a, b):
    return lax.dot_general(a, b, (((1,), (1,)), ((), ())), preferred_element_type=F32)


def _dot_tn(a, b):
    return lax.dot_general(a, b, (((0,), (0,)), ((), ())), preferred_element_type=F32)


def _split3(v):
    hi = v.astype(BF16)
    r1 = v - hi.astype(F32)
    mid = r1.astype(BF16)
    lo = (r1 - mid.astype(F32)).astype(BF16)
    return hi, mid, lo


def _tri_cumsum(tri, v):
    hi, mid, lo = _split3(v)
    return _dot(tri, hi) + _dot(tri, mid) + _dot(tri, lo)


def _rms(x, g):
    ms = jnp.mean(x * x, axis=-1, keepdims=True)
    return x * lax.rsqrt(ms + EPS) * g


def _inproj_kernel(x_ref, g_ref, wqk_ref, wv_ref, wff_ref, wh_ref, fb_ref, lbl_ref, sel_ref, tri_ref,
                   q_ref, k_ref, v_ref, hq_ref, hk_ref, hg_ref, hi_ref, hgate_ref, carry_ref, *, layer):
    @pl.when(pl.program_id(1) == 0)
    def _():
        carry_ref[...] = jnp.zeros_like(carry_ref)

    hb = _rms(x_ref[0], g_ref[...]).astype(BF16)
    tm = hb.shape[0]
    lane = lax.broadcasted_iota(jnp.int32, (tm, LANES), 1)

    z = _dot(hb, wff_ref[...]) + fb_ref[...]
    logf = jnp.minimum(z, 0.0) - jnp.log(1.0 + jnp.exp(-jnp.abs(z)))
    logf = jnp.where(lane < FOX_HEADS, logf, 0.0)
    fcum = _tri_cumsum(tri_ref[...], logf) + carry_ref[0:1, :]
    carry_ref[...] = jnp.broadcast_to(fcum[tm - 1:tm, :], carry_ref.shape)

    f_hi = fcum.astype(BF16).astype(F32)
    r1 = fcum - f_hi
    f_mid = r1.astype(BF16).astype(F32)
    f_lo = (r1 - f_mid).astype(BF16).astype(F32)
    fcat = (f_hi + pltpu.roll(f_mid, FOX_HEADS, 1) + pltpu.roll(f_lo, 2 * FOX_HEADS, 1)
            + jnp.where(lane == FCAT_ONE_LANE, 1.0, 0.0))
    qk = _dot(hb, wqk_ref[...]) + _dot(fcat.astype(BF16), sel_ref[...])
    for h in range(FOX_HEADS):
        q_ref[0, h] = qk[:, h * LANES:(h + 1) * LANES].astype(BF16)
        k_ref[0, h] = qk[:, (FOX_HEADS + h) * LANES:(FOX_HEADS + h + 1) * LANES].astype(BF16)

    vv = _dot(hb, wv_ref[...])
    for p in range(FOX_HEADS // 2):
        blk = vv[:, p * LANES:(p + 1) * LANES]
        v_ref[0, 2 * p] = jnp.where(lane < FOX_HEAD_DIM, blk, 0.0).astype(BF16)
        v_ref[0, 2 * p + 1] = jnp.where(lane >= FOX_HEAD_DIM, blk, 0.0).astype(BF16)

    lbl = lbl_ref[...]
    e = jnp.exp(lbl - jnp.max(lbl, axis=0, keepdims=True))
    lb = jnp.sum(e[0:layer + 1, :], axis=0, keepdims=True) / jnp.sum(e, axis=0, keepdims=True)
    hh = _dot(hb, wh_ref[...])
    w = HGRN_WIDTH
    gq, gf, gi, gg = hh[:, 0:w], hh[:, w:2 * w], hh[:, 2 * w:3 * w], hh[:, 3 * w:4 * w]
    f = lb + (1.0 - lb) * jax.nn.sigmoid(gf)
    hq_ref[0] = gq * jax.nn.sigmoid(gq)
    hk_ref[0] = 1.0 - f
    hg_ref[0] = jnp.log(f)
    hi_ref[0] = gi
    hgate_ref[0] = gg * jax.nn.sigmoid(gg)


def _in_proj(x, norm_g, w_in, fox_f_bias, lb_logits, layer, tm):
    b, s, d = x.shape
    fw, hw = FOX_WIDTH, HGRN_WIDTH
    scale = 1.0 / math.sqrt(FOX_HEAD_DIM)
    wq, wk, wv = w_in[:, 0:fw] * scale, w_in[:, fw:2 * fw], w_in[:, 2 * fw:3 * fw]
    wff = w_in[:, 3 * fw:3 * fw + FOX_HEADS]
    wh = w_in[:, 3 * fw + FOX_HEADS:]

    def pad_heads(wx):
        wx = wx.reshape(d, FOX_HEADS, FOX_HEAD_DIM)
        return jnp.pad(wx, ((0, 0), (0, 0), (0, LANES - FOX_HEAD_DIM))).reshape(d, FOX_HEADS * LANES)

    wqk = jnp.concatenate([pad_heads(wq), pad_heads(wk)], axis=1).astype(BF16)
    wff = jnp.pad(wff, ((0, 0), (0, LANES - FOX_HEADS))).astype(BF16)
    fb = jnp.pad(fox_f_bias.reshape(1, FOX_HEADS), ((0, 0), (0, LANES - FOX_HEADS)))

    sel = np.zeros((LANES, 2 * FOX_HEADS * LANES), np.float32)
    for h in range(FOX_HEADS):
        for j in range(3):
            sel[j * FOX_HEADS + h, h * LANES + AUG_F + j] = 1.0
            sel[FCAT_ONE_LANE, h * LANES + AUG_ONE + j] = 1.0
            sel[FCAT_ONE_LANE, (FOX_HEADS + h) * LANES + AUG_F + j] = 1.0
            sel[j * FOX_HEADS + h, (FOX_HEADS + h) * LANES + AUG_ONE + j] = -1.0
    sel = jnp.asarray(sel, BF16)
    tri = jnp.asarray(np.tril(np.ones((tm, tm), np.float32)), BF16)

    const = lambda shape: pl.BlockSpec(shape, lambda bi, ti: (0,) * len(shape))
    head_out = jax.ShapeDtypeStruct((b, FOX_HEADS, s, LANES), BF16)
    head_spec = pl.BlockSpec((1, FOX_HEADS, tm, LANES), lambda bi, ti: (bi, 0, ti, 0))
    tok_out = jax.ShapeDtypeStruct((b, s, hw), F32)
    tok_spec = pl.BlockSpec((1, tm, hw), lambda bi, ti: (bi, ti, 0))
    return pl.pallas_call(
        functools.partial(_inproj_kernel, layer=layer),
        grid=(b, s // tm),
        in_specs=[pl.BlockSpec((1, tm, d), lambda bi, ti: (bi, ti, 0)),
                  const((1, d)), const(wqk.shape), const((d, fw)), const(wff.shape), const((d, 4 * hw)),
                  const(fb.shape), const(lb_logits.shape), const(sel.shape), const(tri.shape)],
        out_specs=[head_spec, head_spec, head_spec, tok_spec, tok_spec, tok_spec, tok_spec, tok_spec],
        out_shape=[head_out, head_out, head_out, tok_out, tok_out, tok_out, tok_out, tok_out],
        scratch_shapes=[pltpu.VMEM((8, LANES), F32)],
        compiler_params=pltpu.CompilerParams(dimension_semantics=("arbitrary", "arbitrary"),
                                             vmem_limit_bytes=VMEM_LIMIT),
        name="in_proj",
    )(x, norm_g.reshape(1, d), wqk, wv.astype(BF16), wff, wh.astype(BF16), fb, lb_logits, sel, tri)


def _fox_kernel(q_ref, k_ref, v_ref, o_ref, m_sc, l_sc, acc_sc, *, tq):
    i = pl.program_id(2)
    m_sc[...] = jnp.full_like(m_sc, -jnp.inf)
    l_sc[...] = jnp.zeros_like(l_sc)
    acc_sc[...] = jnp.zeros_like(acc_sc)

    def block(j, masked):
        off = pl.multiple_of(j * tq, tq)
        for hh in range(2):
            s = _dot_nt(q_ref[0, hh], k_ref[0, hh, pl.ds(off, tq), :])
            if masked:
                row = lax.broadcasted_iota(jnp.int32, s.shape, 0)
                col = lax.broadcasted_iota(jnp.int32, s.shape, 1)
                s = jnp.where(row >= col, s, -jnp.inf)
            m_prev = m_sc[hh]
            m_new = jnp.maximum(m_prev, jnp.max(s, axis=-1, keepdims=True))
            alpha = jnp.exp(m_prev - m_new)
            p = jnp.exp(s - m_new)
            l_sc[hh] = alpha * l_sc[hh] + jnp.sum(p, axis=-1, keepdims=True)
            acc_sc[hh] = alpha * acc_sc[hh] + _dot(p.astype(BF16), v_ref[0, hh, pl.ds(off, tq), :])
            m_sc[hh] = m_new

    def body(j, carry):
        block(j, False)
        return carry

    lax.fori_loop(0, i, body, 0)
    block(i, True)
    o_ref[0] = (acc_sc[0] / l_sc[0] + acc_sc[1] / l_sc[1]).astype(o_ref.dtype)


def _fox_attention(q, k, v, tq):
    b, hds, s, _ = q.shape
    pairs = hds // 2
    q_spec = pl.BlockSpec((1, 2, tq, LANES), lambda bi, pi, qi: (bi, pi, qi, 0))
    kv_spec = pl.BlockSpec((1, 2, s, LANES), lambda bi, pi, qi: (bi, pi, 0, 0))
    return pl.pallas_call(
        functools.partial(_fox_kernel, tq=tq),
        grid=(b, pairs, s // tq),
        in_specs=[q_spec, kv_spec, kv_spec],
        out_specs=pl.BlockSpec((1, tq, LANES), lambda bi, pi, qi: (bi, qi, pi)),
        out_shape=jax.ShapeDtypeStruct((b, s, pairs * LANES), BF16),
        scratch_shapes=[pltpu.VMEM((2, tq, 1), F32), pltpu.VMEM((2, tq, 1), F32),
                        pltpu.VMEM((2, tq, LANES), F32)],
        compiler_params=pltpu.CompilerParams(dimension_semantics=("arbitrary",) * 3,
                                             vmem_limit_bytes=VMEM_LIMIT),
        name="fox_attention",
    )(q, k, v)


def _hgrn_kernel(q_ref, k_ref, g_ref, i_ref, gate_ref, ng_ref, tri_ref, o_ref, st_sc, b_sc, o_sc, *, tc):
    @pl.when(pl.program_id(1) == 0)
    def _():
        st_sc[...] = jnp.zeros_like(st_sc)

    b_sc[...] = _tri_cumsum(tri_ref[...], g_ref[0])
    row8 = lax.broadcasted_iota(jnp.int32, (8, HGRN_DIM), 0)
    half = HGRN_STEP // 2

    def step(u, carry):
        base = pl.multiple_of(u * HGRN_STEP, HGRN_STEP)
        for h in range(HGRN_HEADS):
            ls = slice(h * HGRN_DIM, (h + 1) * HGRN_DIM)
            rows = pl.ds(base, HGRN_STEP)
            bq = b_sc[rows, ls]
            qq = q_ref[0, rows, ls]
            kk = k_ref[0, rows, ls]
            ii = i_ref[0, rows, ls]
            b_last = bq[HGRN_STEP - 1:HGRN_STEP, :]
            st = st_sc[h]
            inter = _dot_nt((qq * jnp.exp(bq)).astype(BF16), st.astype(BF16))
            k_dec = (kk * jnp.exp(b_last - bq)).astype(BF16)
            st_sc[h] = st * jnp.exp(b_last) + _dot_tn(ii.astype(BF16), k_dec)
            out = [inter[0:half], inter[half:HGRN_STEP]]
            qv = [qq[0:half], qq[half:HGRN_STEP]]
            bv = [bq[0:half], bq[half:HGRN_STEP]]
            for s_ in range(HGRN_STEP):
                k_s, b_s, i_s = kk[s_:s_ + 1, :], bq[s_:s_ + 1, :], ii[s_:s_ + 1, :]
                for v_ in range(s_ // half, 2):
                    diff = bv[v_] - b_s
                    if s_ > v_ * half:
                        diff = jnp.where(row8 + v_ * half >= s_, diff, -jnp.inf)
                    a = jnp.sum(qv[v_] * k_s * jnp.exp(diff), axis=-1, keepdims=True)
                    out[v_] = out[v_] + a * i_s
            o_sc[pl.ds(base, half), ls] = out[0]
            o_sc[pl.ds(base + half, half), ls] = out[1]
        return carry

    lax.fori_loop(0, tc // HGRN_STEP, step, 0)

    for h in range(HGRN_HEADS):
        ls = slice(h * HGRN_DIM, (h + 1) * HGRN_DIM)
        o_ref[0, :, ls] = (_rms(o_sc[:, ls], ng_ref[...]) * gate_ref[0, :, ls]).astype(o_ref.dtype)


def _hgrn(hq, hk, hg, hi, hgate, norm_g, tc):
    b, s, w = hq.shape
    blocks = np.kron(np.eye(tc // HGRN_STEP, dtype=np.float32), np.tril(np.ones((HGRN_STEP, HGRN_STEP), np.float32)))
    tri = jnp.asarray(blocks, BF16)
    tok_spec = pl.BlockSpec((1, tc, w), lambda bi, ci: (bi, ci, 0))
    return pl.pallas_call(
        functools.partial(_hgrn_kernel, tc=tc),
        grid=(b, s // tc),
        in_specs=[tok_spec] * 5 + [pl.BlockSpec((1, HGRN_DIM), lambda bi, ci: (0, 0)),
                                   pl.BlockSpec((tc, tc), lambda bi, ci: (0, 0))],
        out_specs=tok_spec,
        out_shape=jax.ShapeDtypeStruct((b, s, w), BF16),
        scratch_shapes=[pltpu.VMEM((HGRN_HEADS, HGRN_DIM, HGRN_DIM), F32),
                        pltpu.VMEM((tc, w), F32), pltpu.VMEM((tc, w), F32)],
        compiler_params=pltpu.CompilerParams(dimension_semantics=("arbitrary", "arbitrary"),
                                             vmem_limit_bytes=VMEM_LIMIT),
        name="hgrn2",
    )(hq, hk, hg, hi, hgate, norm_g.reshape(1, HGRN_DIM), tri)


def _memkv_kernel(mem_ref, g_ref, w_ref, k_ref, v_ref):
    d = mem_ref.shape[-1]
    kv = _dot(_rms(mem_ref[0], g_ref[...]).astype(BF16), w_ref[...])
    k_ref[0] = kv[:, :d].astype(BF16)
    v_ref[0] = kv[:, d:].astype(BF16)


def _mem_kv(mem, norm_g, w_kv):
    b, m, d = mem.shape
    out = jax.ShapeDtypeStruct((b, m, d), BF16)
    spec = pl.BlockSpec((1, m, d), lambda bi: (bi, 0, 0))
    return pl.pallas_call(
        _memkv_kernel,
        grid=(b,),
        in_specs=[spec, pl.BlockSpec((1, d), lambda bi: (0, 0)), pl.BlockSpec((d, 2 * d), lambda bi: (0, 0))],
        out_specs=[spec, spec],
        out_shape=[out, out],
        compiler_params=pltpu.CompilerParams(dimension_semantics=("arbitrary",), vmem_limit_bytes=VMEM_LIMIT),
        name="mem_kv",
    )(mem, norm_g.reshape(1, d), w_kv.astype(BF16))


def _mix_kernel(x_ref, fox_ref, rec_ref, wo_ref, gx_ref, wq_ref, mk_ref, mv_ref, wxo_ref, o_ref, att_sc):
    d = x_ref.shape[-1]
    fw = fox_ref.shape[-1]
    h1 = x_ref[0] + _dot(fox_ref[0], wo_ref[0:fw, :]) + _dot(rec_ref[0], wo_ref[fw:, :])
    q = _dot(_rms(h1, gx_ref[...]).astype(BF16), wq_ref[...])
    hd = d // X_HEADS
    inv = 1.0 / math.sqrt(hd)
    for h in range(X_HEADS):
        cs = slice(h * hd, (h + 1) * hd)
        s = _dot_nt(q[:, cs].astype(BF16), mk_ref[0, :, cs]) * inv
        p = jnp.exp(s - jnp.max(s, axis=-1, keepdims=True))
        p = p / jnp.sum(p, axis=-1, keepdims=True)
        att_sc[:, cs] = _dot(p.astype(BF16), mv_ref[0, :, cs]).astype(BF16)
    o_ref[0] = h1 + _dot(att_sc[...], wxo_ref[...])


def _mix(x, fox, rec, w_out, norm_g, w_xq, mem_k, mem_v, w_xo, tm):
    b, s, d = x.shape
    m = mem_k.shape[1]
    tok = lambda wd: pl.BlockSpec((1, tm, wd), lambda bi, ti: (bi, ti, 0))
    const = lambda shape: pl.BlockSpec(shape, lambda bi, ti: (0,) * len(shape))
    mem_spec = pl.BlockSpec((1, m, d), lambda bi, ti: (bi, 0, 0))
    return pl.pallas_call(
        _mix_kernel,
        grid=(b, s // tm),
        in_specs=[tok(d), tok(fox.shape[-1]), tok(rec.shape[-1]), const((d, d)), const((1, d)), const((d, d)),
                  mem_spec, mem_spec, const((d, d))],
        out_specs=tok(d),
        out_shape=jax.ShapeDtypeStruct((b, s, d), F32),
        scratch_shapes=[pltpu.VMEM((tm, d), BF16)],
        compiler_params=pltpu.CompilerParams(dimension_semantics=("arbitrary", "arbitrary"),
                                             vmem_limit_bytes=VMEM_LIMIT),
        name="mix",
    )(x, fox, rec, w_out.astype(BF16), norm_g.reshape(1, d), w_xq.astype(BF16), mem_k, mem_v, w_xo.astype(BF16))


def _mlp_kernel(h_ref, g_ref, w1_ref, w2_ref, gf_ref, o_ref, hn_sc, acc_sc):
    f = pl.program_id(1)

    @pl.when(f == 0)
    def _():
        hn_sc[...] = _rms(h_ref[...], g_ref[...]).astype(BF16)
        acc_sc[...] = jnp.zeros_like(acc_sc)

    u = jnp.maximum(_dot(hn_sc[...], w1_ref[...]), 0.0)
    acc_sc[...] += _dot((u * u).astype(BF16), w2_ref[...])

    @pl.when(f == pl.num_programs(1) - 1)
    def _():
        o_ref[...] = _rms(h_ref[...] + acc_sc[...], gf_ref[...])


def _mlp(h, norm_g, w1, w2, final_g, tm, tf):
    t, d = h.shape
    dff = w1.shape[1]
    return pl.pallas_call(
        _mlp_kernel,
        grid=(t // tm, dff // tf),
        in_specs=[pl.BlockSpec((tm, d), lambda ti, fi: (ti, 0)), pl.BlockSpec((1, d), lambda ti, fi: (0, 0)),
                  pl.BlockSpec((d, tf), lambda ti, fi: (0, fi)), pl.BlockSpec((tf, d), lambda ti, fi: (fi, 0)),
                  pl.BlockSpec((1, d), lambda ti, fi: (0, 0))],
        out_specs=pl.BlockSpec((tm, d), lambda ti, fi: (ti, 0)),
        out_shape=jax.ShapeDtypeStruct((t, d), F32),
        scratch_shapes=[pltpu.VMEM((tm, d), BF16), pltpu.VMEM((tm, d), F32)],
        compiler_params=pltpu.CompilerParams(dimension_semantics=("arbitrary", "arbitrary"),
                                             vmem_limit_bytes=VMEM_LIMIT),
        name="mlp",
    )(h, norm_g.reshape(1, d), w1.astype(BF16), w2.astype(BF16), final_g.reshape(1, d))


def _tile(n, want):
    t = min(n, want)
    assert n % t == 0, (n, want)
    return t


def kernel(x, mem, norm_mix_g, w_in, fox_f_bias, hgrn_lb_logits, hgrn_norm_g, w_out, norm_x_g, norm_mem_g,
           w_xq, w_xkv, w_xo, norm_ff_g, w1, w2, final_norm_g):
    b, s, d = x.shape
    h = x
    for l in range(w_in.shape[0]):
        q, k, v, hq, hk, hg, hi, hgate = _in_proj(h, norm_mix_g[l], w_in[l], fox_f_bias[l], hgrn_lb_logits, l,
                                                  _tile(s, 256))
        fox = _fox_attention(q, k, v, _tile(s, 256))
        rec = _hgrn(hq, hk, hg, hi, hgate, hgrn_norm_g[l], _tile(s, 256))
        mem_k, mem_v = _mem_kv(mem, norm_mem_g[l], w_xkv[l])
        h = _mix(h, fox, rec, w_out[l], norm_x_g[l], w_xq[l], mem_k, mem_v, w_xo[l], _tile(s, 256))
        is_last = l == w_in.shape[0] - 1
        assert is_last, "the MLP kernel fuses the final norm, so it must be the last layer"
        h = _mlp(h.reshape(b * s, d), norm_ff_g[l], w1[l], w2[l], final_norm_g,
                 _tile(b * s, 512), _tile(w1.shape[-1], 1024)).reshape(b, s, d)
    return h
```

```python
import functools
import math

import jax
import jax.numpy as jnp
import numpy as np
from jax import lax
from jax.experimental import pallas as pl
from jax.experimental.pallas import tpu as pltpu

EPS = 1e-6
LANES = 128
FOX_HEADS = 8
FOX_HEAD_DIM = 64
FOX_WIDTH = FOX_HEADS * FOX_HEAD_DIM
HGRN_HEADS = 4
HGRN_DIM = 128
HGRN_WIDTH = HGRN_HEADS * HGRN_DIM
X_HEADS = 4
HGRN_STEP = 16
VMEM_LIMIT = 56 * 1024 * 1024

AUG_F = FOX_HEAD_DIM
AUG_ONE = FOX_HEAD_DIM + 3
FCAT_ONE_LANE = 24

BF16 = jnp.bfloat16
F32 = jnp.float32


def _dot(a, b):
    return jnp.dot(a, b, preferred_element_type=F32)


def _dot_nt(a, b):
    return lax.dot_general(a, b, (((1,), (1,)), ((), ())), preferred_element_type=F32)


def _dot_tn(a, b):
    return lax.dot_general(a, b, (((0,), (0,)), ((), ())), preferred_element_type=F32)


def _split3(v):
    hi = v.astype(BF16)
    r1 = v - hi.astype(F32)
    mid = r1.astype(BF16)
    lo = (r1 - mid.astype(F32)).astype(BF16)
    return hi, mid, lo


def _tri_cumsum(tri, v):
    hi, mid, lo = _split3(v)
    return _dot(tri, hi) + _dot(tri, mid) + _dot(tri, lo)


def _rms(x, g):
    ms = jnp.mean(x * x, axis=-1, keepdims=True)
    return x * lax.rsqrt(ms + EPS) * g


def _inproj_kernel(x_ref, g_ref, wqk_ref, wvt_ref, wff_ref, wh_ref, fb_ref, lbl_ref, sel_ref, tri_ref,
                   q_ref, k_ref, vt_ref, hq_ref, hk_ref, hg_ref, hi_ref, hgate_ref, carry_ref, *, layer):
    @pl.when(pl.program_id(1) == 0)
    def _():
        carry_ref[...] = jnp.zeros_like(carry_ref)

    hb = _rms(x_ref[0], g_ref[...]).astype(BF16)
    tm = hb.shape[0]
    lane = lax.broadcasted_iota(jnp.int32, (tm, LANES), 1)

    z = _dot(hb, wff_ref[...]) + fb_ref[...]
    logf = jnp.minimum(z, 0.0) - jnp.log(1.0 + jnp.exp(-jnp.abs(z)))
    logf = jnp.where(lane < FOX_HEADS, logf, 0.0)
    fcum = _tri_cumsum(tri_ref[...], logf) + carry_ref[0:1, :]
    carry_ref[...] = jnp.broadcast_to(fcum[tm - 1:tm, :], carry_ref.shape)

    f_hi = fcum.astype(BF16).astype(F32)
    r1 = fcum - f_hi
    f_mid = r1.astype(BF16).astype(F32)
    f_lo = (r1 - f_mid).astype(BF16).astype(F32)
    fcat = (f_hi + pltpu.roll(f_mid, FOX_HEADS, 1) + pltpu.roll(f_lo, 2 * FOX_HEADS, 1)
            + jnp.where(lane == FCAT_ONE_LANE, 1.0, 0.0))
    qk = _dot(hb, wqk_ref[...]) + _dot(fcat.astype(BF16), sel_ref[...])
    for h in range(FOX_HEADS):
        q_ref[0, h] = qk[:, h * LANES:(h + 1) * LANES].astype(BF16)
        k_ref[0, h] = qk[:, (FOX_HEADS + h) * LANES:(FOX_HEADS + h + 1) * LANES].astype(BF16)

    vt_ref[0] = _dot_nt(wvt_ref[...], hb).astype(BF16)

    lbl = lbl_ref[...]
    e = jnp.exp(lbl - jnp.max(lbl, axis=0, keepdims=True))
    lb = jnp.sum(e[0:layer + 1, :], axis=0, keepdims=True) / jnp.sum(e, axis=0, keepdims=True)
    hh = _dot(hb, wh_ref[...])
    w = HGRN_WIDTH
    gq, gf, gi, gg = hh[:, 0:w], hh[:, w:2 * w], hh[:, 2 * w:3 * w], hh[:, 3 * w:4 * w]
    f = lb + (1.0 - lb) * jax.nn.sigmoid(gf)
    hq_ref[0] = gq * jax.nn.sigmoid(gq)
    hk_ref[0] = 1.0 - f
    hg_ref[0] = jnp.log(f)
    hi_ref[0] = gi
    hgate_ref[0] = gg * jax.nn.sigmoid(gg)


def _in_proj(x, norm_g, w_in, fox_f_bias, lb_logits, layer, tm):
    b, s, d = x.shape
    fw, hw = FOX_WIDTH, HGRN_WIDTH
    scale = 1.0 / math.sqrt(FOX_HEAD_DIM)
    wq, wk, wv = w_in[:, 0:fw] * scale, w_in[:, fw:2 * fw], w_in[:, 2 * fw:3 * fw]
    wff = w_in[:, 3 * fw:3 * fw + FOX_HEADS]
    wh = w_in[:, 3 * fw + FOX_HEADS:]

    def pad_heads(wx):
        wx = wx.reshape(d, FOX_HEADS, FOX_HEAD_DIM)
        return jnp.pad(wx, ((0, 0), (0, 0), (0, LANES - FOX_HEAD_DIM))).reshape(d, FOX_HEADS * LANES)

    wqk = jnp.concatenate([pad_heads(wq), pad_heads(wk)], axis=1).astype(BF16)
    wff = jnp.pad(wff, ((0, 0), (0, LANES - FOX_HEADS))).astype(BF16)
    fb = jnp.pad(fox_f_bias.reshape(1, FOX_HEADS), ((0, 0), (0, LANES - FOX_HEADS)))

    sel = np.zeros((LANES, 2 * FOX_HEADS * LANES), np.float32)
    for h in range(FOX_HEADS):
        for j in range(3):
            sel[j * FOX_HEADS + h, h * LANES + AUG_F + j] = 1.0
            sel[FCAT_ONE_LANE, h * LANES + AUG_ONE + j] = 1.0
            sel[FCAT_ONE_LANE, (FOX_HEADS + h) * LANES + AUG_F + j] = 1.0
            sel[j * FOX_HEADS + h, (FOX_HEADS + h) * LANES + AUG_ONE + j] = -1.0
    sel = jnp.asarray(sel, BF16)
    tri = jnp.asarray(np.tril(np.ones((tm, tm), np.float32)), BF16)

    const = lambda shape: pl.BlockSpec(shape, lambda bi, ti: (0,) * len(shape))
    head_out = jax.ShapeDtypeStruct((b, FOX_HEADS, s, LANES), BF16)
    head_spec = pl.BlockSpec((1, FOX_HEADS, tm, LANES), lambda bi, ti: (bi, 0, ti, 0))
    tok_out = jax.ShapeDtypeStruct((b, s, hw), F32)
    tok_spec = pl.BlockSpec((1, tm, hw), lambda bi, ti: (bi, ti, 0))
    return pl.pallas_call(
        functools.partial(_inproj_kernel, layer=layer),
        grid=(b, s // tm),
        in_specs=[pl.BlockSpec((1, tm, d), lambda bi, ti: (bi, ti, 0)),
                  const((1, d)), const(wqk.shape), const((fw, d)), const(wff.shape), const((d, 4 * hw)),
                  const(fb.shape), const(lb_logits.shape), const(sel.shape), const(tri.shape)],
        out_specs=[head_spec, head_spec, pl.BlockSpec((1, fw, tm), lambda bi, ti: (bi, 0, ti)),
                   tok_spec, tok_spec, tok_spec, tok_spec, tok_spec],
        out_shape=[head_out, head_out, jax.ShapeDtypeStruct((b, fw, s), BF16),
                   tok_out, tok_out, tok_out, tok_out, tok_out],
        scratch_shapes=[pltpu.VMEM((8, LANES), F32)],
        compiler_params=pltpu.CompilerParams(dimension_semantics=("arbitrary", "arbitrary"),
                                             vmem_limit_bytes=VMEM_LIMIT),
        name="in_proj",
    )(x, norm_g.reshape(1, d), wqk, wv.T.astype(BF16), wff, wh.astype(BF16), fb, lb_logits, sel, tri)


def _fox_kernel(q_ref, k_ref, vt_ref, o_ref, m_sc, l_sc, acc_sc, *, tq, hg):
    i = pl.program_id(2)
    hd = FOX_HEAD_DIM
    m_sc[...] = jnp.full_like(m_sc, -jnp.inf)
    l_sc[...] = jnp.zeros_like(l_sc)
    acc_sc[...] = jnp.zeros_like(acc_sc)

    def block(j, masked):
        off = pl.multiple_of(j * tq, tq)
        for hh in range(hg):
            st = _dot_nt(k_ref[0, hh, pl.ds(off, tq), :], q_ref[0, hh])
            if masked:
                key = lax.broadcasted_iota(jnp.int32, st.shape, 0)
                qry = lax.broadcasted_iota(jnp.int32, st.shape, 1)
                st = jnp.where(key <= qry, st, -jnp.inf)
            m_prev = m_sc[hh]
            m_new = jnp.maximum(m_prev, jnp.max(st, axis=0, keepdims=True))
            alpha = jnp.exp(m_prev - m_new)
            p = jnp.exp(st - m_new)
            l_sc[hh] = alpha * l_sc[hh] + jnp.sum(p, axis=0, keepdims=True)
            vt = vt_ref[0, hh * hd:(hh + 1) * hd, pl.ds(off, tq)]
            acc_sc[hh] = alpha * acc_sc[hh] + _dot(vt, p.astype(BF16))
            m_sc[hh] = m_new

    def body(j, carry):
        block(j, False)
        return carry

    lax.fori_loop(0, i, body, 0)
    block(i, True)
    for hh in range(hg):
        o_ref[0, hh * hd:(hh + 1) * hd, :] = (acc_sc[hh] / l_sc[hh]).astype(o_ref.dtype)


def _fox_attention(q, k, vt, tq, hg):
    b, hds, s, _ = q.shape
    groups = hds // hg
    rows = hg * FOX_HEAD_DIM
    return pl.pallas_call(
        functools.partial(_fox_kernel, tq=tq, hg=hg),
        grid=(b, groups, s // tq),
        in_specs=[pl.BlockSpec((1, hg, tq, LANES), lambda bi, pi, qi: (bi, pi, qi, 0)),
                  pl.BlockSpec((1, hg, s, LANES), lambda bi, pi, qi: (bi, pi, 0, 0)),
                  pl.BlockSpec((1, rows, s), lambda bi, pi, qi: (bi, pi, 0))],
        out_specs=pl.BlockSpec((1, rows, tq), lambda bi, pi, qi: (bi, pi, qi)),
        out_shape=jax.ShapeDtypeStruct((b, groups * rows, s), BF16),
        scratch_shapes=[pltpu.VMEM((hg, 1, tq), F32), pltpu.VMEM((hg, 1, tq), F32),
                        pltpu.VMEM((hg, FOX_HEAD_DIM, tq), F32)],
        compiler_params=pltpu.CompilerParams(dimension_semantics=("arbitrary",) * 3,
                                             vmem_limit_bytes=VMEM_LIMIT),
        name="fox_attention",
    )(q, k, vt)


def _hgrn_kernel(q_ref, k_ref, g_ref, i_ref, gate_ref, ng_ref, tri_ref, o_ref, st_sc, b_sc, o_sc, *, tc):
    @pl.when(pl.program_id(1) == 0)
    def _():
        st_sc[...] = jnp.zeros_like(st_sc)

    b_sc[...] = _tri_cumsum(tri_ref[...], g_ref[0])
    row8 = lax.broadcasted_iota(jnp.int32, (8, HGRN_DIM), 0)
    half = HGRN_STEP // 2

    def step(u, carry):
        base = pl.multiple_of(u * HGRN_STEP, HGRN_STEP)
        for h in range(HGRN_HEADS):
            ls = slice(h * HGRN_DIM, (h + 1) * HGRN_DIM)
            rows = pl.ds(base, HGRN_STEP)
            bq = b_sc[rows, ls]
            qq = q_ref[0, rows, ls]
            kk = k_ref[0, rows, ls]
            ii = i_ref[0, rows, ls]
            b_last = bq[HGRN_STEP - 1:HGRN_STEP, :]
            st = st_sc[h]
            inter = _dot_nt((qq * jnp.exp(bq)).astype(BF16), st.astype(BF16))
            k_dec = (kk * jnp.exp(b_last - bq)).astype(BF16)
            st_sc[h] = st * jnp.exp(b_last) + _dot_tn(ii.astype(BF16), k_dec)
            out = [inter[0:half], inter[half:HGRN_STEP]]
            qv = [qq[0:half], qq[half:HGRN_STEP]]
            bv = [bq[0:half], bq[half:HGRN_STEP]]
            for s_ in range(HGRN_STEP):
                k_s, b_s, i_s = kk[s_:s_ + 1, :], bq[s_:s_ + 1, :], ii[s_:s_ + 1, :]
                for v_ in range(s_ // half, 2):
                    diff = bv[v_] - b_s
                    if s_ > v_ * half:
                        diff = jnp.where(row8 + v_ * half >= s_, diff, -jnp.inf)
                    a = jnp.sum(qv[v_] * k_s * jnp.exp(diff), axis=-1, keepdims=True)
                    out[v_] = out[v_] + a * i_s
            o_sc[pl.ds(base, half), ls] = out[0]
            o_sc[pl.ds(base + half, half), ls] = out[1]
        return carry

    lax.fori_loop(0, tc // HGRN_STEP, step, 0)

    for h in range(HGRN_HEADS):
        ls = slice(h * HGRN_DIM, (h + 1) * HGRN_DIM)
        o_ref[0, :, ls] = (_rms(o_sc[:, ls], ng_ref[...]) * gate_ref[0, :, ls]).astype(o_ref.dtype)


def _hgrn(hq, hk, hg, hi, hgate, norm_g, tc):
    b, s, w = hq.shape
    blocks = np.kron(np.eye(tc // HGRN_STEP, dtype=np.float32), np.tril(np.ones((HGRN_STEP, HGRN_STEP), np.float32)))
    tri = jnp.asarray(blocks, BF16)
    tok_spec = pl.BlockSpec((1, tc, w), lambda bi, ci: (bi, ci, 0))
    return pl.pallas_call(
        functools.partial(_hgrn_kernel, tc=tc),
        grid=(b, s // tc),
        in_specs=[tok_spec] * 5 + [pl.BlockSpec((1, HGRN_DIM), lambda bi, ci: (0, 0)),
                                   pl.BlockSpec((tc, tc), lambda bi, ci: (0, 0))],
        out_specs=tok_spec,
        out_shape=jax.ShapeDtypeStruct((b, s, w), BF16),
        scratch_shapes=[pltpu.VMEM((HGRN_HEADS, HGRN_DIM, HGRN_DIM), F32),
                        pltpu.VMEM((tc, w), F32), pltpu.VMEM((tc, w), F32)],
        compiler_params=pltpu.CompilerParams(dimension_semantics=("arbitrary", "arbitrary"),
                                             vmem_limit_bytes=VMEM_LIMIT),
        name="hgrn2",
    )(hq, hk, hg, hi, hgate, norm_g.reshape(1, HGRN_DIM), tri)


def _memkv_kernel(mem_ref, g_ref, w_ref, k_ref, v_ref):
    d = mem_ref.shape[-1]
    kv = _dot(_rms(mem_ref[0], g_ref[...]).astype(BF16), w_ref[...])
    k_ref[0] = kv[:, :d].astype(BF16)
    v_ref[0] = kv[:, d:].astype(BF16)


def _mem_kv(mem, norm_g, w_kv):
    b, m, d = mem.shape
    out = jax.ShapeDtypeStruct((b, m, d), BF16)
    spec = pl.BlockSpec((1, m, d), lambda bi: (bi, 0, 0))
    return pl.pallas_call(
        _memkv_kernel,
        grid=(b,),
        in_specs=[spec, pl.BlockSpec((1, d), lambda bi: (0, 0)), pl.BlockSpec((d, 2 * d), lambda bi: (0, 0))],
        out_specs=[spec, spec],
        out_shape=[out, out],
        compiler_params=pltpu.CompilerParams(dimension_semantics=("arbitrary",), vmem_limit_bytes=VMEM_LIMIT),
        name="mem_kv",
    )(mem, norm_g.reshape(1, d), w_kv.astype(BF16))


def _mix_kernel(x_ref, foxt_ref, rec_ref, wo_ref, gx_ref, wq_ref, mk_ref, mv_ref, wxo_ref, o_ref, att_sc):
    d = x_ref.shape[-1]
    fw = foxt_ref.shape[1]
    h1 = x_ref[0] + _dot_tn(foxt_ref[0], wo_ref[0:fw, :]) + _dot(rec_ref[0], wo_ref[fw:, :])
    q = _dot(_rms(h1, gx_ref[...]).astype(BF16), wq_ref[...])
    hd = d // X_HEADS
    inv = 1.0 / math.sqrt(hd)
    for h in range(X_HEADS):
        cs = slice(h * hd, (h + 1) * hd)
        s = _dot_nt(q[:, cs].astype(BF16), mk_ref[0, :, cs]) * inv
        p = jnp.exp(s - jnp.max(s, axis=-1, keepdims=True))
        p = p / jnp.sum(p, axis=-1, keepdims=True)
        att_sc[:, cs] = _dot(p.astype(BF16), mv_ref[0, :, cs]).astype(BF16)
    o_ref[0] = h1 + _dot(att_sc[...], wxo_ref[...])


def _mix(x, foxt, rec, w_out, norm_g, w_xq, mem_k, mem_v, w_xo, tm):
    b, s, d = x.shape
    m = mem_k.shape[1]
    fw = foxt.shape[1]
    tok = lambda wd: pl.BlockSpec((1, tm, wd), lambda bi, ti: (bi, ti, 0))
    const = lambda shape: pl.BlockSpec(shape, lambda bi, ti: (0,) * len(shape))
    mem_spec = pl.BlockSpec((1, m, d), lambda bi, ti: (bi, 0, 0))
    return pl.pallas_call(
        _mix_kernel,
        grid=(b, s // tm),
        in_specs=[tok(d), pl.BlockSpec((1, fw, tm), lambda bi, ti: (bi, 0, ti)), tok(rec.shape[-1]),
                  const((d, d)), const((1, d)), const((d, d)), mem_spec, mem_spec, const((d, d))],
        out_specs=tok(d),
        out_shape=jax.ShapeDtypeStruct((b, s, d), F32),
        scratch_shapes=[pltpu.VMEM((tm, d), BF16)],
        compiler_params=pltpu.CompilerParams(dimension_semantics=("arbitrary", "arbitrary"),
                                             vmem_limit_bytes=VMEM_LIMIT),
        name="mix",
    )(x, foxt, rec, w_out.astype(BF16), norm_g.reshape(1, d), w_xq.astype(BF16), mem_k, mem_v, w_xo.astype(BF16))


def _mlp_kernel(h_ref, g_ref, w1_ref, w2_ref, gf_ref, o_ref, hn_sc, acc_sc):
    f = pl.program_id(1)

    @pl.when(f == 0)
    def _():
        hn_sc[...] = _rms(h_ref[...], g_ref[...]).astype(BF16)
        acc_sc[...] = jnp.zeros_like(acc_sc)

    u = jnp.maximum(_dot(hn_sc[...], w1_ref[...]), 0.0)
    acc_sc[...] += _dot((u * u).astype(BF16), w2_ref[...])

    @pl.when(f == pl.num_programs(1) - 1)
    def _():
        o_ref[...] = _rms(h_ref[...] + acc_sc[...], gf_ref[...])


def _mlp(h, norm_g, w1, w2, final_g, tm, tf):
    t, d = h.shape
    dff = w1.shape[1]
    return pl.pallas_call(
        _mlp_kernel,
        grid=(t // tm, dff // tf),
        in_specs=[pl.BlockSpec((tm, d), lambda ti, fi: (ti, 0)), pl.BlockSpec((1, d), lambda ti, fi: (0, 0)),
                  pl.BlockSpec((d, tf), lambda ti, fi: (0, fi)), pl.BlockSpec((tf, d), lambda ti, fi: (fi, 0)),
                  pl.BlockSpec((1, d), lambda ti, fi: (0, 0))],
        out_specs=pl.BlockSpec((tm, d), lambda ti, fi: (ti, 0)),
        out_shape=jax.ShapeDtypeStruct((t, d), F32),
        scratch_shapes=[pltpu.VMEM((tm, d), BF16), pltpu.VMEM((tm, d), F32)],
        compiler_params=pltpu.CompilerParams(dimension_semantics=("arbitrary", "arbitrary"),
                                             vmem_limit_bytes=VMEM_LIMIT),
        name="mlp",
    )(h, norm_g.reshape(1, d), w1.astype(BF16), w2.astype(BF16), final_g.reshape(1, d))


def _tile(n, want):
    t = min(n, want)
    assert n % t == 0, (n, want)
    return t


def kernel(x, mem, norm_mix_g, w_in, fox_f_bias, hgrn_lb_logits, hgrn_norm_g, w_out, norm_x_g, norm_mem_g,
           w_xq, w_xkv, w_xo, norm_ff_g, w1, w2, final_norm_g):
    b, s, d = x.shape
    h = x
    for l in range(w_in.shape[0]):
        q, k, v, hq, hk, hg, hi, hgate = _in_proj(h, norm_mix_g[l], w_in[l], fox_f_bias[l], hgrn_lb_logits, l,
                                                  _tile(s, 256))
        fox = _fox_attention(q, k, v, _tile(s, 512), 4)
        rec = _hgrn(hq, hk, hg, hi, hgate, hgrn_norm_g[l], _tile(s, 256))
        mem_k, mem_v = _mem_kv(mem, norm_mem_g[l], w_xkv[l])
        h = _mix(h, fox, rec, w_out[l], norm_x_g[l], w_xq[l], mem_k, mem_v, w_xo[l], _tile(s, 256))
        is_last = l == w_in.shape[0] - 1
        assert is_last, "the MLP kernel fuses the final norm, so it must be the last layer"
        h = _mlp(h.reshape(b * s, d), norm_ff_g[l], w1[l], w2[l], final_norm_g,
                 _tile(b * s, 512), _tile(w1.shape[-1], 1024)).reshape(b, s, d)
    return h
```

```python
import functools
import math

import jax
import jax.numpy as jnp
import numpy as np
from jax import lax
from jax.experimental import pallas as pl
from jax.experimental.pallas import tpu as pltpu

EPS = 1e-6
LOG2E = math.log2(math.e)
LANES = 128
FOX_HEADS = 8
FOX_HEAD_DIM = 64
FOX_WIDTH = FOX_HEADS * FOX_HEAD_DIM
HGRN_HEADS = 4
HGRN_DIM = 128
HGRN_WIDTH = HGRN_HEADS * HGRN_DIM
X_HEADS = 4
HGRN_STEP = 16
VMEM_LIMIT = 56 * 1024 * 1024

AUG_F = FOX_HEAD_DIM
AUG_ONE = FOX_HEAD_DIM + 3
FCAT_ONE_LANE = 24

BF16 = jnp.bfloat16
F32 = jnp.float32


def _dot(a, b):
    return jnp.dot(a, b, preferred_element_type=F32)


def _dot_nt(a, b):
    return lax.dot_general(a, b, (((1,), (1,)), ((), ())), preferred_element_type=F32)


def _dot_tn(a, b):
    return lax.dot_general(a, b, (((0,), (0,)), ((), ())), preferred_element_type=F32)


def _split3(v):
    hi = v.astype(BF16)
    r1 = v - hi.astype(F32)
    mid = r1.astype(BF16)
    lo = (r1 - mid.astype(F32)).astype(BF16)
    return hi, mid, lo


def _tri_cumsum(tri, v):
    hi, mid, lo = _split3(v)
    return _dot(tri, hi) + _dot(tri, mid) + _dot(tri, lo)


def _rms(x, g):
    ms = jnp.mean(x * x, axis=-1, keepdims=True)
    return x * lax.rsqrt(ms + EPS) * g


def _inproj_kernel(x_ref, g_ref, wqk_ref, wvt_ref, wff_ref, wh_ref, fb_ref, lbl_ref, sel_ref, tri_ref,
                   q_ref, k_ref, vt_ref, hq_ref, hk_ref, hg_ref, hi_ref, hgate_ref, carry_ref, *, layer):
    @pl.when(pl.program_id(1) == 0)
    def _():
        carry_ref[...] = jnp.zeros_like(carry_ref)

    hb = _rms(x_ref[0], g_ref[...]).astype(BF16)
    tm = hb.shape[0]
    lane = lax.broadcasted_iota(jnp.int32, (tm, LANES), 1)

    z = _dot(hb, wff_ref[...]) + fb_ref[...]
    logf = jnp.minimum(z, 0.0) - jnp.log(1.0 + jnp.exp(-jnp.abs(z)))
    logf = jnp.where(lane < FOX_HEADS, logf, 0.0)
    fcum = _tri_cumsum(tri_ref[...], logf) + carry_ref[0:1, :]
    carry_ref[...] = jnp.broadcast_to(fcum[tm - 1:tm, :], carry_ref.shape)

    fsc = fcum * LOG2E
    f_hi = fsc.astype(BF16).astype(F32)
    r1 = fsc - f_hi
    f_mid = r1.astype(BF16).astype(F32)
    f_lo = (r1 - f_mid).astype(BF16).astype(F32)
    fcat = (f_hi + pltpu.roll(f_mid, FOX_HEADS, 1) + pltpu.roll(f_lo, 2 * FOX_HEADS, 1)
            + jnp.where(lane == FCAT_ONE_LANE, 1.0, 0.0))
    qk = _dot(hb, wqk_ref[...]) + _dot(fcat.astype(BF16), sel_ref[...])
    for h in range(FOX_HEADS):
        q_ref[0, h] = qk[:, h * LANES:(h + 1) * LANES].astype(BF16)
        k_ref[0, h] = qk[:, (FOX_HEADS + h) * LANES:(FOX_HEADS + h + 1) * LANES].astype(BF16)

    vt_ref[0] = _dot_nt(wvt_ref[...], hb).astype(BF16)

    lbl = lbl_ref[...]
    e = jnp.exp(lbl - jnp.max(lbl, axis=0, keepdims=True))
    lb = jnp.sum(e[0:layer + 1, :], axis=0, keepdims=True) / jnp.sum(e, axis=0, keepdims=True)
    hh = _dot(hb, wh_ref[...])
    w = HGRN_WIDTH
    gq, gf, gi, gg = hh[:, 0:w], hh[:, w:2 * w], hh[:, 2 * w:3 * w], hh[:, 3 * w:4 * w]
    f = lb + (1.0 - lb) * jax.nn.sigmoid(gf)
    hq_ref[0] = gq * jax.nn.sigmoid(gq)
    hk_ref[0] = 1.0 - f
    hg_ref[0] = jnp.log(f)
    hi_ref[0] = gi
    hgate_ref[0] = gg * jax.nn.sigmoid(gg)


def _in_proj(x, norm_g, w_in, fox_f_bias, lb_logits, layer, tm):
    b, s, d = x.shape
    fw, hw = FOX_WIDTH, HGRN_WIDTH
    scale = LOG2E / math.sqrt(FOX_HEAD_DIM)
    wq, wk, wv = w_in[:, 0:fw] * scale, w_in[:, fw:2 * fw], w_in[:, 2 * fw:3 * fw]
    wff = w_in[:, 3 * fw:3 * fw + FOX_HEADS]
    wh = w_in[:, 3 * fw + FOX_HEADS:]

    def pad_heads(wx):
        wx = wx.reshape(d, FOX_HEADS, FOX_HEAD_DIM)
        return jnp.pad(wx, ((0, 0), (0, 0), (0, LANES - FOX_HEAD_DIM))).reshape(d, FOX_HEADS * LANES)

    wqk = jnp.concatenate([pad_heads(wq), pad_heads(wk)], axis=1).astype(BF16)
    wff = jnp.pad(wff, ((0, 0), (0, LANES - FOX_HEADS))).astype(BF16)
    fb = jnp.pad(fox_f_bias.reshape(1, FOX_HEADS), ((0, 0), (0, LANES - FOX_HEADS)))

    sel = np.zeros((LANES, 2 * FOX_HEADS * LANES), np.float32)
    for h in range(FOX_HEADS):
        for j in range(3):
            sel[j * FOX_HEADS + h, h * LANES + AUG_F + j] = 1.0
            sel[FCAT_ONE_LANE, h * LANES + AUG_ONE + j] = 1.0
            sel[FCAT_ONE_LANE, (FOX_HEADS + h) * LANES + AUG_F + j] = 1.0
            sel[j * FOX_HEADS + h, (FOX_HEADS + h) * LANES + AUG_ONE + j] = -1.0
    sel = jnp.asarray(sel, BF16)
    tri = jnp.asarray(np.tril(np.ones((tm, tm), np.float32)), BF16)

    const = lambda shape: pl.BlockSpec(shape, lambda bi, ti: (0,) * len(shape))
    head_out = jax.ShapeDtypeStruct((b, FOX_HEADS, s, LANES), BF16)
    head_spec = pl.BlockSpec((1, FOX_HEADS, tm, LANES), lambda bi, ti: (bi, 0, ti, 0))
    tok_out = jax.ShapeDtypeStruct((b, s, hw), F32)
    tok_spec = pl.BlockSpec((1, tm, hw), lambda bi, ti: (bi, ti, 0))
    return pl.pallas_call(
        functools.partial(_inproj_kernel, layer=layer),
        grid=(b, s // tm),
        in_specs=[pl.BlockSpec((1, tm, d), lambda bi, ti: (bi, ti, 0)),
                  const((1, d)), const(wqk.shape), const((fw, d)), const(wff.shape), const((d, 4 * hw)),
                  const(fb.shape), const(lb_logits.shape), const(sel.shape), const(tri.shape)],
        out_specs=[head_spec, head_spec, pl.BlockSpec((1, fw, tm), lambda bi, ti: (bi, 0, ti)),
                   tok_spec, tok_spec, tok_spec, tok_spec, tok_spec],
        out_shape=[head_out, head_out, jax.ShapeDtypeStruct((b, fw, s), BF16),
                   tok_out, tok_out, tok_out, tok_out, tok_out],
        scratch_shapes=[pltpu.VMEM((8, LANES), F32)],
        compiler_params=pltpu.CompilerParams(dimension_semantics=("arbitrary", "arbitrary"),
                                             vmem_limit_bytes=VMEM_LIMIT),
        name="in_proj",
    )(x, norm_g.reshape(1, d), wqk, wv.T.astype(BF16), wff, wh.astype(BF16), fb, lb_logits, sel, tri)


def _fox_kernel(q_ref, k_ref, vt_ref, o_ref, m_sc, l_sc, acc_sc, *, tq, hg):
    i = pl.program_id(2)
    hd = FOX_HEAD_DIM
    m_sc[...] = jnp.full_like(m_sc, -jnp.inf)
    l_sc[...] = jnp.zeros_like(l_sc)
    acc_sc[...] = jnp.zeros_like(acc_sc)

    def block(j, masked):
        off = pl.multiple_of(j * tq, tq)
        scores = lambda h_: _dot_nt(k_ref[0, h_, pl.ds(off, tq), :], q_ref[0, h_])
        st_next = scores(0)
        for hh in range(hg):
            st = st_next
            if hh + 1 < hg:
                st_next = scores(hh + 1)
            if masked:
                key = lax.broadcasted_iota(jnp.int32, st.shape, 0)
                qry = lax.broadcasted_iota(jnp.int32, st.shape, 1)
                st = jnp.where(key <= qry, st, -jnp.inf)
            m_prev = m_sc[hh]
            m_new = jnp.maximum(m_prev, jnp.max(st, axis=0, keepdims=True))
            alpha = jnp.exp2(m_prev - m_new)
            p = jnp.exp2(st - m_new)
            l_sc[hh] = alpha * l_sc[hh] + jnp.sum(p, axis=0, keepdims=True)
            vt = vt_ref[0, hh * hd:(hh + 1) * hd, pl.ds(off, tq)]
            acc_sc[hh] = alpha * acc_sc[hh] + _dot(vt, p.astype(BF16))
            m_sc[hh] = m_new

    def body(j, carry):
        block(j, False)
        return carry

    lax.fori_loop(0, i, body, 0)
    block(i, True)
    for hh in range(hg):
        o_ref[0, hh * hd:(hh + 1) * hd, :] = (acc_sc[hh] / l_sc[hh]).astype(o_ref.dtype)


def _fox_attention(q, k, vt, tq, hg):
    b, hds, s, _ = q.shape
    groups = hds // hg
    rows = hg * FOX_HEAD_DIM
    return pl.pallas_call(
        functools.partial(_fox_kernel, tq=tq, hg=hg),
        grid=(b, groups, s // tq),
        in_specs=[pl.BlockSpec((1, hg, tq, LANES), lambda bi, pi, qi: (bi, pi, qi, 0)),
                  pl.BlockSpec((1, hg, s, LANES), lambda bi, pi, qi: (bi, pi, 0, 0)),
                  pl.BlockSpec((1, rows, s), lambda bi, pi, qi: (bi, pi, 0))],
        out_specs=pl.BlockSpec((1, rows, tq), lambda bi, pi, qi: (bi, pi, qi)),
        out_shape=jax.ShapeDtypeStruct((b, groups * rows, s), BF16),
        scratch_shapes=[pltpu.VMEM((hg, 1, tq), F32), pltpu.VMEM((hg, 1, tq), F32),
                        pltpu.VMEM((hg, FOX_HEAD_DIM, tq), F32)],
        compiler_params=pltpu.CompilerParams(dimension_semantics=("arbitrary",) * 3,
                                             vmem_limit_bytes=VMEM_LIMIT),
        name="fox_attention",
    )(q, k, vt)


def _hgrn_kernel(q_ref, k_ref, g_ref, i_ref, gate_ref, ng_ref, tri_ref, o_ref, st_sc, b_sc, o_sc, *, tc):
    @pl.when(pl.program_id(1) == 0)
    def _():
        st_sc[...] = jnp.zeros_like(st_sc)

    b_sc[...] = _tri_cumsum(tri_ref[...], g_ref[0])
    row8 = lax.broadcasted_iota(jnp.int32, (8, HGRN_DIM), 0)
    half = HGRN_STEP // 2

    def step(u, carry):
        base = pl.multiple_of(u * HGRN_STEP, HGRN_STEP)
        for h in range(HGRN_HEADS):
            ls = slice(h * HGRN_DIM, (h + 1) * HGRN_DIM)
            rows = pl.ds(base, HGRN_STEP)
            bq = b_sc[rows, ls]
            qq = q_ref[0, rows, ls]
            kk = k_ref[0, rows, ls]
            ii = i_ref[0, rows, ls]
            b_last = bq[HGRN_STEP - 1:HGRN_STEP, :]
            st = st_sc[h]
            inter = _dot_nt((qq * jnp.exp(bq)).astype(BF16), st.astype(BF16))
            k_dec = (kk * jnp.exp(b_last - bq)).astype(BF16)
            st_sc[h] = st * jnp.exp(b_last) + _dot_tn(ii.astype(BF16), k_dec)
            out = [inter[0:half], inter[half:HGRN_STEP]]
            qv = [qq[0:half], qq[half:HGRN_STEP]]
            bv = [bq[0:half], bq[half:HGRN_STEP]]
            for s_ in range(HGRN_STEP):
                k_s, b_s, i_s = kk[s_:s_ + 1, :], bq[s_:s_ + 1, :], ii[s_:s_ + 1, :]
                for v_ in range(s_ // half, 2):
                    diff = bv[v_] - b_s
                    if s_ > v_ * half:
                        diff = jnp.where(row8 + v_ * half >= s_, diff, -jnp.inf)
                    a = jnp.sum(qv[v_] * k_s * jnp.exp(diff), axis=-1, keepdims=True)
                    out[v_] = out[v_] + a * i_s
            o_sc[pl.ds(base, half), ls] = out[0]
            o_sc[pl.ds(base + half, half), ls] = out[1]
        return carry

    lax.fori_loop(0, tc // HGRN_STEP, step, 0)

    for h in range(HGRN_HEADS):
        ls = slice(h * HGRN_DIM, (h + 1) * HGRN_DIM)
        o_ref[0, :, ls] = (_rms(o_sc[:, ls], ng_ref[...]) * gate_ref[0, :, ls]).astype(o_ref.dtype)


def _hgrn(hq, hk, hg, hi, hgate, norm_g, tc):
    b, s, w = hq.shape
    blocks = np.kron(np.eye(tc // HGRN_STEP, dtype=np.float32), np.tril(np.ones((HGRN_STEP, HGRN_STEP), np.float32)))
    tri = jnp.asarray(blocks, BF16)
    tok_spec = pl.BlockSpec((1, tc, w), lambda bi, ci: (bi, ci, 0))
    return pl.pallas_call(
        functools.partial(_hgrn_kernel, tc=tc),
        grid=(b, s // tc),
        in_specs=[tok_spec] * 5 + [pl.BlockSpec((1, HGRN_DIM), lambda bi, ci: (0, 0)),
                                   pl.BlockSpec((tc, tc), lambda bi, ci: (0, 0))],
        out_specs=tok_spec,
        out_shape=jax.ShapeDtypeStruct((b, s, w), BF16),
        scratch_shapes=[pltpu.VMEM((HGRN_HEADS, HGRN_DIM, HGRN_DIM), F32),
                        pltpu.VMEM((tc, w), F32), pltpu.VMEM((tc, w), F32)],
        compiler_params=pltpu.CompilerParams(dimension_semantics=("arbitrary", "arbitrary"),
                                             vmem_limit_bytes=VMEM_LIMIT),
        name="hgrn2",
    )(hq, hk, hg, hi, hgate, norm_g.reshape(1, HGRN_DIM), tri)


def _memkv_kernel(mem_ref, g_ref, w_ref, k_ref, v_ref):
    d = mem_ref.shape[-1]
    kv = _dot(_rms(mem_ref[0], g_ref[...]).astype(BF16), w_ref[...])
    k_ref[0] = kv[:, :d].astype(BF16)
    v_ref[0] = kv[:, d:].astype(BF16)


def _mem_kv(mem, norm_g, w_kv):
    b, m, d = mem.shape
    out = jax.ShapeDtypeStruct((b, m, d), BF16)
    spec = pl.BlockSpec((1, m, d), lambda bi: (bi, 0, 0))
    return pl.pallas_call(
        _memkv_kernel,
        grid=(b,),
        in_specs=[spec, pl.BlockSpec((1, d), lambda bi: (0, 0)), pl.BlockSpec((d, 2 * d), lambda bi: (0, 0))],
        out_specs=[spec, spec],
        out_shape=[out, out],
        compiler_params=pltpu.CompilerParams(dimension_semantics=("arbitrary",), vmem_limit_bytes=VMEM_LIMIT),
        name="mem_kv",
    )(mem, norm_g.reshape(1, d), w_kv.astype(BF16))


def _mix_kernel(x_ref, foxt_ref, rec_ref, wo_ref, gx_ref, wq_ref, mk_ref, mv_ref, wxo_ref, o_ref, att_sc):
    d = x_ref.shape[-1]
    fw = foxt_ref.shape[1]
    h1 = x_ref[0] + _dot_tn(foxt_ref[0], wo_ref[0:fw, :]) + _dot(rec_ref[0], wo_ref[fw:, :])
    q = _dot(_rms(h1, gx_ref[...]).astype(BF16), wq_ref[...])
    hd = d // X_HEADS
    inv = 1.0 / math.sqrt(hd)
    for h in range(X_HEADS):
        cs = slice(h * hd, (h + 1) * hd)
        s = _dot_nt(q[:, cs].astype(BF16), mk_ref[0, :, cs]) * inv
        p = jnp.exp(s - jnp.max(s, axis=-1, keepdims=True))
        p = p / jnp.sum(p, axis=-1, keepdims=True)
        att_sc[:, cs] = _dot(p.astype(BF16), mv_ref[0, :, cs]).astype(BF16)
    o_ref[0] = h1 + _dot(att_sc[...], wxo_ref[...])


def _mix(x, foxt, rec, w_out, norm_g, w_xq, mem_k, mem_v, w_xo, tm):
    b, s, d = x.shape
    m = mem_k.shape[1]
    fw = foxt.shape[1]
    tok = lambda wd: pl.BlockSpec((1, tm, wd), lambda bi, ti: (bi, ti, 0))
    const = lambda shape: pl.BlockSpec(shape, lambda bi, ti: (0,) * len(shape))
    mem_spec = pl.BlockSpec((1, m, d), lambda bi, ti: (bi, 0, 0))
    return pl.pallas_call(
        _mix_kernel,
        grid=(b, s // tm),
        in_specs=[tok(d), pl.BlockSpec((1, fw, tm), lambda bi, ti: (bi, 0, ti)), tok(rec.shape[-1]),
                  const((d, d)), const((1, d)), const((d, d)), mem_spec, mem_spec, const((d, d))],
        out_specs=tok(d),
        out_shape=jax.ShapeDtypeStruct((b, s, d), F32),
        scratch_shapes=[pltpu.VMEM((tm, d), BF16)],
        compiler_params=pltpu.CompilerParams(dimension_semantics=("arbitrary", "arbitrary"),
                                             vmem_limit_bytes=VMEM_LIMIT),
        name="mix",
    )(x, foxt, rec, w_out.astype(BF16), norm_g.reshape(1, d), w_xq.astype(BF16), mem_k, mem_v, w_xo.astype(BF16))


def _mlp_kernel(h_ref, g_ref, w1_ref, w2_ref, gf_ref, o_ref, hn_sc, acc_sc):
    f = pl.program_id(1)

    @pl.when(f == 0)
    def _():
        hn_sc[...] = _rms(h_ref[...], g_ref[...]).astype(BF16)
        acc_sc[...] = jnp.zeros_like(acc_sc)

    u = jnp.maximum(_dot(hn_sc[...], w1_ref[...]), 0.0)
    acc_sc[...] += _dot((u * u).astype(BF16), w2_ref[...])

    @pl.when(f == pl.num_programs(1) - 1)
    def _():
        o_ref[...] = _rms(h_ref[...] + acc_sc[...], gf_ref[...])


def _mlp(h, norm_g, w1, w2, final_g, tm, tf):
    t, d = h.shape
    dff = w1.shape[1]
    return pl.pallas_call(
        _mlp_kernel,
        grid=(t // tm, dff // tf),
        in_specs=[pl.BlockSpec((tm, d), lambda ti, fi: (ti, 0)), pl.BlockSpec((1, d), lambda ti, fi: (0, 0)),
                  pl.BlockSpec((d, tf), lambda ti, fi: (0, fi)), pl.BlockSpec((tf, d), lambda ti, fi: (fi, 0)),
                  pl.BlockSpec((1, d), lambda ti, fi: (0, 0))],
        out_specs=pl.BlockSpec((tm, d), lambda ti, fi: (ti, 0)),
        out_shape=jax.ShapeDtypeStruct((t, d), F32),
        scratch_shapes=[pltpu.VMEM((tm, d), BF16), pltpu.VMEM((tm, d), F32)],
        compiler_params=pltpu.CompilerParams(dimension_semantics=("arbitrary", "arbitrary"),
                                             vmem_limit_bytes=VMEM_LIMIT),
        name="mlp",
    )(h, norm_g.reshape(1, d), w1.astype(BF16), w2.astype(BF16), final_g.reshape(1, d))


def _tile(n, want):
    t = min(n, want)
    assert n % t == 0, (n, want)
    return t


def kernel(x, mem, norm_mix_g, w_in, fox_f_bias, hgrn_lb_logits, hgrn_norm_g, w_out, norm_x_g, norm_mem_g,
           w_xq, w_xkv, w_xo, norm_ff_g, w1, w2, final_norm_g):
    b, s, d = x.shape
    h = x
    for l in range(w_in.shape[0]):
        q, k, v, hq, hk, hg, hi, hgate = _in_proj(h, norm_mix_g[l], w_in[l], fox_f_bias[l], hgrn_lb_logits, l,
                                                  _tile(s, 256))
        fox = _fox_attention(q, k, v, _tile(s, 512), 8)
        rec = _hgrn(hq, hk, hg, hi, hgate, hgrn_norm_g[l], _tile(s, 256))
        mem_k, mem_v = _mem_kv(mem, norm_mem_g[l], w_xkv[l])
        h = _mix(h, fox, rec, w_out[l], norm_x_g[l], w_xq[l], mem_k, mem_v, w_xo[l], _tile(s, 256))
        is_last = l == w_in.shape[0] - 1
        assert is_last, "the MLP kernel fuses the final norm, so it must be the last layer"
        h = _mlp(h.reshape(b * s, d), norm_ff_g[l], w1[l], w2[l], final_norm_g,
                 _tile(b * s, 512), _tile(w1.shape[-1], 1024)).reshape(b, s, d)
    return h
```

```python
import functools
import math

import jax
import jax.numpy as jnp
import numpy as np
from jax import lax
from jax.experimental import pallas as pl
from jax.experimental.pallas import tpu as pltpu

EPS = 1e-6
LOG2E = math.log2(math.e)
LANES = 128
FOX_HEADS = 8
FOX_HEAD_DIM = 64
FOX_WIDTH = FOX_HEADS * FOX_HEAD_DIM
HGRN_HEADS = 4
HGRN_DIM = 128
HGRN_WIDTH = HGRN_HEADS * HGRN_DIM
X_HEADS = 4
HGRN_STEP = 16
HGRN_CHUNK = 64
HGRN_MIN_CHUNK_LOG_DECAY = -60.0
VMEM_LIMIT = 56 * 1024 * 1024

AUG_F = FOX_HEAD_DIM
AUG_ONE = FOX_HEAD_DIM + 3
FCAT_ONE_LANE = 24

BF16 = jnp.bfloat16
F32 = jnp.float32


def _dot(a, b):
    return jnp.dot(a, b, preferred_element_type=F32)


def _dot_nt(a, b):
    return lax.dot_general(a, b, (((1,), (1,)), ((), ())), preferred_element_type=F32)


def _dot_tn(a, b):
    return lax.dot_general(a, b, (((0,), (0,)), ((), ())), preferred_element_type=F32)


def _split3(v):
    hi = v.astype(BF16)
    r1 = v - hi.astype(F32)
    mid = r1.astype(BF16)
    lo = (r1 - mid.astype(F32)).astype(BF16)
    return hi, mid, lo


def _tri_cumsum(tri, v):
    hi, mid, lo = _split3(v)
    return _dot(tri, hi) + _dot(tri, mid) + _dot(tri, lo)


def _rms(x, g):
    ms = jnp.mean(x * x, axis=-1, keepdims=True)
    return x * lax.rsqrt(ms + EPS) * g


def _resident(shape):
    return pl.BlockSpec(shape, lambda *_: (0,) * len(shape), pipeline_mode=pl.Buffered(1))


def _inproj_kernel(x_ref, g_ref, wqk_ref, wvt_ref, wff_ref, wh_ref, fb_ref, lbl_ref, sel_ref, tri_ref,
                   q_ref, k_ref, vt_ref, hq_ref, hk_ref, hg_ref, hi_ref, hgate_ref, carry_ref, *, layer):
    @pl.when(pl.program_id(1) == 0)
    def _():
        carry_ref[...] = jnp.zeros_like(carry_ref)

    hb = _rms(x_ref[0], g_ref[...]).astype(BF16)
    tm = hb.shape[0]
    lane = lax.broadcasted_iota(jnp.int32, (tm, LANES), 1)

    z = _dot(hb, wff_ref[...]) + fb_ref[...]
    logf = jnp.minimum(z, 0.0) - jnp.log(1.0 + jnp.exp(-jnp.abs(z)))
    logf = jnp.where(lane < FOX_HEADS, logf, 0.0)
    fcum = _tri_cumsum(tri_ref[...], logf) + carry_ref[0:1, :]
    carry_ref[...] = jnp.broadcast_to(fcum[tm - 1:tm, :], carry_ref.shape)

    fsc = fcum * LOG2E
    f_hi = fsc.astype(BF16).astype(F32)
    r1 = fsc - f_hi
    f_mid = r1.astype(BF16).astype(F32)
    f_lo = (r1 - f_mid).astype(BF16).astype(F32)
    fcat = (f_hi + pltpu.roll(f_mid, FOX_HEADS, 1) + pltpu.roll(f_lo, 2 * FOX_HEADS, 1)
            + jnp.where(lane == FCAT_ONE_LANE, 1.0, 0.0))
    qk = _dot(hb, wqk_ref[...]) + _dot(fcat.astype(BF16), sel_ref[...])
    for h in range(FOX_HEADS):
        q_ref[0, h] = qk[:, h * LANES:(h + 1) * LANES].astype(BF16)
        k_ref[0, h] = qk[:, (FOX_HEADS + h) * LANES:(FOX_HEADS + h + 1) * LANES].astype(BF16)

    vt_ref[0] = _dot_nt(wvt_ref[...], hb).astype(BF16)

    lbl = lbl_ref[...]
    e = jnp.exp(lbl - jnp.max(lbl, axis=0, keepdims=True))
    lb = jnp.sum(e[0:layer + 1, :], axis=0, keepdims=True) / jnp.sum(e, axis=0, keepdims=True)
    w = HGRN_WIDTH
    seg = lambda n: _dot(hb, wh_ref[:, n * w:(n + 1) * w])
    gq = seg(0)
    hq_ref[0] = gq * jax.nn.sigmoid(gq)
    f = lb + (1.0 - lb) * jax.nn.sigmoid(seg(1))
    hk_ref[0] = 1.0 - f
    hg_ref[0] = jnp.log(f)
    hi_ref[0] = seg(2).astype(BF16)
    gg = seg(3)
    hgate_ref[0] = (gg * jax.nn.sigmoid(gg)).astype(BF16)


def _in_proj(x, norm_g, w_in, fox_f_bias, lb_logits, layer, tm):
    b, s, d = x.shape
    fw, hw = FOX_WIDTH, HGRN_WIDTH
    scale = LOG2E / math.sqrt(FOX_HEAD_DIM)
    wq, wk, wv = w_in[:, 0:fw] * scale, w_in[:, fw:2 * fw], w_in[:, 2 * fw:3 * fw]
    wff = w_in[:, 3 * fw:3 * fw + FOX_HEADS]
    wh = w_in[:, 3 * fw + FOX_HEADS:]

    def pad_heads(wx):
        wx = wx.reshape(d, FOX_HEADS, FOX_HEAD_DIM)
        return jnp.pad(wx, ((0, 0), (0, 0), (0, LANES - FOX_HEAD_DIM))).reshape(d, FOX_HEADS * LANES)

    wqk = jnp.concatenate([pad_heads(wq), pad_heads(wk)], axis=1).astype(BF16)
    wff = jnp.pad(wff, ((0, 0), (0, LANES - FOX_HEADS))).astype(BF16)
    fb = jnp.pad(fox_f_bias.reshape(1, FOX_HEADS), ((0, 0), (0, LANES - FOX_HEADS)))

    sel = np.zeros((LANES, 2 * FOX_HEADS * LANES), np.float32)
    for h in range(FOX_HEADS):
        for j in range(3):
            sel[j * FOX_HEADS + h, h * LANES + AUG_F + j] = 1.0
            sel[FCAT_ONE_LANE, h * LANES + AUG_ONE + j] = 1.0
            sel[FCAT_ONE_LANE, (FOX_HEADS + h) * LANES + AUG_F + j] = 1.0
            sel[j * FOX_HEADS + h, (FOX_HEADS + h) * LANES + AUG_ONE + j] = -1.0
    sel = jnp.asarray(sel, BF16)
    tri = jnp.asarray(np.tril(np.ones((tm, tm), np.float32)), BF16)

    const = _resident
    head_out = jax.ShapeDtypeStruct((b, FOX_HEADS, s, LANES), BF16)
    head_spec = pl.BlockSpec((1, FOX_HEADS, tm, LANES), lambda bi, ti: (bi, 0, ti, 0))
    tok_out = lambda dt: jax.ShapeDtypeStruct((b, s, hw), dt)
    tok_spec = pl.BlockSpec((1, tm, hw), lambda bi, ti: (bi, ti, 0))
    return pl.pallas_call(
        functools.partial(_inproj_kernel, layer=layer),
        grid=(b, s // tm),
        in_specs=[pl.BlockSpec((1, tm, d), lambda bi, ti: (bi, ti, 0)),
                  const((1, d)), const(wqk.shape), const((fw, d)), const(wff.shape), const((d, 4 * hw)),
                  const(fb.shape), const(lb_logits.shape), const(sel.shape), const(tri.shape)],
        out_specs=[head_spec, head_spec, pl.BlockSpec((1, fw, tm), lambda bi, ti: (bi, 0, ti)),
                   tok_spec, tok_spec, tok_spec, tok_spec, tok_spec],
        out_shape=[head_out, head_out, jax.ShapeDtypeStruct((b, fw, s), BF16),
                   tok_out(F32), tok_out(F32), tok_out(F32), tok_out(BF16), tok_out(BF16)],
        scratch_shapes=[pltpu.VMEM((8, LANES), F32)],
        compiler_params=pltpu.CompilerParams(dimension_semantics=("arbitrary", "arbitrary"),
                                             vmem_limit_bytes=VMEM_LIMIT),
        name="in_proj",
    )(x, norm_g.reshape(1, d), wqk, wv.T.astype(BF16), wff, wh.astype(BF16), fb, lb_logits, sel, tri)


def _fox_kernel(q_ref, k_ref, vt_ref, o_ref, m_sc, l_sc, acc_sc, *, tq, hg):
    i = pl.program_id(2)
    hd = FOX_HEAD_DIM
    m_sc[...] = jnp.full_like(m_sc, -jnp.inf)
    l_sc[...] = jnp.zeros_like(l_sc)
    acc_sc[...] = jnp.zeros_like(acc_sc)

    def block(j, masked):
        off = pl.multiple_of(j * tq, tq)
        scores = lambda h_: _dot_nt(k_ref[0, h_, pl.ds(off, tq), :], q_ref[0, h_])
        st_next = scores(0)
        for hh in range(hg):
            st = st_next
            if hh + 1 < hg:
                st_next = scores(hh + 1)
            if masked:
                key = lax.broadcasted_iota(jnp.int32, st.shape, 0)
                qry = lax.broadcasted_iota(jnp.int32, st.shape, 1)
                st = jnp.where(key <= qry, st, -jnp.inf)
            m_prev = m_sc[hh]
            m_new = jnp.maximum(m_prev, jnp.max(st, axis=0, keepdims=True))
            alpha = jnp.exp2(m_prev - m_new)
            p = jnp.exp2(st - m_new)
            l_sc[hh] = alpha * l_sc[hh] + jnp.sum(p, axis=0, keepdims=True)
            vt = vt_ref[0, hh * hd:(hh + 1) * hd, pl.ds(off, tq)]
            acc_sc[hh] = alpha * acc_sc[hh] + _dot(vt, p.astype(BF16))
            m_sc[hh] = m_new

    def body(j, carry):
        block(j, False)
        return carry

    lax.fori_loop(0, i, body, 0)
    block(i, True)
    for hh in range(hg):
        o_ref[0, hh * hd:(hh + 1) * hd, :] = (acc_sc[hh] / l_sc[hh]).astype(o_ref.dtype)


def _fox_attention(q, k, vt, tq, hg):
    b, hds, s, _ = q.shape
    groups = hds // hg
    rows = hg * FOX_HEAD_DIM
    return pl.pallas_call(
        functools.partial(_fox_kernel, tq=tq, hg=hg),
        grid=(b, groups, s // tq),
        in_specs=[pl.BlockSpec((1, hg, tq, LANES), lambda bi, pi, qi: (bi, pi, qi, 0)),
                  pl.BlockSpec((1, hg, s, LANES), lambda bi, pi, qi: (bi, pi, 0, 0)),
                  pl.BlockSpec((1, rows, s), lambda bi, pi, qi: (bi, pi, 0))],
        out_specs=pl.BlockSpec((1, rows, tq), lambda bi, pi, qi: (bi, pi, qi)),
        out_shape=jax.ShapeDtypeStruct((b, groups * rows, s), BF16),
        scratch_shapes=[pltpu.VMEM((hg, 1, tq), F32), pltpu.VMEM((hg, 1, tq), F32),
                        pltpu.VMEM((hg, FOX_HEAD_DIM, tq), F32)],
        compiler_params=pltpu.CompilerParams(dimension_semantics=("arbitrary",) * 3,
                                             vmem_limit_bytes=VMEM_LIMIT),
        name="fox_attention",
    )(q, k, vt)


def _hgrn_chunk_path(q_ref, k_ref, i_ref, st_sc, b_sc, o_sc, tc):
    c_len = HGRN_CHUNK
    t_idx = lax.broadcasted_iota(jnp.int32, (c_len, c_len), 0)
    s_idx = lax.broadcasted_iota(jnp.int32, (c_len, c_len), 1)
    causal = s_idx <= t_idx
    for c in range(tc // c_len):
        rows = slice(c * c_len, (c + 1) * c_len)
        for h in range(HGRN_HEADS):
            ls = slice(h * HGRN_DIM, (h + 1) * HGRN_DIM)
            b = b_sc[rows, ls]
            qq = q_ref[0, rows, ls]
            kk = k_ref[0, rows, ls]
            ii = i_ref[0, rows, ls]
            b_last = b[c_len - 1:c_len, :]
            qe = (qq * jnp.exp(b)).astype(BF16)
            ke = (kk * jnp.exp(-b)).astype(BF16)
            kl = (kk * jnp.exp(b_last - b)).astype(BF16)
            attn = jnp.where(causal, _dot_nt(qe, ke), 0.0).astype(BF16)
            st = st_sc[h]
            o_sc[rows, ls] = _dot(attn, ii) + _dot_nt(qe, st.astype(BF16))
            st_sc[h] = st * jnp.exp(b_last) + _dot_tn(ii, kl)


def _hgrn_step_path(q_ref, k_ref, i_ref, st_sc, b_sc, o_sc, tc):
    row8 = lax.broadcasted_iota(jnp.int32, (8, HGRN_DIM), 0)
    half = HGRN_STEP // 2

    def step(u, carry):
        base = pl.multiple_of(u * HGRN_STEP, HGRN_STEP)
        for h in range(HGRN_HEADS):
            ls = slice(h * HGRN_DIM, (h + 1) * HGRN_DIM)
            rows = pl.ds(base, HGRN_STEP)
            bq = b_sc[rows, ls]
            qq = q_ref[0, rows, ls]
            kk = k_ref[0, rows, ls]
            ii = i_ref[0, rows, ls].astype(F32)
            b_last = bq[HGRN_STEP - 1:HGRN_STEP, :]
            st = st_sc[h]
            inter = _dot_nt((qq * jnp.exp(bq)).astype(BF16), st.astype(BF16))
            k_dec = (kk * jnp.exp(b_last - bq)).astype(BF16)
            st_sc[h] = st * jnp.exp(b_last) + _dot_tn(ii.astype(BF16), k_dec)
            out = [inter[0:half], inter[half:HGRN_STEP]]
            qv = [qq[0:half], qq[half:HGRN_STEP]]
            bv = [bq[0:half], bq[half:HGRN_STEP]]
            for s_ in range(HGRN_STEP):
                k_s, b_s, i_s = kk[s_:s_ + 1, :], bq[s_:s_ + 1, :], ii[s_:s_ + 1, :]
                for v_ in range(s_ // half, 2):
                    diff = bv[v_] - b_s
                    if s_ > v_ * half:
                        diff = jnp.where(row8 + v_ * half >= s_, diff, -jnp.inf)
                    a = jnp.sum(qv[v_] * k_s * jnp.exp(diff), axis=-1, keepdims=True)
                    out[v_] = out[v_] + a * i_s
            o_sc[pl.ds(base, half), ls] = out[0]
            o_sc[pl.ds(base + half, half), ls] = out[1]
        return carry

    lax.fori_loop(0, tc // HGRN_STEP, step, 0)


def _hgrn_kernel(q_ref, k_ref, g_ref, i_ref, gate_ref, ng_ref, tri64_ref, tri16_ref, o_ref,
                 st_sc, b_sc, o_sc, *, tc):
    @pl.when(pl.program_id(1) == 0)
    def _():
        st_sc[...] = jnp.zeros_like(st_sc)

    b64 = _tri_cumsum(tri64_ref[...], g_ref[0])
    b_sc[...] = b64
    chunk_ok = jnp.min(b64) >= HGRN_MIN_CHUNK_LOG_DECAY

    @pl.when(chunk_ok)
    def _():
        _hgrn_chunk_path(q_ref, k_ref, i_ref, st_sc, b_sc, o_sc, tc)

    @pl.when(jnp.logical_not(chunk_ok))
    def _():
        b_sc[...] = _tri_cumsum(tri16_ref[...], g_ref[0])
        _hgrn_step_path(q_ref, k_ref, i_ref, st_sc, b_sc, o_sc, tc)

    for h in range(HGRN_HEADS):
        ls = slice(h * HGRN_DIM, (h + 1) * HGRN_DIM)
        o_ref[0, :, ls] = (_rms(o_sc[:, ls], ng_ref[...]) * gate_ref[0, :, ls]).astype(o_ref.dtype)


def _block_tri(n, blk):
    return jnp.asarray(np.kron(np.eye(n // blk, dtype=np.float32), np.tril(np.ones((blk, blk), np.float32))), BF16)


def _hgrn(hq, hk, hg, hi, hgate, norm_g, tc):
    b, s, w = hq.shape
    tok_spec = pl.BlockSpec((1, tc, w), lambda bi, ci: (bi, ci, 0))
    return pl.pallas_call(
        functools.partial(_hgrn_kernel, tc=tc),
        grid=(b, s // tc),
        in_specs=[tok_spec] * 5 + [_resident((1, HGRN_DIM)), _resident((tc, tc)), _resident((tc, tc))],
        out_specs=tok_spec,
        out_shape=jax.ShapeDtypeStruct((b, s, w), BF16),
        scratch_shapes=[pltpu.VMEM((HGRN_HEADS, HGRN_DIM, HGRN_DIM), F32),
                        pltpu.VMEM((tc, w), F32), pltpu.VMEM((tc, w), F32)],
        compiler_params=pltpu.CompilerParams(dimension_semantics=("arbitrary", "arbitrary"),
                                             vmem_limit_bytes=VMEM_LIMIT),
        name="hgrn2",
    )(hq, hk, hg, hi, hgate, norm_g.reshape(1, HGRN_DIM), _block_tri(tc, HGRN_CHUNK), _block_tri(tc, HGRN_STEP))


def _memkv_kernel(mem_ref, g_ref, w_ref, k_ref, v_ref):
    d = mem_ref.shape[-1]
    kv = _dot(_rms(mem_ref[0], g_ref[...]).astype(BF16), w_ref[...])
    k_ref[0] = kv[:, :d].astype(BF16)
    v_ref[0] = kv[:, d:].astype(BF16)


def _mem_kv(mem, norm_g, w_kv):
    b, m, d = mem.shape
    out = jax.ShapeDtypeStruct((b, m, d), BF16)
    spec = pl.BlockSpec((1, m, d), lambda bi: (bi, 0, 0))
    return pl.pallas_call(
        _memkv_kernel,
        grid=(b,),
        in_specs=[spec, pl.BlockSpec((1, d), lambda bi: (0, 0)), pl.BlockSpec((d, 2 * d), lambda bi: (0, 0))],
        out_specs=[spec, spec],
        out_shape=[out, out],
        compiler_params=pltpu.CompilerParams(dimension_semantics=("arbitrary",), vmem_limit_bytes=VMEM_LIMIT),
        name="mem_kv",
    )(mem, norm_g.reshape(1, d), w_kv.astype(BF16))


def _mix_kernel(x_ref, foxt_ref, rec_ref, wo_ref, gx_ref, wq_ref, mk_ref, mv_ref, wxo_ref, o_ref, att_sc):
    d = x_ref.shape[-1]
    fw = foxt_ref.shape[1]
    h1 = x_ref[0] + _dot_tn(foxt_ref[0], wo_ref[0:fw, :]) + _dot(rec_ref[0], wo_ref[fw:, :])
    q = _dot(_rms(h1, gx_ref[...]).astype(BF16), wq_ref[...])
    hd = d // X_HEADS
    inv = 1.0 / math.sqrt(hd)
    for h in range(X_HEADS):
        cs = slice(h * hd, (h + 1) * hd)
        s = _dot_nt(q[:, cs].astype(BF16), mk_ref[0, :, cs]) * inv
        p = jnp.exp(s - jnp.max(s, axis=-1, keepdims=True))
        p = p / jnp.sum(p, axis=-1, keepdims=True)
        att_sc[:, cs] = _dot(p.astype(BF16), mv_ref[0, :, cs]).astype(BF16)
    o_ref[0] = h1 + _dot(att_sc[...], wxo_ref[...])


def _mix(x, foxt, rec, w_out, norm_g, w_xq, mem_k, mem_v, w_xo, tm):
    b, s, d = x.shape
    m = mem_k.shape[1]
    fw = foxt.shape[1]
    tok = lambda wd: pl.BlockSpec((1, tm, wd), lambda bi, ti: (bi, ti, 0))
    const = lambda shape: pl.BlockSpec(shape, lambda bi, ti: (0,) * len(shape))
    mem_spec = pl.BlockSpec((1, m, d), lambda bi, ti: (bi, 0, 0))
    return pl.pallas_call(
        _mix_kernel,
        grid=(b, s // tm),
        in_specs=[tok(d), pl.BlockSpec((1, fw, tm), lambda bi, ti: (bi, 0, ti)), tok(rec.shape[-1]),
                  const((d, d)), const((1, d)), const((d, d)), mem_spec, mem_spec, const((d, d))],
        out_specs=tok(d),
        out_shape=jax.ShapeDtypeStruct((b, s, d), F32),
        scratch_shapes=[pltpu.VMEM((tm, d), BF16)],
        compiler_params=pltpu.CompilerParams(dimension_semantics=("arbitrary", "arbitrary"),
                                             vmem_limit_bytes=VMEM_LIMIT),
        name="mix",
    )(x, foxt, rec, w_out.astype(BF16), norm_g.reshape(1, d), w_xq.astype(BF16), mem_k, mem_v, w_xo.astype(BF16))


def _mlp_kernel(h_ref, g_ref, w1_ref, w2_ref, gf_ref, o_ref, acc_sc, *, fc):
    hn = _rms(h_ref[...], g_ref[...]).astype(BF16)
    n_chunks = w1_ref.shape[1] // fc
    up = lambda c: _dot(hn, w1_ref[:, c * fc:(c + 1) * fc])
    u_next = up(0)
    for c in range(n_chunks):
        u = jnp.maximum(u_next, 0.0)
        if c + 1 < n_chunks:
            u_next = up(c + 1)
        part = _dot((u * u).astype(BF16), w2_ref[c * fc:(c + 1) * fc, :])
        if c == 0:
            acc_sc[...] = part
        else:
            acc_sc[...] += part
    o_ref[...] = _rms(h_ref[...] + acc_sc[...], gf_ref[...])


def _mlp(h, norm_g, w1, w2, final_g, tm, fc):
    t, d = h.shape
    dff = w1.shape[1]
    return pl.pallas_call(
        functools.partial(_mlp_kernel, fc=fc),
        grid=(t // tm,),
        in_specs=[pl.BlockSpec((tm, d), lambda ti: (ti, 0)), _resident((1, d)),
                  _resident((d, dff)), _resident((dff, d)), _resident((1, d))],
        out_specs=pl.BlockSpec((tm, d), lambda ti: (ti, 0)),
        out_shape=jax.ShapeDtypeStruct((t, d), F32),
        scratch_shapes=[pltpu.VMEM((tm, d), F32)],
        compiler_params=pltpu.CompilerParams(dimension_semantics=("arbitrary",), vmem_limit_bytes=VMEM_LIMIT),
        name="mlp",
    )(h, norm_g.reshape(1, d), w1.astype(BF16), w2.astype(BF16), final_g.reshape(1, d))


def _tile(n, want):
    t = min(n, want)
    assert n % t == 0, (n, want)
    return t


def kernel(x, mem, norm_mix_g, w_in, fox_f_bias, hgrn_lb_logits, hgrn_norm_g, w_out, norm_x_g, norm_mem_g,
           w_xq, w_xkv, w_xo, norm_ff_g, w1, w2, final_norm_g):
    b, s, d = x.shape
    h = x
    for l in range(w_in.shape[0]):
        q, k, v, hq, hk, hg, hi, hgate = _in_proj(h, norm_mix_g[l], w_in[l], fox_f_bias[l], hgrn_lb_logits, l,
                                                  _tile(s, 512))
        fox = _fox_attention(q, k, v, _tile(s, 512), 8)
        rec = _hgrn(hq, hk, hg, hi, hgate, hgrn_norm_g[l], _tile(s, 256))
        mem_k, mem_v = _mem_kv(mem, norm_mem_g[l], w_xkv[l])
        h = _mix(h, fox, rec, w_out[l], norm_x_g[l], w_xq[l], mem_k, mem_v, w_xo[l], _tile(s, 256))
        is_last = l == w_in.shape[0] - 1
        assert is_last, "the MLP kernel fuses the final norm, so it must be the last layer"
        h = _mlp(h.reshape(b * s, d), norm_ff_g[l], w1[l], w2[l], final_norm_g,
                 _tile(b * s, 512), _tile(w1.shape[-1], 1024)).reshape(b, s, d)
    return h
```

```python
import functools
import math

import jax
import jax.numpy as jnp
import numpy as np
from jax import lax
from jax.experimental import pallas as pl
from jax.experimental.pallas import tpu as pltpu

EPS = 1e-6
LOG2E = math.log2(math.e)
LANES = 128
FOX_HEADS = 8
FOX_HEAD_DIM = 64
FOX_WIDTH = FOX_HEADS * FOX_HEAD_DIM
HGRN_HEADS = 4
HGRN_DIM = 128
HGRN_WIDTH = HGRN_HEADS * HGRN_DIM
X_HEADS = 4
HGRN_STEP = 16
HGRN_CHUNK = 64
HGRN_MIN_CHUNK_LOG_DECAY = -60.0
VMEM_LIMIT = 56 * 1024 * 1024

PAIR_LANES = 2 * LANES
AUG_F = 0
AUG_ONE = 3
FCAT_ONE_LANE = 24

BF16 = jnp.bfloat16
F32 = jnp.float32


def _dot(a, b):
    return jnp.dot(a, b, preferred_element_type=F32)


def _dot_nt(a, b):
    return lax.dot_general(a, b, (((1,), (1,)), ((), ())), preferred_element_type=F32)


def _dot_tn(a, b):
    return lax.dot_general(a, b, (((0,), (0,)), ((), ())), preferred_element_type=F32)


def _split3(v):
    hi = v.astype(BF16)
    r1 = v - hi.astype(F32)
    mid = r1.astype(BF16)
    lo = (r1 - mid.astype(F32)).astype(BF16)
    return hi, mid, lo


def _tri_cumsum(tri, v):
    hi, mid, lo = _split3(v)
    return _dot(tri, hi) + _dot(tri, mid) + _dot(tri, lo)


def _rms(x, g):
    ms = jnp.mean(x * x, axis=-1, keepdims=True)
    return x * lax.rsqrt(ms + EPS) * g


def _resident(shape):
    return pl.BlockSpec(shape, lambda *_: (0,) * len(shape), pipeline_mode=pl.Buffered(1))


def _inproj_kernel(x_ref, g_ref, wqk_ref, wvt_ref, wff_ref, wh_ref, fb_ref, lbl_ref, sel_ref, tri_ref,
                   q_ref, k_ref, vt_ref, hq_ref, hk_ref, hg_ref, hi_ref, hgate_ref, carry_ref, *, layer):
    @pl.when(pl.program_id(1) == 0)
    def _():
        carry_ref[...] = jnp.zeros_like(carry_ref)

    hb = _rms(x_ref[0], g_ref[...]).astype(BF16)
    tm = hb.shape[0]
    lane = lax.broadcasted_iota(jnp.int32, (tm, LANES), 1)

    z = _dot(hb, wff_ref[...]) + fb_ref[...]
    logf = jnp.minimum(z, 0.0) - jnp.log(1.0 + jnp.exp(-jnp.abs(z)))
    logf = jnp.where(lane < FOX_HEADS, logf, 0.0)
    fcum = _tri_cumsum(tri_ref[...], logf) + carry_ref[0:1, :]
    carry_ref[...] = jnp.broadcast_to(fcum[tm - 1:tm, :], carry_ref.shape)

    fsc = fcum * LOG2E
    f_hi = fsc.astype(BF16).astype(F32)
    r1 = fsc - f_hi
    f_mid = r1.astype(BF16).astype(F32)
    f_lo = (r1 - f_mid).astype(BF16).astype(F32)
    fcat = (f_hi + pltpu.roll(f_mid, FOX_HEADS, 1) + pltpu.roll(f_lo, 2 * FOX_HEADS, 1)
            + jnp.where(lane == FCAT_ONE_LANE, 1.0, 0.0))
    qk = _dot(hb, wqk_ref[...]).astype(BF16)
    aug = _dot(fcat.astype(BF16), sel_ref[...]).astype(BF16)
    pairs = FOX_HEADS // 2
    for p in range(pairs):
        q_ref[0, p, :, 0:LANES] = qk[:, p * LANES:(p + 1) * LANES]
        q_ref[0, p, :, LANES:PAIR_LANES] = aug[:, p * LANES:(p + 1) * LANES]
        k_ref[0, p, :, 0:LANES] = qk[:, (pairs + p) * LANES:(pairs + p + 1) * LANES]
        k_ref[0, p, :, LANES:PAIR_LANES] = aug[:, (pairs + p) * LANES:(pairs + p + 1) * LANES]

    vt_ref[0] = _dot_nt(wvt_ref[...], hb).astype(BF16)

    lbl = lbl_ref[...]
    e = jnp.exp(lbl - jnp.max(lbl, axis=0, keepdims=True))
    lb = jnp.sum(e[0:layer + 1, :], axis=0, keepdims=True) / jnp.sum(e, axis=0, keepdims=True)
    w = HGRN_WIDTH
    seg = lambda n: _dot(hb, wh_ref[:, n * w:(n + 1) * w])
    gq = seg(0)
    hq_ref[0] = gq * jax.nn.sigmoid(gq)
    f = lb + (1.0 - lb) * jax.nn.sigmoid(seg(1))
    hk_ref[0] = 1.0 - f
    hg_ref[0] = jnp.log(f)
    hi_ref[0] = seg(2).astype(BF16)
    gg = seg(3)
    hgate_ref[0] = (gg * jax.nn.sigmoid(gg)).astype(BF16)


def _in_proj(x, norm_g, w_in, fox_f_bias, lb_logits, layer, tm):
    b, s, d = x.shape
    fw, hw = FOX_WIDTH, HGRN_WIDTH
    scale = LOG2E / math.sqrt(FOX_HEAD_DIM)
    wq, wk, wv = w_in[:, 0:fw] * scale, w_in[:, fw:2 * fw], w_in[:, 2 * fw:3 * fw]
    wff = w_in[:, 3 * fw:3 * fw + FOX_HEADS]
    wh = w_in[:, 3 * fw + FOX_HEADS:]

    wqk = jnp.concatenate([wq, wk], axis=1).astype(BF16)
    wff = jnp.pad(wff, ((0, 0), (0, LANES - FOX_HEADS))).astype(BF16)
    fb = jnp.pad(fox_f_bias.reshape(1, FOX_HEADS), ((0, 0), (0, LANES - FOX_HEADS)))

    sel = np.zeros((LANES, 2 * fw), np.float32)
    for h in range(FOX_HEADS):
        col = h * FOX_HEAD_DIM
        for j in range(3):
            sel[j * FOX_HEADS + h, col + AUG_F + j] = 1.0
            sel[FCAT_ONE_LANE, col + AUG_ONE + j] = 1.0
            sel[FCAT_ONE_LANE, fw + col + AUG_F + j] = 1.0
            sel[j * FOX_HEADS + h, fw + col + AUG_ONE + j] = -1.0
    sel = jnp.asarray(sel, BF16)
    tri = jnp.asarray(np.tril(np.ones((tm, tm), np.float32)), BF16)

    const = _resident
    pairs = FOX_HEADS // 2
    head_out = jax.ShapeDtypeStruct((b, pairs, s, PAIR_LANES), BF16)
    head_spec = pl.BlockSpec((1, pairs, tm, PAIR_LANES), lambda bi, ti: (bi, 0, ti, 0))
    tok_out = lambda dt: jax.ShapeDtypeStruct((b, s, hw), dt)
    tok_spec = pl.BlockSpec((1, tm, hw), lambda bi, ti: (bi, ti, 0))
    return pl.pallas_call(
        functools.partial(_inproj_kernel, layer=layer),
        grid=(b, s // tm),
        in_specs=[pl.BlockSpec((1, tm, d), lambda bi, ti: (bi, ti, 0)),
                  const((1, d)), const(wqk.shape), const((fw, d)), const(wff.shape), const((d, 4 * hw)),
                  const(fb.shape), const(lb_logits.shape), const(sel.shape), const(tri.shape)],
        out_specs=[head_spec, head_spec, pl.BlockSpec((1, fw, tm), lambda bi, ti: (bi, 0, ti)),
                   tok_spec, tok_spec, tok_spec, tok_spec, tok_spec],
        out_shape=[head_out, head_out, jax.ShapeDtypeStruct((b, fw, s), BF16),
                   tok_out(F32), tok_out(F32), tok_out(F32), tok_out(BF16), tok_out(BF16)],
        scratch_shapes=[pltpu.VMEM((8, LANES), F32)],
        compiler_params=pltpu.CompilerParams(dimension_semantics=("arbitrary", "arbitrary"),
                                             vmem_limit_bytes=VMEM_LIMIT),
        name="in_proj",
    )(x, norm_g.reshape(1, d), wqk, wv.T.astype(BF16), wff, wh.astype(BF16), fb, lb_logits, sel, tri)


def _fox_kernel(q_ref, k_ref, vt_ref, o_ref, qm_sc, m_sc, l_sc, acc_sc, *, tq, hg):
    i = pl.program_id(2)
    hd = FOX_HEAD_DIM
    m_sc[...] = jnp.full_like(m_sc, -jnp.inf)
    l_sc[...] = jnp.zeros_like(l_sc)
    acc_sc[...] = jnp.zeros_like(acc_sc)
    lane = lax.broadcasted_iota(jnp.int32, (tq, PAIR_LANES), 1)
    for hh in range(hg):
        own = (lane // hd) % 2 == hh % 2
        qm_sc[hh] = jnp.where(own, q_ref[0, hh // 2], jnp.zeros((), BF16))

    def block(j, masked):
        off = pl.multiple_of(j * tq, tq)
        scores = lambda h_: _dot_nt(k_ref[0, h_ // 2, pl.ds(off, tq), :], qm_sc[h_])
        st_next = scores(0)
        for hh in range(hg):
            st = st_next
            if hh + 1 < hg:
                st_next = scores(hh + 1)
            if masked:
                key = lax.broadcasted_iota(jnp.int32, st.shape, 0)
                qry = lax.broadcasted_iota(jnp.int32, st.shape, 1)
                st = jnp.where(key <= qry, st, -jnp.inf)
            m_prev = m_sc[hh]
            m_new = jnp.maximum(m_prev, jnp.max(st, axis=0, keepdims=True))
            alpha = jnp.exp2(m_prev - m_new)
            p = jnp.exp2(st - m_new)
            l_sc[hh] = alpha * l_sc[hh] + jnp.sum(p, axis=0, keepdims=True)
            vt = vt_ref[0, hh * hd:(hh + 1) * hd, pl.ds(off, tq)]
            acc_sc[hh] = alpha * acc_sc[hh] + _dot(vt, p.astype(BF16))
            m_sc[hh] = m_new

    def body(j, carry):
        block(j, False)
        return carry

    lax.fori_loop(0, i, body, 0)
    block(i, True)
    for hh in range(hg):
        o_ref[0, hh * hd:(hh + 1) * hd, :] = (acc_sc[hh] / l_sc[hh]).astype(o_ref.dtype)


def _fox_attention(q, k, vt, tq, hg):
    b, pairs, s, _ = q.shape
    assert hg % 2 == 0
    groups = 2 * pairs // hg
    rows = hg * FOX_HEAD_DIM
    return pl.pallas_call(
        functools.partial(_fox_kernel, tq=tq, hg=hg),
        grid=(b, groups, s // tq),
        in_specs=[pl.BlockSpec((1, hg // 2, tq, PAIR_LANES), lambda bi, pi, qi: (bi, pi, qi, 0)),
                  pl.BlockSpec((1, hg // 2, s, PAIR_LANES), lambda bi, pi, qi: (bi, pi, 0, 0)),
                  pl.BlockSpec((1, rows, s), lambda bi, pi, qi: (bi, pi, 0))],
        out_specs=pl.BlockSpec((1, rows, tq), lambda bi, pi, qi: (bi, pi, qi)),
        out_shape=jax.ShapeDtypeStruct((b, groups * rows, s), BF16),
        scratch_shapes=[pltpu.VMEM((hg, tq, PAIR_LANES), BF16),
                        pltpu.VMEM((hg, 1, tq), F32), pltpu.VMEM((hg, 1, tq), F32),
                        pltpu.VMEM((hg, FOX_HEAD_DIM, tq), F32)],
        compiler_params=pltpu.CompilerParams(dimension_semantics=("arbitrary",) * 3,
                                             vmem_limit_bytes=VMEM_LIMIT),
        name="fox_attention",
    )(q, k, vt)


def _hgrn_chunk_path(q_ref, k_ref, i_ref, st_sc, b_sc, o_sc, tc):
    c_len = HGRN_CHUNK
    t_idx = lax.broadcasted_iota(jnp.int32, (c_len, c_len), 0)
    s_idx = lax.broadcasted_iota(jnp.int32, (c_len, c_len), 1)
    causal = s_idx <= t_idx
    for c in range(tc // c_len):
        rows = slice(c * c_len, (c + 1) * c_len)
        for h in range(HGRN_HEADS):
            ls = slice(h * HGRN_DIM, (h + 1) * HGRN_DIM)
            b = b_sc[rows, ls]
            qq = q_ref[0, rows, ls]
            kk = k_ref[0, rows, ls]
            ii = i_ref[0, rows, ls]
            b_last = b[c_len - 1:c_len, :]
            qe = (qq * jnp.exp(b)).astype(BF16)
            ke = (kk * jnp.exp(-b)).astype(BF16)
            kl = (kk * jnp.exp(b_last - b)).astype(BF16)
            attn = jnp.where(causal, _dot_nt(qe, ke), 0.0).astype(BF16)
            st = st_sc[h]
            o_sc[rows, ls] = _dot(attn, ii) + _dot_nt(qe, st.astype(BF16))
            st_sc[h] = st * jnp.exp(b_last) + _dot_tn(ii, kl)


def _hgrn_step_path(q_ref, k_ref, i_ref, st_sc, b_sc, o_sc, tc):
    row8 = lax.broadcasted_iota(jnp.int32, (8, HGRN_DIM), 0)
    half = HGRN_STEP // 2

    def step(u, carry):
        base = pl.multiple_of(u * HGRN_STEP, HGRN_STEP)
        for h in range(HGRN_HEADS):
            ls = slice(h * HGRN_DIM, (h + 1) * HGRN_DIM)
            rows = pl.ds(base, HGRN_STEP)
            bq = b_sc[rows, ls]
            qq = q_ref[0, rows, ls]
            kk = k_ref[0, rows, ls]
            ii = i_ref[0, rows, ls].astype(F32)
            b_last = bq[HGRN_STEP - 1:HGRN_STEP, :]
            st = st_sc[h]
            inter = _dot_nt((qq * jnp.exp(bq)).astype(BF16), st.astype(BF16))
            k_dec = (kk * jnp.exp(b_last - bq)).astype(BF16)
            st_sc[h] = st * jnp.exp(b_last) + _dot_tn(ii.astype(BF16), k_dec)
            out = [inter[0:half], inter[half:HGRN_STEP]]
            qv = [qq[0:half], qq[half:HGRN_STEP]]
            bv = [bq[0:half], bq[half:HGRN_STEP]]
            for s_ in range(HGRN_STEP):
                k_s, b_s, i_s = kk[s_:s_ + 1, :], bq[s_:s_ + 1, :], ii[s_:s_ + 1, :]
                for v_ in range(s_ // half, 2):
                    diff = bv[v_] - b_s
                    if s_ > v_ * half:
                        diff = jnp.where(row8 + v_ * half >= s_, diff, -jnp.inf)
                    a = jnp.sum(qv[v_] * k_s * jnp.exp(diff), axis=-1, keepdims=True)
                    out[v_] = out[v_] + a * i_s
            o_sc[pl.ds(base, half), ls] = out[0]
            o_sc[pl.ds(base + half, half), ls] = out[1]
        return carry

    lax.fori_loop(0, tc // HGRN_STEP, step, 0)


def _hgrn_kernel(q_ref, k_ref, g_ref, i_ref, gate_ref, ng_ref, tri64_ref, tri16_ref, o_ref,
                 st_sc, b_sc, o_sc, *, tc):
    @pl.when(pl.program_id(1) == 0)
    def _():
        st_sc[...] = jnp.zeros_like(st_sc)

    b64 = _tri_cumsum(tri64_ref[...], g_ref[0])
    b_sc[...] = b64
    chunk_ok = jnp.min(b64) >= HGRN_MIN_CHUNK_LOG_DECAY

    @pl.when(chunk_ok)
    def _():
        _hgrn_chunk_path(q_ref, k_ref, i_ref, st_sc, b_sc, o_sc, tc)

    @pl.when(jnp.logical_not(chunk_ok))
    def _():
        b_sc[...] = _tri_cumsum(tri16_ref[...], g_ref[0])
        _hgrn_step_path(q_ref, k_ref, i_ref, st_sc, b_sc, o_sc, tc)

    for h in range(HGRN_HEADS):
        ls = slice(h * HGRN_DIM, (h + 1) * HGRN_DIM)
        o_ref[0, :, ls] = (_rms(o_sc[:, ls], ng_ref[...]) * gate_ref[0, :, ls]).astype(o_ref.dtype)


def _block_tri(n, blk):
    return jnp.asarray(np.kron(np.eye(n // blk, dtype=np.float32), np.tril(np.ones((blk, blk), np.float32))), BF16)


def _hgrn(hq, hk, hg, hi, hgate, norm_g, tc):
    b, s, w = hq.shape
    tok_spec = pl.BlockSpec((1, tc, w), lambda bi, ci: (bi, ci, 0))
    return pl.pallas_call(
        functools.partial(_hgrn_kernel, tc=tc),
        grid=(b, s // tc),
        in_specs=[tok_spec] * 5 + [_resident((1, HGRN_DIM)), _resident((tc, tc)), _resident((tc, tc))],
        out_specs=tok_spec,
        out_shape=jax.ShapeDtypeStruct((b, s, w), BF16),
        scratch_shapes=[pltpu.VMEM((HGRN_HEADS, HGRN_DIM, HGRN_DIM), F32),
                        pltpu.VMEM((tc, w), F32), pltpu.VMEM((tc, w), F32)],
        compiler_params=pltpu.CompilerParams(dimension_semantics=("arbitrary", "arbitrary"),
                                             vmem_limit_bytes=VMEM_LIMIT),
        name="hgrn2",
    )(hq, hk, hg, hi, hgate, norm_g.reshape(1, HGRN_DIM), _block_tri(tc, HGRN_CHUNK), _block_tri(tc, HGRN_STEP))


def _memkv_kernel(mem_ref, g_ref, w_ref, k_ref, v_ref):
    d = mem_ref.shape[-1]
    kv = _dot(_rms(mem_ref[0], g_ref[...]).astype(BF16), w_ref[...])
    k_ref[0] = kv[:, :d].astype(BF16)
    v_ref[0] = kv[:, d:].astype(BF16)


def _mem_kv(mem, norm_g, w_kv):
    b, m, d = mem.shape
    out = jax.ShapeDtypeStruct((b, m, d), BF16)
    spec = pl.BlockSpec((1, m, d), lambda bi: (bi, 0, 0))
    return pl.pallas_call(
        _memkv_kernel,
        grid=(b,),
        in_specs=[spec, pl.BlockSpec((1, d), lambda bi: (0, 0)), pl.BlockSpec((d, 2 * d), lambda bi: (0, 0))],
        out_specs=[spec, spec],
        out_shape=[out, out],
        compiler_params=pltpu.CompilerParams(dimension_semantics=("arbitrary",), vmem_limit_bytes=VMEM_LIMIT),
        name="mem_kv",
    )(mem, norm_g.reshape(1, d), w_kv.astype(BF16))


def _mix_kernel(x_ref, foxt_ref, rec_ref, wo_ref, gx_ref, wq_ref, mk_ref, mv_ref, wxo_ref, o_ref,
                h1_sc, q_sc, att_sc, *, sub):
    d = x_ref.shape[-1]
    fw = foxt_ref.shape[1]
    hd = d // X_HEADS
    inv = 1.0 / math.sqrt(hd)
    n_sub = x_ref.shape[1] // sub

    def out_proj(i):
        r = slice(i * sub, (i + 1) * sub)
        h1_sc[r, :] = (x_ref[0, r, :] + _dot_tn(foxt_ref[0, :, r], wo_ref[0:fw, :])
                       + _dot(rec_ref[0, r, :], wo_ref[fw:, :]))

    def query(i):
        r = slice(i * sub, (i + 1) * sub)
        q_sc[r, :] = _dot(_rms(h1_sc[r, :], gx_ref[...]).astype(BF16), wq_ref[...]).astype(BF16)

    def attend(i):
        r = slice(i * sub, (i + 1) * sub)
        for h in range(X_HEADS):
            cs = slice(h * hd, (h + 1) * hd)
            s = _dot_nt(q_sc[r, cs], mk_ref[0, :, cs]) * inv
            p = jnp.exp(s - jnp.max(s, axis=-1, keepdims=True))
            p = p / jnp.sum(p, axis=-1, keepdims=True)
            att_sc[r, cs] = _dot(p.astype(BF16), mv_ref[0, :, cs]).astype(BF16)

    def finish(i):
        r = slice(i * sub, (i + 1) * sub)
        o_ref[0, r, :] = h1_sc[r, :] + _dot(att_sc[r, :], wxo_ref[...])

    stages = (out_proj, query, attend, finish)
    for t in range(n_sub + len(stages) - 1):
        for k, stage in enumerate(stages):
            if 0 <= t - k < n_sub:
                stage(t - k)


def _mix(x, foxt, rec, w_out, norm_g, w_xq, mem_k, mem_v, w_xo, tm):
    b, s, d = x.shape
    m = mem_k.shape[1]
    fw = foxt.shape[1]
    tok = lambda wd: pl.BlockSpec((1, tm, wd), lambda bi, ti: (bi, ti, 0))
    const = _resident
    mem_spec = pl.BlockSpec((1, m, d), lambda bi, ti: (bi, 0, 0))
    return pl.pallas_call(
        functools.partial(_mix_kernel, sub=min(tm, 256)),
        grid=(b, s // tm),
        in_specs=[tok(d), pl.BlockSpec((1, fw, tm), lambda bi, ti: (bi, 0, ti)), tok(rec.shape[-1]),
                  const((d, d)), const((1, d)), const((d, d)), mem_spec, mem_spec, const((d, d))],
        out_specs=tok(d),
        out_shape=jax.ShapeDtypeStruct((b, s, d), F32),
        scratch_shapes=[pltpu.VMEM((tm, d), F32), pltpu.VMEM((tm, d), BF16), pltpu.VMEM((tm, d), BF16)],
        compiler_params=pltpu.CompilerParams(dimension_semantics=("arbitrary", "arbitrary"),
                                             vmem_limit_bytes=VMEM_LIMIT),
        name="mix",
    )(x, foxt, rec, w_out.astype(BF16), norm_g.reshape(1, d), w_xq.astype(BF16), mem_k, mem_v, w_xo.astype(BF16))


def _mlp_kernel(h_ref, g_ref, w1_ref, w2_ref, gf_ref, o_ref, acc_sc, *, fc):
    hn = _rms(h_ref[...], g_ref[...]).astype(BF16)
    n_chunks = w1_ref.shape[1] // fc
    up = lambda c: _dot(hn, w1_ref[:, c * fc:(c + 1) * fc])
    u_next = up(0)
    for c in range(n_chunks):
        u = jnp.maximum(u_next, 0.0)
        if c + 1 < n_chunks:
            u_next = up(c + 1)
        part = _dot((u * u).astype(BF16), w2_ref[c * fc:(c + 1) * fc, :])
        if c == 0:
            acc_sc[...] = part
        else:
            acc_sc[...] += part
    o_ref[...] = _rms(h_ref[...] + acc_sc[...], gf_ref[...])


def _mlp(h, norm_g, w1, w2, final_g, tm, fc):
    t, d = h.shape
    dff = w1.shape[1]
    return pl.pallas_call(
        functools.partial(_mlp_kernel, fc=fc),
        grid=(t // tm,),
        in_specs=[pl.BlockSpec((tm, d), lambda ti: (ti, 0)), _resident((1, d)),
                  _resident((d, dff)), _resident((dff, d)), _resident((1, d))],
        out_specs=pl.BlockSpec((tm, d), lambda ti: (ti, 0)),
        out_shape=jax.ShapeDtypeStruct((t, d), F32),
        scratch_shapes=[pltpu.VMEM((tm, d), F32)],
        compiler_params=pltpu.CompilerParams(dimension_semantics=("arbitrary",), vmem_limit_bytes=VMEM_LIMIT),
        name="mlp",
    )(h, norm_g.reshape(1, d), w1.astype(BF16), w2.astype(BF16), final_g.reshape(1, d))


def _tile(n, want):
    t = min(n, want)
    assert n % t == 0, (n, want)
    return t


def kernel(x, mem, norm_mix_g, w_in, fox_f_bias, hgrn_lb_logits, hgrn_norm_g, w_out, norm_x_g, norm_mem_g,
           w_xq, w_xkv, w_xo, norm_ff_g, w1, w2, final_norm_g):
    b, s, d = x.shape
    h = x
    for l in range(w_in.shape[0]):
        q, k, v, hq, hk, hg, hi, hgate = _in_proj(h, norm_mix_g[l], w_in[l], fox_f_bias[l], hgrn_lb_logits, l,
                                                  _tile(s, 512))
        fox = _fox_attention(q, k, v, _tile(s, 512), 8)
        rec = _hgrn(hq, hk, hg, hi, hgate, hgrn_norm_g[l], _tile(s, 256))
        mem_k, mem_v = _mem_kv(mem, norm_mem_g[l], w_xkv[l])
        h = _mix(h, fox, rec, w_out[l], norm_x_g[l], w_xq[l], mem_k, mem_v, w_xo[l], _tile(s, 1024))
        is_last = l == w_in.shape[0] - 1
        assert is_last, "the MLP kernel fuses the final norm, so it must be the last layer"
        h = _mlp(h.reshape(b * s, d), norm_ff_g[l], w1[l], w2[l], final_norm_g,
                 _tile(b * s, 512), _tile(w1.shape[-1], 1024)).reshape(b, s, d)
    return h
```

```python
import functools
import math

import jax
import jax.numpy as jnp
import numpy as np
from jax import lax
from jax.experimental import pallas as pl
from jax.experimental.pallas import tpu as pltpu

EPS = 1e-6
LOG2E = math.log2(math.e)
LANES = 128
FOX_HEADS = 8
FOX_HEAD_DIM = 64
FOX_WIDTH = FOX_HEADS * FOX_HEAD_DIM
HGRN_HEADS = 4
HGRN_DIM = 128
HGRN_WIDTH = HGRN_HEADS * HGRN_DIM
X_HEADS = 4
HGRN_STEP = 16
HGRN_CHUNK = 64
HGRN_MIN_CHUNK_LOG_DECAY = -60.0
VMEM_LIMIT = 56 * 1024 * 1024

PAIR_LANES = 2 * LANES
AUG_F = 0
AUG_ONE = 3
FCAT_ONE_LANE = 24
STAT_BLOCK = 256
FOX_SKIP_LOG2 = 100.0

BF16 = jnp.bfloat16
F32 = jnp.float32


def _dot(a, b):
    return jnp.dot(a, b, preferred_element_type=F32)


def _dot_nt(a, b):
    return lax.dot_general(a, b, (((1,), (1,)), ((), ())), preferred_element_type=F32)


def _dot_tn(a, b):
    return lax.dot_general(a, b, (((0,), (0,)), ((), ())), preferred_element_type=F32)


def _split3(v):
    hi = v.astype(BF16)
    r1 = v - hi.astype(F32)
    mid = r1.astype(BF16)
    lo = (r1 - mid.astype(F32)).astype(BF16)
    return hi, mid, lo


def _tri_cumsum(tri, v):
    hi, mid, lo = _split3(v)
    return _dot(tri, hi) + _dot(tri, mid) + _dot(tri, lo)


def _rms(x, g):
    ms = jnp.mean(x * x, axis=-1, keepdims=True)
    return x * lax.rsqrt(ms + EPS) * g


def _resident(shape):
    return pl.BlockSpec(shape, lambda *_: (0,) * len(shape), pipeline_mode=pl.Buffered(1))


def _inproj_kernel(x_ref, g_ref, wqk_ref, wvt_ref, wff_ref, wh_ref, fb_ref, lbl_ref, sel_ref, tri_ref, hsum_ref,
                   q_ref, k_ref, vt_ref, stat_ref, hq_ref, hk_ref, hg_ref, hi_ref, hgate_ref, carry_ref, *, layer):
    @pl.when(pl.program_id(1) == 0)
    def _():
        carry_ref[...] = jnp.zeros_like(carry_ref)

    hb = _rms(x_ref[0], g_ref[...]).astype(BF16)
    tm = hb.shape[0]
    lane = lax.broadcasted_iota(jnp.int32, (tm, LANES), 1)

    z = _dot(hb, wff_ref[...]) + fb_ref[...]
    logf = jnp.minimum(z, 0.0) - jnp.log(1.0 + jnp.exp(-jnp.abs(z)))
    logf = jnp.where(lane < FOX_HEADS, logf, 0.0)
    fcum = _tri_cumsum(tri_ref[...], logf) + carry_ref[0:1, :]
    carry_ref[...] = jnp.broadcast_to(fcum[tm - 1:tm, :], carry_ref.shape)

    fsc = fcum * LOG2E
    f_hi = fsc.astype(BF16).astype(F32)
    r1 = fsc - f_hi
    f_mid = r1.astype(BF16).astype(F32)
    f_lo = (r1 - f_mid).astype(BF16).astype(F32)
    fcat = (f_hi + pltpu.roll(f_mid, FOX_HEADS, 1) + pltpu.roll(f_lo, 2 * FOX_HEADS, 1)
            + jnp.where(lane == FCAT_ONE_LANE, 1.0, 0.0))
    qk = _dot(hb, wqk_ref[...]).astype(BF16)
    aug = _dot(fcat.astype(BF16), sel_ref[...]).astype(BF16)
    pairs = FOX_HEADS // 2
    for p in range(pairs):
        q_ref[0, p, :, 0:LANES] = qk[:, p * LANES:(p + 1) * LANES]
        q_ref[0, p, :, LANES:PAIR_LANES] = aug[:, p * LANES:(p + 1) * LANES]
        k_ref[0, p, :, 0:LANES] = qk[:, (pairs + p) * LANES:(pairs + p + 1) * LANES]
        k_ref[0, p, :, LANES:PAIR_LANES] = aug[:, (pairs + p) * LANES:(pairs + p + 1) * LANES]

    vt_ref[0] = _dot_nt(wvt_ref[...], hb).astype(BF16)

    qkf = qk.astype(F32)
    norm2 = _dot((qkf * qkf).astype(BF16), hsum_ref[...])
    for sb in range(tm // STAT_BLOCK):
        r0 = sb * STAT_BLOCK
        stat_ref[0, sb, 0:1, :] = jnp.max(norm2[r0:r0 + STAT_BLOCK, :], axis=0, keepdims=True)
        stat_ref[0, sb, 1:2, :] = fsc[r0:r0 + 1, :]
        stat_ref[0, sb, 2:3, :] = fsc[r0 + STAT_BLOCK - 1:r0 + STAT_BLOCK, :]
        stat_ref[0, sb, 3:8, :] = jnp.zeros((5, LANES), F32)

    lbl = lbl_ref[...]
    e = jnp.exp(lbl - jnp.max(lbl, axis=0, keepdims=True))
    lb = jnp.sum(e[0:layer + 1, :], axis=0, keepdims=True) / jnp.sum(e, axis=0, keepdims=True)
    w = HGRN_WIDTH
    seg = lambda n: _dot(hb, wh_ref[:, n * w:(n + 1) * w])
    gq = seg(0)
    hq_ref[0] = gq * jax.nn.sigmoid(gq)
    f = lb + (1.0 - lb) * jax.nn.sigmoid(seg(1))
    hk_ref[0] = 1.0 - f
    hg_ref[0] = jnp.log(f)
    hi_ref[0] = seg(2).astype(BF16)
    gg = seg(3)
    hgate_ref[0] = (gg * jax.nn.sigmoid(gg)).astype(BF16)


def _in_proj(x, norm_g, w_in, fox_f_bias, lb_logits, layer, tm):
    b, s, d = x.shape
    fw, hw = FOX_WIDTH, HGRN_WIDTH
    scale = LOG2E / math.sqrt(FOX_HEAD_DIM)
    wq, wk, wv = w_in[:, 0:fw] * scale, w_in[:, fw:2 * fw], w_in[:, 2 * fw:3 * fw]
    wff = w_in[:, 3 * fw:3 * fw + FOX_HEADS]
    wh = w_in[:, 3 * fw + FOX_HEADS:]

    wqk = jnp.concatenate([wq, wk], axis=1).astype(BF16)
    wff = jnp.pad(wff, ((0, 0), (0, LANES - FOX_HEADS))).astype(BF16)
    fb = jnp.pad(fox_f_bias.reshape(1, FOX_HEADS), ((0, 0), (0, LANES - FOX_HEADS)))

    sel = np.zeros((LANES, 2 * fw), np.float32)
    for h in range(FOX_HEADS):
        col = h * FOX_HEAD_DIM
        for j in range(3):
            sel[j * FOX_HEADS + h, col + AUG_F + j] = 1.0
            sel[FCAT_ONE_LANE, col + AUG_ONE + j] = 1.0
            sel[FCAT_ONE_LANE, fw + col + AUG_F + j] = 1.0
            sel[j * FOX_HEADS + h, fw + col + AUG_ONE + j] = -1.0
    sel = jnp.asarray(sel, BF16)
    tri = jnp.asarray(np.tril(np.ones((tm, tm), np.float32)), BF16)
    hsum = np.zeros((2 * fw, LANES), np.float32)
    hsum[np.arange(2 * fw), np.arange(2 * fw) // FOX_HEAD_DIM] = 1.0
    hsum = jnp.asarray(hsum, BF16)

    const = _resident
    pairs = FOX_HEADS // 2
    head_out = jax.ShapeDtypeStruct((b, pairs, s, PAIR_LANES), BF16)
    head_spec = pl.BlockSpec((1, pairs, tm, PAIR_LANES), lambda bi, ti: (bi, 0, ti, 0))
    tok_out = lambda dt: jax.ShapeDtypeStruct((b, s, hw), dt)
    tok_spec = pl.BlockSpec((1, tm, hw), lambda bi, ti: (bi, ti, 0))
    return pl.pallas_call(
        functools.partial(_inproj_kernel, layer=layer),
        grid=(b, s // tm),
        in_specs=[pl.BlockSpec((1, tm, d), lambda bi, ti: (bi, ti, 0)),
                  const((1, d)), const(wqk.shape), const((fw, d)), const(wff.shape), const((d, 4 * hw)),
                  const(fb.shape), const(lb_logits.shape), const(sel.shape), const(tri.shape), const(hsum.shape)],
        out_specs=[head_spec, head_spec, pl.BlockSpec((1, fw, tm), lambda bi, ti: (bi, 0, ti)),
                   pl.BlockSpec((1, tm // STAT_BLOCK, 8, LANES), lambda bi, ti: (bi, ti, 0, 0)),
                   tok_spec, tok_spec, tok_spec, tok_spec, tok_spec],
        out_shape=[head_out, head_out, jax.ShapeDtypeStruct((b, fw, s), BF16),
                   jax.ShapeDtypeStruct((b, s // STAT_BLOCK, 8, LANES), F32),
                   tok_out(F32), tok_out(F32), tok_out(F32), tok_out(BF16), tok_out(BF16)],
        scratch_shapes=[pltpu.VMEM((8, LANES), F32)],
        compiler_params=pltpu.CompilerParams(dimension_semantics=("arbitrary", "arbitrary"),
                                             vmem_limit_bytes=VMEM_LIMIT),
        name="in_proj",
    )(x, norm_g.reshape(1, d), wqk, wv.T.astype(BF16), wff, wh.astype(BF16), fb, lb_logits, sel, tri, hsum)


def _fox_first_block(stats, tq):
    b, nsb = stats.shape[0], stats.shape[1]
    r = tq // STAT_BLOCK
    st = stats.reshape(b, nsb // r, r, 8, LANES)
    h = FOX_HEADS
    up = 1.0 + 2.0 ** -7
    qn = jnp.sqrt(jnp.max(st[:, :, :, 0, 0:h], axis=2) * up)
    kn = jnp.sqrt(jnp.max(st[:, :, :, 0, h:2 * h], axis=2) * up)
    f_first, f_last = st[:, :, 0, 1, 0:h], st[:, :, r - 1, 2, 0:h]
    bound = (qn[:, :, None, :] * (kn[:, None, :, :] + kn[:, :, None, :])
             + f_first[:, :, None, :] - f_last[:, None, :, :])
    blk = jnp.arange(nsb // r)
    need = jnp.any(jnp.logical_not(bound <= -FOX_SKIP_LOG2), axis=-1) & (blk[None, None, :] < blk[None, :, None])
    first = jnp.min(jnp.where(need, blk[None, None, :], nsb // r), axis=-1)
    return jnp.minimum(first, blk[None, :]).astype(jnp.int32)


def _fox_kernel(first_ref, q_ref, k_ref, vt_ref, o_ref, qm_sc, m_sc, l_sc, acc_sc, *, tq, hg):
    i = pl.program_id(2)
    hd = FOX_HEAD_DIM
    m_sc[...] = jnp.full_like(m_sc, -jnp.inf)
    l_sc[...] = jnp.zeros_like(l_sc)
    acc_sc[...] = jnp.zeros_like(acc_sc)
    lane = lax.broadcasted_iota(jnp.int32, (tq, PAIR_LANES), 1)
    for hh in range(hg):
        own = (lane // hd) % 2 == hh % 2
        qm_sc[hh] = jnp.where(own, q_ref[0, hh // 2], jnp.zeros((), BF16))

    def block(j, masked):
        off = pl.multiple_of(j * tq, tq)
        scores = lambda h_: _dot_nt(k_ref[0, h_ // 2, pl.ds(off, tq), :], qm_sc[h_])
        st_next = scores(0)
        for hh in range(hg):
            st = st_next
            if hh + 1 < hg:
                st_next = scores(hh + 1)
            if masked:
                key = lax.broadcasted_iota(jnp.int32, st.shape, 0)
                qry = lax.broadcasted_iota(jnp.int32, st.shape, 1)
                st = jnp.where(key <= qry, st, -jnp.inf)
            m_prev = m_sc[hh]
            m_new = jnp.maximum(m_prev, jnp.max(st, axis=0, keepdims=True))
            alpha = jnp.exp2(m_prev - m_new)
            p = jnp.exp2(st - m_new)
            l_sc[hh] = alpha * l_sc[hh] + jnp.sum(p, axis=0, keepdims=True)
            vt = vt_ref[0, hh * hd:(hh + 1) * hd, pl.ds(off, tq)]
            acc_sc[hh] = alpha * acc_sc[hh] + _dot(vt, p.astype(BF16))
            m_sc[hh] = m_new

    def body(j, carry):
        block(j, False)
        return carry

    lax.fori_loop(first_ref[pl.program_id(0), i], i, body, 0)
    block(i, True)
    for hh in range(hg):
        o_ref[0, hh * hd:(hh + 1) * hd, :] = (acc_sc[hh] / l_sc[hh]).astype(o_ref.dtype)


def _fox_attention(q, k, vt, first_block, tq, hg):
    b, pairs, s, _ = q.shape
    assert hg % 2 == 0
    groups = 2 * pairs // hg
    rows = hg * FOX_HEAD_DIM
    grid_spec = pltpu.PrefetchScalarGridSpec(
        num_scalar_prefetch=1,
        grid=(b, groups, s // tq),
        in_specs=[pl.BlockSpec((1, hg // 2, tq, PAIR_LANES), lambda bi, pi, qi, first: (bi, pi, qi, 0)),
                  pl.BlockSpec((1, hg // 2, s, PAIR_LANES), lambda bi, pi, qi, first: (bi, pi, 0, 0)),
                  pl.BlockSpec((1, rows, s), lambda bi, pi, qi, first: (bi, pi, 0))],
        out_specs=pl.BlockSpec((1, rows, tq), lambda bi, pi, qi, first: (bi, pi, qi)),
        scratch_shapes=[pltpu.VMEM((hg, tq, PAIR_LANES), BF16),
                        pltpu.VMEM((hg, 1, tq), F32), pltpu.VMEM((hg, 1, tq), F32),
                        pltpu.VMEM((hg, FOX_HEAD_DIM, tq), F32)])
    return pl.pallas_call(
        functools.partial(_fox_kernel, tq=tq, hg=hg),
        grid_spec=grid_spec,
        out_shape=jax.ShapeDtypeStruct((b, groups * rows, s), BF16),
        compiler_params=pltpu.CompilerParams(dimension_semantics=("arbitrary",) * 3,
                                             vmem_limit_bytes=VMEM_LIMIT),
        name="fox_attention",
    )(first_block, q, k, vt)


def _hgrn_chunk_path(q_ref, k_ref, i_ref, st_sc, b_sc, o_sc, tc):
    c_len = HGRN_CHUNK
    t_idx = lax.broadcasted_iota(jnp.int32, (c_len, c_len), 0)
    s_idx = lax.broadcasted_iota(jnp.int32, (c_len, c_len), 1)
    causal = s_idx <= t_idx
    for c in range(tc // c_len):
        rows = slice(c * c_len, (c + 1) * c_len)
        for h in range(HGRN_HEADS):
            ls = slice(h * HGRN_DIM, (h + 1) * HGRN_DIM)
            b = b_sc[rows, ls]
            qq = q_ref[0, rows, ls]
            kk = k_ref[0, rows, ls]
            ii = i_ref[0, rows, ls]
            b_last = b[c_len - 1:c_len, :]
            qe = (qq * jnp.exp(b)).astype(BF16)
            ke = (kk * jnp.exp(-b)).astype(BF16)
            kl = (kk * jnp.exp(b_last - b)).astype(BF16)
            attn = jnp.where(causal, _dot_nt(qe, ke), 0.0).astype(BF16)
            st = st_sc[h]
            o_sc[rows, ls] = _dot(attn, ii) + _dot_nt(qe, st.astype(BF16))
            st_sc[h] = st * jnp.exp(b_last) + _dot_tn(ii, kl)


def _hgrn_step_path(q_ref, k_ref, i_ref, st_sc, b_sc, o_sc, tc):
    row8 = lax.broadcasted_iota(jnp.int32, (8, HGRN_DIM), 0)
    half = HGRN_STEP // 2

    def step(u, carry):
        base = pl.multiple_of(u * HGRN_STEP, HGRN_STEP)
        for h in range(HGRN_HEADS):
            ls = slice(h * HGRN_DIM, (h + 1) * HGRN_DIM)
            rows = pl.ds(base, HGRN_STEP)
            bq = b_sc[rows, ls]
            qq = q_ref[0, rows, ls]
            kk = k_ref[0, rows, ls]
            ii = i_ref[0, rows, ls].astype(F32)
            b_last = bq[HGRN_STEP - 1:HGRN_STEP, :]
            st = st_sc[h]
            inter = _dot_nt((qq * jnp.exp(bq)).astype(BF16), st.astype(BF16))
            k_dec = (kk * jnp.exp(b_last - bq)).astype(BF16)
            st_sc[h] = st * jnp.exp(b_last) + _dot_tn(ii.astype(BF16), k_dec)
            out = [inter[0:half], inter[half:HGRN_STEP]]
            qv = [qq[0:half], qq[half:HGRN_STEP]]
            bv = [bq[0:half], bq[half:HGRN_STEP]]
            for s_ in range(HGRN_STEP):
                k_s, b_s, i_s = kk[s_:s_ + 1, :], bq[s_:s_ + 1, :], ii[s_:s_ + 1, :]
                for v_ in range(s_ // half, 2):
                    diff = bv[v_] - b_s
                    if s_ > v_ * half:
                        diff = jnp.where(row8 + v_ * half >= s_, diff, -jnp.inf)
                    a = jnp.sum(qv[v_] * k_s * jnp.exp(diff), axis=-1, keepdims=True)
                    out[v_] = out[v_] + a * i_s
            o_sc[pl.ds(base, half), ls] = out[0]
            o_sc[pl.ds(base + half, half), ls] = out[1]
        return carry

    lax.fori_loop(0, tc // HGRN_STEP, step, 0)


def _hgrn_kernel(q_ref, k_ref, g_ref, i_ref, gate_ref, ng_ref, tri64_ref, tri16_ref, o_ref,
                 st_sc, b_sc, o_sc, *, tc):
    @pl.when(pl.program_id(1) == 0)
    def _():
        st_sc[...] = jnp.zeros_like(st_sc)

    b64 = _tri_cumsum(tri64_ref[...], g_ref[0])
    b_sc[...] = b64
    chunk_ok = jnp.min(b64) >= HGRN_MIN_CHUNK_LOG_DECAY

    @pl.when(chunk_ok)
    def _():
        _hgrn_chunk_path(q_ref, k_ref, i_ref, st_sc, b_sc, o_sc, tc)

    @pl.when(jnp.logical_not(chunk_ok))
    def _():
        b_sc[...] = _tri_cumsum(tri16_ref[...], g_ref[0])
        _hgrn_step_path(q_ref, k_ref, i_ref, st_sc, b_sc, o_sc, tc)

    for h in range(HGRN_HEADS):
        ls = slice(h * HGRN_DIM, (h + 1) * HGRN_DIM)
        o_ref[0, :, ls] = (_rms(o_sc[:, ls], ng_ref[...]) * gate_ref[0, :, ls]).astype(o_ref.dtype)


def _block_tri(n, blk):
    return jnp.asarray(np.kron(np.eye(n // blk, dtype=np.float32), np.tril(np.ones((blk, blk), np.float32))), BF16)


def _hgrn(hq, hk, hg, hi, hgate, norm_g, tc):
    b, s, w = hq.shape
    tok_spec = pl.BlockSpec((1, tc, w), lambda bi, ci: (bi, ci, 0))
    return pl.pallas_call(
        functools.partial(_hgrn_kernel, tc=tc),
        grid=(b, s // tc),
        in_specs=[tok_spec] * 5 + [_resident((1, HGRN_DIM)), _resident((tc, tc)), _resident((tc, tc))],
        out_specs=tok_spec,
        out_shape=jax.ShapeDtypeStruct((b, s, w), BF16),
        scratch_shapes=[pltpu.VMEM((HGRN_HEADS, HGRN_DIM, HGRN_DIM), F32),
                        pltpu.VMEM((tc, w), F32), pltpu.VMEM((tc, w), F32)],
        compiler_params=pltpu.CompilerParams(dimension_semantics=("arbitrary", "arbitrary"),
                                             vmem_limit_bytes=VMEM_LIMIT),
        name="hgrn2",
    )(hq, hk, hg, hi, hgate, norm_g.reshape(1, HGRN_DIM), _block_tri(tc, HGRN_CHUNK), _block_tri(tc, HGRN_STEP))


def _memkv_kernel(mem_ref, g_ref, w_ref, k_ref, v_ref):
    d = mem_ref.shape[-1]
    kv = _dot(_rms(mem_ref[0], g_ref[...]).astype(BF16), w_ref[...])
    k_ref[0] = kv[:, :d].astype(BF16)
    v_ref[0] = kv[:, d:].astype(BF16)


def _mem_kv(mem, norm_g, w_kv):
    b, m, d = mem.shape
    out = jax.ShapeDtypeStruct((b, m, d), BF16)
    spec = pl.BlockSpec((1, m, d), lambda bi: (bi, 0, 0))
    return pl.pallas_call(
        _memkv_kernel,
        grid=(b,),
        in_specs=[spec, pl.BlockSpec((1, d), lambda bi: (0, 0)), pl.BlockSpec((d, 2 * d), lambda bi: (0, 0))],
        out_specs=[spec, spec],
        out_shape=[out, out],
        compiler_params=pltpu.CompilerParams(dimension_semantics=("arbitrary",), vmem_limit_bytes=VMEM_LIMIT),
        name="mem_kv",
    )(mem, norm_g.reshape(1, d), w_kv.astype(BF16))


def _mix_kernel(x_ref, foxt_ref, rec_ref, wo_ref, gx_ref, wq_ref, mk_ref, mv_ref, wxo_ref, o_ref,
                h1_sc, q_sc, att_sc, *, sub):
    d = x_ref.shape[-1]
    fw = foxt_ref.shape[1]
    hd = d // X_HEADS
    inv = 1.0 / math.sqrt(hd)
    n_sub = x_ref.shape[1] // sub

    def out_proj(i):
        r = slice(i * sub, (i + 1) * sub)
        h1_sc[r, :] = (x_ref[0, r, :] + _dot_tn(foxt_ref[0, :, r], wo_ref[0:fw, :])
                       + _dot(rec_ref[0, r, :], wo_ref[fw:, :]))

    def query(i):
        r = slice(i * sub, (i + 1) * sub)
        q_sc[r, :] = _dot(_rms(h1_sc[r, :], gx_ref[...]).astype(BF16), wq_ref[...]).astype(BF16)

    def attend(i):
        r = slice(i * sub, (i + 1) * sub)
        for h in range(X_HEADS):
            cs = slice(h * hd, (h + 1) * hd)
            s = _dot_nt(q_sc[r, cs], mk_ref[0, :, cs]) * inv
            p = jnp.exp(s - jnp.max(s, axis=-1, keepdims=True))
            p = p / jnp.sum(p, axis=-1, keepdims=True)
            att_sc[r, cs] = _dot(p.astype(BF16), mv_ref[0, :, cs]).astype(BF16)

    def finish(i):
        r = slice(i * sub, (i + 1) * sub)
        o_ref[0, r, :] = h1_sc[r, :] + _dot(att_sc[r, :], wxo_ref[...])

    stages = (out_proj, query, attend, finish)
    for t in range(n_sub + len(stages) - 1):
        for k, stage in enumerate(stages):
            if 0 <= t - k < n_sub:
                stage(t - k)


def _mix(x, foxt, rec, w_out, norm_g, w_xq, mem_k, mem_v, w_xo, tm):
    b, s, d = x.shape
    m = mem_k.shape[1]
    fw = foxt.shape[1]
    tok = lambda wd: pl.BlockSpec((1, tm, wd), lambda bi, ti: (bi, ti, 0))
    const = _resident
    mem_spec = pl.BlockSpec((1, m, d), lambda bi, ti: (bi, 0, 0))
    return pl.pallas_call(
        functools.partial(_mix_kernel, sub=min(tm, 256)),
        grid=(b, s // tm),
        in_specs=[tok(d), pl.BlockSpec((1, fw, tm), lambda bi, ti: (bi, 0, ti)), tok(rec.shape[-1]),
                  const((d, d)), const((1, d)), const((d, d)), mem_spec, mem_spec, const((d, d))],
        out_specs=tok(d),
        out_shape=jax.ShapeDtypeStruct((b, s, d), F32),
        scratch_shapes=[pltpu.VMEM((tm, d), F32), pltpu.VMEM((tm, d), BF16), pltpu.VMEM((tm, d), BF16)],
        compiler_params=pltpu.CompilerParams(dimension_semantics=("arbitrary", "arbitrary"),
                                             vmem_limit_bytes=VMEM_LIMIT),
        name="mix",
    )(x, foxt, rec, w_out.astype(BF16), norm_g.reshape(1, d), w_xq.astype(BF16), mem_k, mem_v, w_xo.astype(BF16))


def _mlp_kernel(h_ref, g_ref, w1_ref, w2_ref, gf_ref, o_ref, acc_sc, *, fc):
    hn = _rms(h_ref[...], g_ref[...]).astype(BF16)
    n_chunks = w1_ref.shape[1] // fc
    up = lambda c: _dot(hn, w1_ref[:, c * fc:(c + 1) * fc])
    u_next = up(0)
    for c in range(n_chunks):
        u = jnp.maximum(u_next, 0.0)
        if c + 1 < n_chunks:
            u_next = up(c + 1)
        part = _dot((u * u).astype(BF16), w2_ref[c * fc:(c + 1) * fc, :])
        if c == 0:
            acc_sc[...] = part
        else:
            acc_sc[...] += part
    o_ref[...] = _rms(h_ref[...] + acc_sc[...], gf_ref[...])


def _mlp(h, norm_g, w1, w2, final_g, tm, fc):
    t, d = h.shape
    dff = w1.shape[1]
    return pl.pallas_call(
        functools.partial(_mlp_kernel, fc=fc),
        grid=(t // tm,),
        in_specs=[pl.BlockSpec((tm, d), lambda ti: (ti, 0)), _resident((1, d)),
                  _resident((d, dff)), _resident((dff, d)), _resident((1, d))],
        out_specs=pl.BlockSpec((tm, d), lambda ti: (ti, 0)),
        out_shape=jax.ShapeDtypeStruct((t, d), F32),
        scratch_shapes=[pltpu.VMEM((tm, d), F32)],
        compiler_params=pltpu.CompilerParams(dimension_semantics=("arbitrary",), vmem_limit_bytes=VMEM_LIMIT),
        name="mlp",
    )(h, norm_g.reshape(1, d), w1.astype(BF16), w2.astype(BF16), final_g.reshape(1, d))


def _tile(n, want):
    t = min(n, want)
    assert n % t == 0, (n, want)
    return t


def kernel(x, mem, norm_mix_g, w_in, fox_f_bias, hgrn_lb_logits, hgrn_norm_g, w_out, norm_x_g, norm_mem_g,
           w_xq, w_xkv, w_xo, norm_ff_g, w1, w2, final_norm_g):
    b, s, d = x.shape
    h = x
    for l in range(w_in.shape[0]):
        q, k, v, stats, hq, hk, hg, hi, hgate = _in_proj(h, norm_mix_g[l], w_in[l], fox_f_bias[l], hgrn_lb_logits, l,
                                                  _tile(s, 512))
        tq = _tile(s, 512)
        fox = _fox_attention(q, k, v, _fox_first_block(stats, tq), tq, 8)
        rec = _hgrn(hq, hk, hg, hi, hgate, hgrn_norm_g[l], _tile(s, 256))
        mem_k, mem_v = _mem_kv(mem, norm_mem_g[l], w_xkv[l])
        h = _mix(h, fox, rec, w_out[l], norm_x_g[l], w_xq[l], mem_k, mem_v, w_xo[l], _tile(s, 1024))
        is_last = l == w_in.shape[0] - 1
        assert is_last, "the MLP kernel fuses the final norm, so it must be the last layer"
        h = _mlp(h.reshape(b * s, d), norm_ff_g[l], w1[l], w2[l], final_norm_g,
                 _tile(b * s, 512), _tile(w1.shape[-1], 1024)).reshape(b, s, d)
    return h
```

```python
import functools
import math

import jax
import jax.numpy as jnp
import numpy as np
from jax import lax
from jax.experimental import pallas as pl
from jax.experimental.pallas import tpu as pltpu

EPS = 1e-6
LOG2E = math.log2(math.e)
LANES = 128
FOX_HEADS = 8
FOX_HEAD_DIM = 64
FOX_WIDTH = FOX_HEADS * FOX_HEAD_DIM
HGRN_HEADS = 4
HGRN_DIM = 128
HGRN_WIDTH = HGRN_HEADS * HGRN_DIM
X_HEADS = 4
HGRN_STEP = 16
HGRN_CHUNK = 64
HGRN_MIN_CHUNK_LOG_DECAY = -60.0
VMEM_LIMIT = 56 * 1024 * 1024

PAIR_LANES = 2 * LANES
AUG_F = 0
AUG_ONE = 3
FCAT_ONE_LANE = 24
STAT_BLOCK = 256
FOX_SKIP_LOG2 = 100.0

BF16 = jnp.bfloat16
F32 = jnp.float32


def _dot(a, b):
    return jnp.dot(a, b, preferred_element_type=F32)


def _dot_nt(a, b):
    return lax.dot_general(a, b, (((1,), (1,)), ((), ())), preferred_element_type=F32)


def _dot_tn(a, b):
    return lax.dot_general(a, b, (((0,), (0,)), ((), ())), preferred_element_type=F32)


def _split3(v):
    hi = v.astype(BF16)
    r1 = v - hi.astype(F32)
    mid = r1.astype(BF16)
    lo = (r1 - mid.astype(F32)).astype(BF16)
    return hi, mid, lo


def _tri_cumsum(tri, v):
    hi, mid, lo = _split3(v)
    return _dot(tri, hi) + _dot(tri, mid) + _dot(tri, lo)


def _rms(x, g):
    ms = jnp.mean(x * x, axis=-1, keepdims=True)
    return x * lax.rsqrt(ms + EPS) * g


def _resident(shape):
    return pl.BlockSpec(shape, lambda *_: (0,) * len(shape), pipeline_mode=pl.Buffered(1))


def _inproj_kernel(x_ref, g_ref, wqk_ref, wvt_ref, wff_ref, wh_ref, fb_ref, lbl_ref, sel_ref, tri_ref, hsum_ref,
                   q_ref, k_ref, vt_ref, stat_ref, hq_ref, hk_ref, hg_ref, hi_ref, hgate_ref, carry_ref, *, layer):
    @pl.when(pl.program_id(1) == 0)
    def _():
        carry_ref[...] = jnp.zeros_like(carry_ref)

    hb = _rms(x_ref[0], g_ref[...]).astype(BF16)
    tm = hb.shape[0]
    lane = lax.broadcasted_iota(jnp.int32, (tm, LANES), 1)

    z = _dot(hb, wff_ref[...]) + fb_ref[...]
    logf = jnp.minimum(z, 0.0) - jnp.log(1.0 + jnp.exp(-jnp.abs(z)))
    logf = jnp.where(lane < FOX_HEADS, logf, 0.0)
    fcum = _tri_cumsum(tri_ref[...], logf) + carry_ref[0:1, :]
    carry_ref[...] = jnp.broadcast_to(fcum[tm - 1:tm, :], carry_ref.shape)

    fsc = fcum * LOG2E
    f_hi = fsc.astype(BF16).astype(F32)
    r1 = fsc - f_hi
    f_mid = r1.astype(BF16).astype(F32)
    f_lo = (r1 - f_mid).astype(BF16).astype(F32)
    fcat = (f_hi + pltpu.roll(f_mid, FOX_HEADS, 1) + pltpu.roll(f_lo, 2 * FOX_HEADS, 1)
            + jnp.where(lane == FCAT_ONE_LANE, 1.0, 0.0))
    qk = _dot(hb, wqk_ref[...]).astype(BF16)
    aug = _dot(fcat.astype(BF16), sel_ref[...]).astype(BF16)
    pairs = FOX_HEADS // 2
    for p in range(pairs):
        q_ref[0, p, :, 0:LANES] = qk[:, p * LANES:(p + 1) * LANES]
        q_ref[0, p, :, LANES:PAIR_LANES] = aug[:, p * LANES:(p + 1) * LANES]
        k_ref[0, p, :, 0:LANES] = qk[:, (pairs + p) * LANES:(pairs + p + 1) * LANES]
        k_ref[0, p, :, LANES:PAIR_LANES] = aug[:, (pairs + p) * LANES:(pairs + p + 1) * LANES]

    vt_ref[0] = _dot_nt(wvt_ref[...], hb).astype(BF16)

    qkf = qk.astype(F32)
    norm2 = _dot((qkf * qkf).astype(BF16), hsum_ref[...])
    for sb in range(tm // STAT_BLOCK):
        r0 = sb * STAT_BLOCK
        stat_ref[0, sb, 0:1, :] = jnp.max(norm2[r0:r0 + STAT_BLOCK, :], axis=0, keepdims=True)
        stat_ref[0, sb, 1:2, :] = fsc[r0:r0 + 1, :]
        stat_ref[0, sb, 2:3, :] = fsc[r0 + STAT_BLOCK - 1:r0 + STAT_BLOCK, :]
        stat_ref[0, sb, 3:8, :] = jnp.zeros((5, LANES), F32)

    lbl = lbl_ref[...]
    e = jnp.exp(lbl - jnp.max(lbl, axis=0, keepdims=True))
    lb = jnp.sum(e[0:layer + 1, :], axis=0, keepdims=True) / jnp.sum(e, axis=0, keepdims=True)
    w = HGRN_WIDTH
    seg = lambda n: _dot(hb, wh_ref[:, n * w:(n + 1) * w])
    gq = seg(0)
    hq_ref[0] = gq * jax.nn.sigmoid(gq)
    f = lb + (1.0 - lb) * jax.nn.sigmoid(seg(1))
    hk_ref[0] = 1.0 - f
    hg_ref[0] = jnp.log(f)
    hi_ref[0] = seg(2).astype(BF16)
    gg = seg(3)
    hgate_ref[0] = (gg * jax.nn.sigmoid(gg)).astype(BF16)


def _in_proj(x, norm_g, w_in, fox_f_bias, lb_logits, layer, tm):
    b, s, d = x.shape
    fw, hw = FOX_WIDTH, HGRN_WIDTH
    scale = LOG2E / math.sqrt(FOX_HEAD_DIM)
    wq, wk, wv = w_in[:, 0:fw] * scale, w_in[:, fw:2 * fw], w_in[:, 2 * fw:3 * fw]
    wff = w_in[:, 3 * fw:3 * fw + FOX_HEADS]
    wh = w_in[:, 3 * fw + FOX_HEADS:]

    wqk = jnp.concatenate([wq, wk], axis=1).astype(BF16)
    wff = jnp.pad(wff, ((0, 0), (0, LANES - FOX_HEADS))).astype(BF16)
    fb = jnp.pad(fox_f_bias.reshape(1, FOX_HEADS), ((0, 0), (0, LANES - FOX_HEADS)))

    sel = np.zeros((LANES, 2 * fw), np.float32)
    for h in range(FOX_HEADS):
        col = h * FOX_HEAD_DIM
        for j in range(3):
            sel[j * FOX_HEADS + h, col + AUG_F + j] = 1.0
            sel[FCAT_ONE_LANE, col + AUG_ONE + j] = 1.0
            sel[FCAT_ONE_LANE, fw + col + AUG_F + j] = 1.0
            sel[j * FOX_HEADS + h, fw + col + AUG_ONE + j] = -1.0
    sel = jnp.asarray(sel, BF16)
    tri = jnp.asarray(np.tril(np.ones((tm, tm), np.float32)), BF16)
    hsum = np.zeros((2 * fw, LANES), np.float32)
    hsum[np.arange(2 * fw), np.arange(2 * fw) // FOX_HEAD_DIM] = 1.0
    hsum = jnp.asarray(hsum, BF16)

    const = _resident
    pairs = FOX_HEADS // 2
    head_out = jax.ShapeDtypeStruct((b, pairs, s, PAIR_LANES), BF16)
    head_spec = pl.BlockSpec((1, pairs, tm, PAIR_LANES), lambda bi, ti: (bi, 0, ti, 0))
    tok_out = lambda dt: jax.ShapeDtypeStruct((b, s, hw), dt)
    tok_spec = pl.BlockSpec((1, tm, hw), lambda bi, ti: (bi, ti, 0))
    return pl.pallas_call(
        functools.partial(_inproj_kernel, layer=layer),
        grid=(b, s // tm),
        in_specs=[pl.BlockSpec((1, tm, d), lambda bi, ti: (bi, ti, 0)),
                  const((1, d)), const(wqk.shape), const((fw, d)), const(wff.shape), const((d, 4 * hw)),
                  const(fb.shape), const(lb_logits.shape), const(sel.shape), const(tri.shape), const(hsum.shape)],
        out_specs=[head_spec, head_spec, pl.BlockSpec((1, fw, tm), lambda bi, ti: (bi, 0, ti)),
                   pl.BlockSpec((1, tm // STAT_BLOCK, 8, LANES), lambda bi, ti: (bi, ti, 0, 0)),
                   tok_spec, tok_spec, tok_spec, tok_spec, tok_spec],
        out_shape=[head_out, head_out, jax.ShapeDtypeStruct((b, fw, s), BF16),
                   jax.ShapeDtypeStruct((b, s // STAT_BLOCK, 8, LANES), F32),
                   tok_out(F32), tok_out(F32), tok_out(F32), tok_out(BF16), tok_out(BF16)],
        scratch_shapes=[pltpu.VMEM((8, LANES), F32)],
        compiler_params=pltpu.CompilerParams(dimension_semantics=("arbitrary", "arbitrary"),
                                             vmem_limit_bytes=VMEM_LIMIT),
        name="in_proj",
    )(x, norm_g.reshape(1, d), wqk, wv.T.astype(BF16), wff, wh.astype(BF16), fb, lb_logits, sel, tri, hsum)


def _fox_first_block(stats, tq):
    b, nsb = stats.shape[0], stats.shape[1]
    r = tq // STAT_BLOCK
    st = stats.reshape(b, nsb // r, r, 8, LANES)
    h = FOX_HEADS
    up = 1.0 + 2.0 ** -7
    qn = jnp.sqrt(jnp.max(st[:, :, :, 0, 0:h], axis=2) * up)
    kn = jnp.sqrt(jnp.max(st[:, :, :, 0, h:2 * h], axis=2) * up)
    f_first, f_last = st[:, :, 0, 1, 0:h], st[:, :, r - 1, 2, 0:h]
    bound = (qn[:, :, None, :] * (kn[:, None, :, :] + kn[:, :, None, :])
             + f_first[:, :, None, :] - f_last[:, None, :, :])
    blk = jnp.arange(nsb // r)
    need = jnp.any(jnp.logical_not(bound <= -FOX_SKIP_LOG2), axis=-1) & (blk[None, None, :] < blk[None, :, None])
    first = jnp.min(jnp.where(need, blk[None, None, :], nsb // r), axis=-1)
    return jnp.minimum(first, blk[None, :]).astype(jnp.int32)


def _fox_kernel(first_ref, q_ref, k_ref, vt_ref, o_ref, qm_sc, m_sc, l_sc, acc_sc, *, tq, hg):
    i = pl.program_id(2)
    hd = FOX_HEAD_DIM
    m_sc[...] = jnp.full_like(m_sc, -jnp.inf)
    l_sc[...] = jnp.zeros_like(l_sc)
    acc_sc[...] = jnp.zeros_like(acc_sc)
    lane = lax.broadcasted_iota(jnp.int32, (tq, PAIR_LANES), 1)
    for hh in range(hg):
        own = (lane // hd) % 2 == hh % 2
        qm_sc[hh] = jnp.where(own, q_ref[0, hh // 2], jnp.zeros((), BF16))

    def block(j, masked):
        off = pl.multiple_of(j * tq, tq)
        scores = lambda h_: _dot_nt(k_ref[0, h_ // 2, pl.ds(off, tq), :], qm_sc[h_])
        st_next = scores(0)
        for hh in range(hg):
            st = st_next
            if hh + 1 < hg:
                st_next = scores(hh + 1)
            if masked:
                key = lax.broadcasted_iota(jnp.int32, st.shape, 0)
                qry = lax.broadcasted_iota(jnp.int32, st.shape, 1)
                st = jnp.where(key <= qry, st, -jnp.inf)
            m_prev = m_sc[hh]
            m_new = jnp.maximum(m_prev, jnp.max(st, axis=0, keepdims=True))
            alpha = jnp.exp2(m_prev - m_new)
            p = jnp.exp2(st - m_new)
            l_sc[hh] = alpha * l_sc[hh] + jnp.sum(p, axis=0, keepdims=True)
            vt = vt_ref[0, hh * hd:(hh + 1) * hd, pl.ds(off, tq)]
            acc_sc[hh] = alpha * acc_sc[hh] + _dot(vt, p.astype(BF16))
            m_sc[hh] = m_new

    def body(j, carry):
        block(j, False)
        return carry

    lax.fori_loop(first_ref[pl.program_id(0), i], i, body, 0)
    block(i, True)
    for hh in range(hg):
        o_ref[0, hh * hd:(hh + 1) * hd, :] = (acc_sc[hh] / l_sc[hh]).astype(o_ref.dtype)


def _fox_attention(q, k, vt, first_block, tq, hg):
    b, pairs, s, _ = q.shape
    assert hg % 2 == 0
    groups = 2 * pairs // hg
    rows = hg * FOX_HEAD_DIM
    grid_spec = pltpu.PrefetchScalarGridSpec(
        num_scalar_prefetch=1,
        grid=(b, groups, s // tq),
        in_specs=[pl.BlockSpec((1, hg // 2, tq, PAIR_LANES), lambda bi, pi, qi, first: (bi, pi, qi, 0)),
                  pl.BlockSpec((1, hg // 2, s, PAIR_LANES), lambda bi, pi, qi, first: (bi, pi, 0, 0)),
                  pl.BlockSpec((1, rows, s), lambda bi, pi, qi, first: (bi, pi, 0))],
        out_specs=pl.BlockSpec((1, rows, tq), lambda bi, pi, qi, first: (bi, pi, qi)),
        scratch_shapes=[pltpu.VMEM((hg, tq, PAIR_LANES), BF16),
                        pltpu.VMEM((hg, 1, tq), F32), pltpu.VMEM((hg, 1, tq), F32),
                        pltpu.VMEM((hg, FOX_HEAD_DIM, tq), F32)])
    return pl.pallas_call(
        functools.partial(_fox_kernel, tq=tq, hg=hg),
        grid_spec=grid_spec,
        out_shape=jax.ShapeDtypeStruct((b, groups * rows, s), BF16),
        compiler_params=pltpu.CompilerParams(dimension_semantics=("arbitrary",) * 3,
                                             vmem_limit_bytes=VMEM_LIMIT),
        name="fox_attention",
    )(first_block, q, k, vt)


def _hgrn_chunk_path(q_ref, k_ref, i_ref, st_sc, b_sc, o_sc, tc):
    c_len = HGRN_CHUNK
    n_chunks = tc // c_len
    t_idx = lax.broadcasted_iota(jnp.int32, (c_len, c_len), 0)
    s_idx = lax.broadcasted_iota(jnp.int32, (c_len, c_len), 1)
    causal = s_idx <= t_idx
    units = [(c, h) for c in range(n_chunks) for h in range(HGRN_HEADS)]
    rows = lambda c: slice(c * c_len, (c + 1) * c_len)
    lanes = lambda h: slice(h * HGRN_DIM, (h + 1) * HGRN_DIM)

    qe, ke, kl, decay = {}, {}, {}, {}
    for c, h in units:
        b = b_sc[rows(c), lanes(h)]
        kk = k_ref[0, rows(c), lanes(h)]
        b_last = b[c_len - 1:c_len, :]
        qe[c, h] = (q_ref[0, rows(c), lanes(h)] * jnp.exp(b)).astype(BF16)
        ke[c, h] = (kk * jnp.exp(-b)).astype(BF16)
        kl[c, h] = (kk * jnp.exp(b_last - b)).astype(BF16)
        decay[c, h] = jnp.exp(b_last)
    attn = {u: jnp.where(causal, _dot_nt(qe[u], ke[u]), 0.0).astype(BF16) for u in units}
    intra = {(c, h): _dot(attn[c, h], i_ref[0, rows(c), lanes(h)]) for c, h in units}
    d_state = {(c, h): _dot_tn(i_ref[0, rows(c), lanes(h)], kl[c, h]) for c, h in units}
    state = {}
    for h in range(HGRN_HEADS):
        st = st_sc[h]
        for c in range(n_chunks):
            state[c, h] = st.astype(BF16)
            st = st * decay[c, h] + d_state[c, h]
        st_sc[h] = st
    for c, h in units:
        o_sc[rows(c), lanes(h)] = intra[c, h] + _dot_nt(qe[c, h], state[c, h])


def _hgrn_step_path(q_ref, k_ref, i_ref, st_sc, b_sc, o_sc, tc):
    row8 = lax.broadcasted_iota(jnp.int32, (8, HGRN_DIM), 0)
    half = HGRN_STEP // 2

    def step(u, carry):
        base = pl.multiple_of(u * HGRN_STEP, HGRN_STEP)
        for h in range(HGRN_HEADS):
            ls = slice(h * HGRN_DIM, (h + 1) * HGRN_DIM)
            rows = pl.ds(base, HGRN_STEP)
            bq = b_sc[rows, ls]
            qq = q_ref[0, rows, ls]
            kk = k_ref[0, rows, ls]
            ii = i_ref[0, rows, ls].astype(F32)
            b_last = bq[HGRN_STEP - 1:HGRN_STEP, :]
            st = st_sc[h]
            inter = _dot_nt((qq * jnp.exp(bq)).astype(BF16), st.astype(BF16))
            k_dec = (kk * jnp.exp(b_last - bq)).astype(BF16)
            st_sc[h] = st * jnp.exp(b_last) + _dot_tn(ii.astype(BF16), k_dec)
            out = [inter[0:half], inter[half:HGRN_STEP]]
            qv = [qq[0:half], qq[half:HGRN_STEP]]
            bv = [bq[0:half], bq[half:HGRN_STEP]]
            for s_ in range(HGRN_STEP):
                k_s, b_s, i_s = kk[s_:s_ + 1, :], bq[s_:s_ + 1, :], ii[s_:s_ + 1, :]
                for v_ in range(s_ // half, 2):
                    diff = bv[v_] - b_s
                    if s_ > v_ * half:
                        diff = jnp.where(row8 + v_ * half >= s_, diff, -jnp.inf)
                    a = jnp.sum(qv[v_] * k_s * jnp.exp(diff), axis=-1, keepdims=True)
                    out[v_] = out[v_] + a * i_s
            o_sc[pl.ds(base, half), ls] = out[0]
            o_sc[pl.ds(base + half, half), ls] = out[1]
        return carry

    lax.fori_loop(0, tc // HGRN_STEP, step, 0)


def _hgrn_kernel(q_ref, k_ref, g_ref, i_ref, gate_ref, ng_ref, tri64_ref, tri16_ref, o_ref,
                 st_sc, b_sc, o_sc, *, tc):
    @pl.when(pl.program_id(1) == 0)
    def _():
        st_sc[...] = jnp.zeros_like(st_sc)

    b64 = _tri_cumsum(tri64_ref[...], g_ref[0])
    b_sc[...] = b64
    chunk_ok = jnp.min(b64) >= HGRN_MIN_CHUNK_LOG_DECAY

    @pl.when(chunk_ok)
    def _():
        _hgrn_chunk_path(q_ref, k_ref, i_ref, st_sc, b_sc, o_sc, tc)

    @pl.when(jnp.logical_not(chunk_ok))
    def _():
        b_sc[...] = _tri_cumsum(tri16_ref[...], g_ref[0])
        _hgrn_step_path(q_ref, k_ref, i_ref, st_sc, b_sc, o_sc, tc)

    for h in range(HGRN_HEADS):
        ls = slice(h * HGRN_DIM, (h + 1) * HGRN_DIM)
        o_ref[0, :, ls] = (_rms(o_sc[:, ls], ng_ref[...]) * gate_ref[0, :, ls]).astype(o_ref.dtype)


def _block_tri(n, blk):
    return jnp.asarray(np.kron(np.eye(n // blk, dtype=np.float32), np.tril(np.ones((blk, blk), np.float32))), BF16)


def _hgrn(hq, hk, hg, hi, hgate, norm_g, tc):
    b, s, w = hq.shape
    tok_spec = pl.BlockSpec((1, tc, w), lambda bi, ci: (bi, ci, 0))
    return pl.pallas_call(
        functools.partial(_hgrn_kernel, tc=tc),
        grid=(b, s // tc),
        in_specs=[tok_spec] * 5 + [_resident((1, HGRN_DIM)), _resident((tc, tc)), _resident((tc, tc))],
        out_specs=tok_spec,
        out_shape=jax.ShapeDtypeStruct((b, s, w), BF16),
        scratch_shapes=[pltpu.VMEM((HGRN_HEADS, HGRN_DIM, HGRN_DIM), F32),
                        pltpu.VMEM((tc, w), F32), pltpu.VMEM((tc, w), F32)],
        compiler_params=pltpu.CompilerParams(dimension_semantics=("arbitrary", "arbitrary"),
                                             vmem_limit_bytes=VMEM_LIMIT),
        name="hgrn2",
    )(hq, hk, hg, hi, hgate, norm_g.reshape(1, HGRN_DIM), _block_tri(tc, HGRN_CHUNK), _block_tri(tc, HGRN_STEP))


def _memkv_kernel(mem_ref, g_ref, w_ref, k_ref, v_ref):
    d = mem_ref.shape[-1]
    kv = _dot(_rms(mem_ref[0], g_ref[...]).astype(BF16), w_ref[...])
    k_ref[0] = kv[:, :d].astype(BF16)
    v_ref[0] = kv[:, d:].astype(BF16)


def _mem_kv(mem, norm_g, w_kv):
    b, m, d = mem.shape
    out = jax.ShapeDtypeStruct((b, m, d), BF16)
    spec = pl.BlockSpec((1, m, d), lambda bi: (bi, 0, 0))
    return pl.pallas_call(
        _memkv_kernel,
        grid=(b,),
        in_specs=[spec, pl.BlockSpec((1, d), lambda bi: (0, 0)), pl.BlockSpec((d, 2 * d), lambda bi: (0, 0))],
        out_specs=[spec, spec],
        out_shape=[out, out],
        compiler_params=pltpu.CompilerParams(dimension_semantics=("arbitrary",), vmem_limit_bytes=VMEM_LIMIT),
        name="mem_kv",
    )(mem, norm_g.reshape(1, d), w_kv.astype(BF16))


def _mix_kernel(x_ref, foxt_ref, rec_ref, wo_ref, gx_ref, wq_ref, mk_ref, mv_ref, wxo_ref, o_ref,
                h1_sc, q_sc, att_sc, *, sub):
    d = x_ref.shape[-1]
    fw = foxt_ref.shape[1]
    hd = d // X_HEADS
    inv = 1.0 / math.sqrt(hd)
    n_sub = x_ref.shape[1] // sub

    def out_proj(i):
        r = slice(i * sub, (i + 1) * sub)
        h1_sc[r, :] = (x_ref[0, r, :] + _dot_tn(foxt_ref[0, :, r], wo_ref[0:fw, :])
                       + _dot(rec_ref[0, r, :], wo_ref[fw:, :]))

    def query(i):
        r = slice(i * sub, (i + 1) * sub)
        q_sc[r, :] = _dot(_rms(h1_sc[r, :], gx_ref[...]).astype(BF16), wq_ref[...]).astype(BF16)

    def attend(i):
        r = slice(i * sub, (i + 1) * sub)
        for h in range(X_HEADS):
            cs = slice(h * hd, (h + 1) * hd)
            s = _dot_nt(q_sc[r, cs], mk_ref[0, :, cs]) * inv
            p = jnp.exp(s - jnp.max(s, axis=-1, keepdims=True))
            p = p / jnp.sum(p, axis=-1, keepdims=True)
            att_sc[r, cs] = _dot(p.astype(BF16), mv_ref[0, :, cs]).astype(BF16)

    def finish(i):
        r = slice(i * sub, (i + 1) * sub)
        o_ref[0, r, :] = h1_sc[r, :] + _dot(att_sc[r, :], wxo_ref[...])

    stages = (out_proj, query, attend, finish)
    for t in range(n_sub + len(stages) - 1):
        for k, stage in enumerate(stages):
            if 0 <= t - k < n_sub:
                stage(t - k)


def _mix(x, foxt, rec, w_out, norm_g, w_xq, mem_k, mem_v, w_xo, tm):
    b, s, d = x.shape
    m = mem_k.shape[1]
    fw = foxt.shape[1]
    tok = lambda wd: pl.BlockSpec((1, tm, wd), lambda bi, ti: (bi, ti, 0))
    const = _resident
    mem_spec = pl.BlockSpec((1, m, d), lambda bi, ti: (bi, 0, 0))
    return pl.pallas_call(
        functools.partial(_mix_kernel, sub=min(tm, 256)),
        grid=(b, s // tm),
        in_specs=[tok(d), pl.BlockSpec((1, fw, tm), lambda bi, ti: (bi, 0, ti)), tok(rec.shape[-1]),
                  const((d, d)), const((1, d)), const((d, d)), mem_spec, mem_spec, const((d, d))],
        out_specs=tok(d),
        out_shape=jax.ShapeDtypeStruct((b, s, d), F32),
        scratch_shapes=[pltpu.VMEM((tm, d), F32), pltpu.VMEM((tm, d), BF16), pltpu.VMEM((tm, d), BF16)],
        compiler_params=pltpu.CompilerParams(dimension_semantics=("arbitrary", "arbitrary"),
                                             vmem_limit_bytes=VMEM_LIMIT),
        name="mix",
    )(x, foxt, rec, w_out.astype(BF16), norm_g.reshape(1, d), w_xq.astype(BF16), mem_k, mem_v, w_xo.astype(BF16))


def _mlp_kernel(h_ref, g_ref, w1_ref, w2_ref, gf_ref, o_ref, acc_sc, *, fc):
    hn = _rms(h_ref[...], g_ref[...]).astype(BF16)
    n_chunks = w1_ref.shape[1] // fc
    up = lambda c: _dot(hn, w1_ref[:, c * fc:(c + 1) * fc])
    u_next = up(0)
    for c in range(n_chunks):
        u = jnp.maximum(u_next, 0.0)
        if c + 1 < n_chunks:
            u_next = up(c + 1)
        part = _dot((u * u).astype(BF16), w2_ref[c * fc:(c + 1) * fc, :])
        if c == 0:
            acc_sc[...] = part
        else:
            acc_sc[...] += part
    o_ref[...] = _rms(h_ref[...] + acc_sc[...], gf_ref[...])


def _mlp(h, norm_g, w1, w2, final_g, tm, fc):
    t, d = h.shape
    dff = w1.shape[1]
    return pl.pallas_call(
        functools.partial(_mlp_kernel, fc=fc),
        grid=(t // tm,),
        in_specs=[pl.BlockSpec((tm, d), lambda ti: (ti, 0)), _resident((1, d)),
                  _resident((d, dff)), _resident((dff, d)), _resident((1, d))],
        out_specs=pl.BlockSpec((tm, d), lambda ti: (ti, 0)),
        out_shape=jax.ShapeDtypeStruct((t, d), F32),
        scratch_shapes=[pltpu.VMEM((tm, d), F32)],
        compiler_params=pltpu.CompilerParams(dimension_semantics=("arbitrary",), vmem_limit_bytes=VMEM_LIMIT),
        name="mlp",
    )(h, norm_g.reshape(1, d), w1.astype(BF16), w2.astype(BF16), final_g.reshape(1, d))


def _tile(n, want):
    t = min(n, want)
    assert n % t == 0, (n, want)
    return t


def kernel(x, mem, norm_mix_g, w_in, fox_f_bias, hgrn_lb_logits, hgrn_norm_g, w_out, norm_x_g, norm_mem_g,
           w_xq, w_xkv, w_xo, norm_ff_g, w1, w2, final_norm_g):
    b, s, d = x.shape
    h = x
    for l in range(w_in.shape[0]):
        q, k, v, stats, hq, hk, hg, hi, hgate = _in_proj(h, norm_mix_g[l], w_in[l], fox_f_bias[l], hgrn_lb_logits, l,
                                                  _tile(s, 512))
        tq = _tile(s, 512)
        fox = _fox_attention(q, k, v, _fox_first_block(stats, tq), tq, 8)
        rec = _hgrn(hq, hk, hg, hi, hgate, hgrn_norm_g[l], _tile(s, 256))
        mem_k, mem_v = _mem_kv(mem, norm_mem_g[l], w_xkv[l])
        h = _mix(h, fox, rec, w_out[l], norm_x_g[l], w_xq[l], mem_k, mem_v, w_xo[l], _tile(s, 1024))
        is_last = l == w_in.shape[0] - 1
        assert is_last, "the MLP kernel fuses the final norm, so it must be the last layer"
        h = _mlp(h.reshape(b * s, d), norm_ff_g[l], w1[l], w2[l], final_norm_g,
                 _tile(b * s, 512), _tile(w1.shape[-1], 1024)).reshape(b, s, d)
    return h
```

```python
import functools
import math

import jax
import jax.numpy as jnp
import numpy as np
from jax import lax
from jax.experimental import pallas as pl
from jax.experimental.pallas import tpu as pltpu

EPS = 1e-6
LOG2E = math.log2(math.e)
LANES = 128
FOX_HEADS = 8
FOX_HEAD_DIM = 64
FOX_WIDTH = FOX_HEADS * FOX_HEAD_DIM
HGRN_HEADS = 4
HGRN_DIM = 128
HGRN_WIDTH = HGRN_HEADS * HGRN_DIM
X_HEADS = 4
HGRN_STEP = 16
HGRN_CHUNK = 64
HGRN_MIN_CHUNK_LOG_DECAY = -60.0
VMEM_LIMIT = 56 * 1024 * 1024

PAIR_LANES = 2 * LANES
AUG_F = 0
AUG_ONE = 3
FCAT_ONE_LANE = 24
STAT_BLOCK = 256
FOX_SKIP_LOG2 = 100.0

BF16 = jnp.bfloat16
F32 = jnp.float32


def _dot(a, b):
    return jnp.dot(a, b, preferred_element_type=F32)


def _dot_nt(a, b):
    return lax.dot_general(a, b, (((1,), (1,)), ((), ())), preferred_element_type=F32)


def _dot_tn(a, b):
    return lax.dot_general(a, b, (((0,), (0,)), ((), ())), preferred_element_type=F32)


def _split3(v):
    hi = v.astype(BF16)
    r1 = v - hi.astype(F32)
    mid = r1.astype(BF16)
    lo = (r1 - mid.astype(F32)).astype(BF16)
    return hi, mid, lo


def _tri_cumsum(tri, v):
    hi, mid, lo = _split3(v)
    return _dot(tri, hi) + _dot(tri, mid) + _dot(tri, lo)


def _rms(x, g):
    ms = jnp.mean(x * x, axis=-1, keepdims=True)
    return x * lax.rsqrt(ms + EPS) * g


def _resident(shape):
    return pl.BlockSpec(shape, lambda *_: (0,) * len(shape), pipeline_mode=pl.Buffered(1))


def _inproj_kernel(x_ref, g_ref, wqk_ref, wvt_ref, wff_ref, wh_ref, fb_ref, lbl_ref, sel_ref, tri_ref, hsum_ref,
                   q_ref, k_ref, vt_ref, stat_ref, hq_ref, hk_ref, hg_ref, hi_ref, hgate_ref, carry_ref, *, layer):
    @pl.when(pl.program_id(1) == 0)
    def _():
        carry_ref[...] = jnp.zeros_like(carry_ref)

    hb = _rms(x_ref[0], g_ref[...]).astype(BF16)
    tm = hb.shape[0]
    lane = lax.broadcasted_iota(jnp.int32, (tm, LANES), 1)

    z = _dot_nt(hb, wff_ref[...]) + fb_ref[...]
    logf = jnp.minimum(z, 0.0) - jnp.log(1.0 + jnp.exp(-jnp.abs(z)))
    logf = jnp.where(lane < FOX_HEADS, logf, 0.0)
    fcum = _tri_cumsum(tri_ref[...], logf) + carry_ref[0:1, :]
    carry_ref[...] = jnp.broadcast_to(fcum[tm - 1:tm, :], carry_ref.shape)

    fsc = fcum * LOG2E
    f_hi = fsc.astype(BF16).astype(F32)
    r1 = fsc - f_hi
    f_mid = r1.astype(BF16).astype(F32)
    f_lo = (r1 - f_mid).astype(BF16).astype(F32)
    fcat = (f_hi + pltpu.roll(f_mid, FOX_HEADS, 1) + pltpu.roll(f_lo, 2 * FOX_HEADS, 1)
            + jnp.where(lane == FCAT_ONE_LANE, 1.0, 0.0))
    qk = _dot_nt(hb, wqk_ref[...]).astype(BF16)
    aug = _dot(fcat.astype(BF16), sel_ref[...]).astype(BF16)
    pairs = FOX_HEADS // 2
    for p in range(pairs):
        q_ref[0, p, :, 0:LANES] = qk[:, p * LANES:(p + 1) * LANES]
        q_ref[0, p, :, LANES:PAIR_LANES] = aug[:, p * LANES:(p + 1) * LANES]
        k_ref[0, p, :, 0:LANES] = qk[:, (pairs + p) * LANES:(pairs + p + 1) * LANES]
        k_ref[0, p, :, LANES:PAIR_LANES] = aug[:, (pairs + p) * LANES:(pairs + p + 1) * LANES]

    vt_ref[0] = _dot_nt(wvt_ref[...], hb).astype(BF16)

    qkf = qk.astype(F32)
    norm2 = _dot((qkf * qkf).astype(BF16), hsum_ref[...])
    for sb in range(tm // STAT_BLOCK):
        r0 = sb * STAT_BLOCK
        stat_ref[0, sb, 0:1, :] = jnp.max(norm2[r0:r0 + STAT_BLOCK, :], axis=0, keepdims=True)
        stat_ref[0, sb, 1:2, :] = fsc[r0:r0 + 1, :]
        stat_ref[0, sb, 2:3, :] = fsc[r0 + STAT_BLOCK - 1:r0 + STAT_BLOCK, :]
        stat_ref[0, sb, 3:8, :] = jnp.zeros((5, LANES), F32)

    lbl = lbl_ref[...]
    e = jnp.exp(lbl - jnp.max(lbl, axis=0, keepdims=True))
    lb = jnp.sum(e[0:layer + 1, :], axis=0, keepdims=True) / jnp.sum(e, axis=0, keepdims=True)
    w = HGRN_WIDTH
    seg = lambda n: _dot_nt(hb, wh_ref[n * w:(n + 1) * w, :])
    gq = seg(0)
    hq_ref[0] = gq * jax.nn.sigmoid(gq)
    f = lb + (1.0 - lb) * jax.nn.sigmoid(seg(1))
    hk_ref[0] = 1.0 - f
    hg_ref[0] = jnp.log(f)
    hi_ref[0] = seg(2).astype(BF16)
    gg = seg(3)
    hgate_ref[0] = (gg * jax.nn.sigmoid(gg)).astype(BF16)


def _in_proj(x, norm_g, w_in, fox_f_bias, lb_logits, layer, tm):
    b, s, d = x.shape
    fw, hw = FOX_WIDTH, HGRN_WIDTH
    assert w_in.shape == (d, 3 * fw + FOX_HEADS + 4 * hw)
    wt = w_in.T
    scale = LOG2E / math.sqrt(FOX_HEAD_DIM)
    wqk = jnp.concatenate([wt[0:fw] * scale, wt[fw:2 * fw]], axis=0).astype(BF16)
    wvt = wt[2 * fw:3 * fw].astype(BF16)
    wff = jnp.pad(wt[3 * fw:3 * fw + FOX_HEADS], ((0, LANES - FOX_HEADS), (0, 0))).astype(BF16)
    wh = wt[3 * fw + FOX_HEADS:].astype(BF16)
    fb = jnp.pad(fox_f_bias.reshape(1, FOX_HEADS), ((0, 0), (0, LANES - FOX_HEADS)))

    sel = np.zeros((LANES, 2 * fw), np.float32)
    for h in range(FOX_HEADS):
        col = h * FOX_HEAD_DIM
        for j in range(3):
            sel[j * FOX_HEADS + h, col + AUG_F + j] = 1.0
            sel[FCAT_ONE_LANE, col + AUG_ONE + j] = 1.0
            sel[FCAT_ONE_LANE, fw + col + AUG_F + j] = 1.0
            sel[j * FOX_HEADS + h, fw + col + AUG_ONE + j] = -1.0
    sel = jnp.asarray(sel, BF16)
    tri = jnp.asarray(np.tril(np.ones((tm, tm), np.float32)), BF16)
    hsum = np.zeros((2 * fw, LANES), np.float32)
    hsum[np.arange(2 * fw), np.arange(2 * fw) // FOX_HEAD_DIM] = 1.0
    hsum = jnp.asarray(hsum, BF16)

    const = _resident
    pairs = FOX_HEADS // 2
    head_out = jax.ShapeDtypeStruct((b, pairs, s, PAIR_LANES), BF16)
    head_spec = pl.BlockSpec((1, pairs, tm, PAIR_LANES), lambda bi, ti: (bi, 0, ti, 0))
    tok_out = lambda dt: jax.ShapeDtypeStruct((b, s, hw), dt)
    tok_spec = pl.BlockSpec((1, tm, hw), lambda bi, ti: (bi, ti, 0))
    return pl.pallas_call(
        functools.partial(_inproj_kernel, layer=layer),
        grid=(b, s // tm),
        in_specs=[pl.BlockSpec((1, tm, d), lambda bi, ti: (bi, ti, 0)),
                  const((1, d)), const(wqk.shape), const(wvt.shape), const(wff.shape), const(wh.shape),
                  const(fb.shape), const(lb_logits.shape), const(sel.shape), const(tri.shape), const(hsum.shape)],
        out_specs=[head_spec, head_spec, pl.BlockSpec((1, fw, tm), lambda bi, ti: (bi, 0, ti)),
                   pl.BlockSpec((1, tm // STAT_BLOCK, 8, LANES), lambda bi, ti: (bi, ti, 0, 0)),
                   tok_spec, tok_spec, tok_spec, tok_spec, tok_spec],
        out_shape=[head_out, head_out, jax.ShapeDtypeStruct((b, fw, s), BF16),
                   jax.ShapeDtypeStruct((b, s // STAT_BLOCK, 8, LANES), F32),
                   tok_out(F32), tok_out(F32), tok_out(F32), tok_out(BF16), tok_out(BF16)],
        scratch_shapes=[pltpu.VMEM((8, LANES), F32)],
        compiler_params=pltpu.CompilerParams(dimension_semantics=("arbitrary", "arbitrary"),
                                             vmem_limit_bytes=VMEM_LIMIT),
        name="in_proj",
    )(x, norm_g.reshape(1, d), wqk, wvt, wff, wh, fb, lb_logits, sel, tri, hsum)


def _fox_first_block(stats, tq):
    b, nsb = stats.shape[0], stats.shape[1]
    r = tq // STAT_BLOCK
    st = stats.reshape(b, nsb // r, r, 8, LANES)
    h = FOX_HEADS
    up = 1.0 + 2.0 ** -7
    qn = jnp.sqrt(jnp.max(st[:, :, :, 0, 0:h], axis=2) * up)
    kn = jnp.sqrt(jnp.max(st[:, :, :, 0, h:2 * h], axis=2) * up)
    f_first, f_last = st[:, :, 0, 1, 0:h], st[:, :, r - 1, 2, 0:h]
    bound = (qn[:, :, None, :] * (kn[:, None, :, :] + kn[:, :, None, :])
             + f_first[:, :, None, :] - f_last[:, None, :, :])
    blk = jnp.arange(nsb // r)
    need = jnp.any(jnp.logical_not(bound <= -FOX_SKIP_LOG2), axis=-1) & (blk[None, None, :] < blk[None, :, None])
    first = jnp.min(jnp.where(need, blk[None, None, :], nsb // r), axis=-1)
    return jnp.minimum(first, blk[None, :]).astype(jnp.int32)


def _fox_kernel(first_ref, q_ref, k_ref, vt_ref, o_ref, qm_sc, m_sc, l_sc, acc_sc, *, tq, hg):
    i = pl.program_id(2)
    hd = FOX_HEAD_DIM
    m_sc[...] = jnp.full_like(m_sc, -jnp.inf)
    l_sc[...] = jnp.zeros_like(l_sc)
    acc_sc[...] = jnp.zeros_like(acc_sc)
    lane = lax.broadcasted_iota(jnp.int32, (tq, PAIR_LANES), 1)
    for hh in range(hg):
        own = (lane // hd) % 2 == hh % 2
        qm_sc[hh] = jnp.where(own, q_ref[0, hh // 2], jnp.zeros((), BF16))

    def block(j, masked):
        off = pl.multiple_of(j * tq, tq)
        scores = lambda h_: _dot_nt(k_ref[0, h_ // 2, pl.ds(off, tq), :], qm_sc[h_])
        ahead = 2
        pending = [scores(h_) for h_ in range(min(ahead, hg))]
        for hh in range(hg):
            st = pending.pop(0)
            if hh + ahead < hg:
                pending.append(scores(hh + ahead))
            if masked:
                key = lax.broadcasted_iota(jnp.int32, st.shape, 0)
                qry = lax.broadcasted_iota(jnp.int32, st.shape, 1)
                st = jnp.where(key <= qry, st, -jnp.inf)
            m_prev = m_sc[hh]
            m_new = jnp.maximum(m_prev, jnp.max(st, axis=0, keepdims=True))
            alpha = jnp.exp2(m_prev - m_new)
            p = jnp.exp2(st - m_new)
            l_sc[hh] = alpha * l_sc[hh] + jnp.sum(p, axis=0, keepdims=True)
            vt = vt_ref[0, hh * hd:(hh + 1) * hd, pl.ds(off, tq)]
            acc_sc[hh] = alpha * acc_sc[hh] + _dot(vt, p.astype(BF16))
            m_sc[hh] = m_new

    def body(j, carry):
        block(j, False)
        return carry

    lax.fori_loop(first_ref[pl.program_id(0), i], i, body, 0)
    block(i, True)
    for hh in range(hg):
        o_ref[0, hh * hd:(hh + 1) * hd, :] = (acc_sc[hh] / l_sc[hh]).astype(o_ref.dtype)


def _fox_attention(q, k, vt, first_block, tq, hg):
    b, pairs, s, _ = q.shape
    assert hg % 2 == 0
    groups = 2 * pairs // hg
    rows = hg * FOX_HEAD_DIM
    grid_spec = pltpu.PrefetchScalarGridSpec(
        num_scalar_prefetch=1,
        grid=(b, groups, s // tq),
        in_specs=[pl.BlockSpec((1, hg // 2, tq, PAIR_LANES), lambda bi, pi, qi, first: (bi, pi, qi, 0)),
                  pl.BlockSpec((1, hg // 2, s, PAIR_LANES), lambda bi, pi, qi, first: (bi, pi, 0, 0)),
                  pl.BlockSpec((1, rows, s), lambda bi, pi, qi, first: (bi, pi, 0))],
        out_specs=pl.BlockSpec((1, rows, tq), lambda bi, pi, qi, first: (bi, pi, qi)),
        scratch_shapes=[pltpu.VMEM((hg, tq, PAIR_LANES), BF16),
                        pltpu.VMEM((hg, 1, tq), F32), pltpu.VMEM((hg, 1, tq), F32),
                        pltpu.VMEM((hg, FOX_HEAD_DIM, tq), F32)])
    return pl.pallas_call(
        functools.partial(_fox_kernel, tq=tq, hg=hg),
        grid_spec=grid_spec,
        out_shape=jax.ShapeDtypeStruct((b, groups * rows, s), BF16),
        compiler_params=pltpu.CompilerParams(dimension_semantics=("arbitrary",) * 3,
                                             vmem_limit_bytes=VMEM_LIMIT),
        name="fox_attention",
    )(first_block, q, k, vt)


def _hgrn_chunk_path(q_ref, k_ref, i_ref, st_sc, b_sc, o_sc, tc):
    c_len = HGRN_CHUNK
    n_chunks = tc // c_len
    t_idx = lax.broadcasted_iota(jnp.int32, (c_len, c_len), 0)
    s_idx = lax.broadcasted_iota(jnp.int32, (c_len, c_len), 1)
    causal = s_idx <= t_idx
    units = [(c, h) for c in range(n_chunks) for h in range(HGRN_HEADS)]
    rows = lambda c: slice(c * c_len, (c + 1) * c_len)
    lanes = lambda h: slice(h * HGRN_DIM, (h + 1) * HGRN_DIM)

    qe, ke, kl, decay = {}, {}, {}, {}
    for c, h in units:
        b = b_sc[rows(c), lanes(h)]
        kk = k_ref[0, rows(c), lanes(h)]
        b_last = b[c_len - 1:c_len, :]
        qe[c, h] = (q_ref[0, rows(c), lanes(h)] * jnp.exp(b)).astype(BF16)
        ke[c, h] = (kk * jnp.exp(-b)).astype(BF16)
        kl[c, h] = (kk * jnp.exp(b_last - b)).astype(BF16)
        decay[c, h] = jnp.exp(b_last)
    attn = {u: jnp.where(causal, _dot_nt(qe[u], ke[u]), 0.0).astype(BF16) for u in units}
    intra = {(c, h): _dot(attn[c, h], i_ref[0, rows(c), lanes(h)]) for c, h in units}
    d_state = {(c, h): _dot_tn(i_ref[0, rows(c), lanes(h)], kl[c, h]) for c, h in units}
    state = {}
    for h in range(HGRN_HEADS):
        st = st_sc[h]
        for c in range(n_chunks):
            state[c, h] = st.astype(BF16)
            st = st * decay[c, h] + d_state[c, h]
        st_sc[h] = st
    for c, h in units:
        o_sc[rows(c), lanes(h)] = intra[c, h] + _dot_nt(qe[c, h], state[c, h])


def _hgrn_step_path(q_ref, k_ref, i_ref, st_sc, b_sc, o_sc, tc):
    row8 = lax.broadcasted_iota(jnp.int32, (8, HGRN_DIM), 0)
    half = HGRN_STEP // 2

    def step(u, carry):
        base = pl.multiple_of(u * HGRN_STEP, HGRN_STEP)
        for h in range(HGRN_HEADS):
            ls = slice(h * HGRN_DIM, (h + 1) * HGRN_DIM)
            rows = pl.ds(base, HGRN_STEP)
            bq = b_sc[rows, ls]
            qq = q_ref[0, rows, ls]
            kk = k_ref[0, rows, ls]
            ii = i_ref[0, rows, ls].astype(F32)
            b_last = bq[HGRN_STEP - 1:HGRN_STEP, :]
            st = st_sc[h]
            inter = _dot_nt((qq * jnp.exp(bq)).astype(BF16), st.astype(BF16))
            k_dec = (kk * jnp.exp(b_last - bq)).astype(BF16)
            st_sc[h] = st * jnp.exp(b_last) + _dot_tn(ii.astype(BF16), k_dec)
            out = [inter[0:half], inter[half:HGRN_STEP]]
            qv = [qq[0:half], qq[half:HGRN_STEP]]
            bv = [bq[0:half], bq[half:HGRN_STEP]]
            for s_ in range(HGRN_STEP):
                k_s, b_s, i_s = kk[s_:s_ + 1, :], bq[s_:s_ + 1, :], ii[s_:s_ + 1, :]
                for v_ in range(s_ // half, 2):
                    diff = bv[v_] - b_s
                    if s_ > v_ * half:
                        diff = jnp.where(row8 + v_ * half >= s_, diff, -jnp.inf)
                    a = jnp.sum(qv[v_] * k_s * jnp.exp(diff), axis=-1, keepdims=True)
                    out[v_] = out[v_] + a * i_s
            o_sc[pl.ds(base, half), ls] = out[0]
            o_sc[pl.ds(base + half, half), ls] = out[1]
        return carry

    lax.fori_loop(0, tc // HGRN_STEP, step, 0)


def _hgrn_kernel(q_ref, k_ref, g_ref, i_ref, gate_ref, ng_ref, tri64_ref, tri16_ref, o_ref,
                 st_sc, b_sc, o_sc, *, tc):
    @pl.when(pl.program_id(1) == 0)
    def _():
        st_sc[...] = jnp.zeros_like(st_sc)

    b64 = _tri_cumsum(tri64_ref[...], g_ref[0])
    b_sc[...] = b64
    chunk_ok = jnp.min(b64) >= HGRN_MIN_CHUNK_LOG_DECAY

    @pl.when(chunk_ok)
    def _():
        _hgrn_chunk_path(q_ref, k_ref, i_ref, st_sc, b_sc, o_sc, tc)

    @pl.when(jnp.logical_not(chunk_ok))
    def _():
        b_sc[...] = _tri_cumsum(tri16_ref[...], g_ref[0])
        _hgrn_step_path(q_ref, k_ref, i_ref, st_sc, b_sc, o_sc, tc)

    for h in range(HGRN_HEADS):
        ls = slice(h * HGRN_DIM, (h + 1) * HGRN_DIM)
        o_ref[0, :, ls] = (_rms(o_sc[:, ls], ng_ref[...]) * gate_ref[0, :, ls]).astype(o_ref.dtype)


def _block_tri(n, blk):
    return jnp.asarray(np.kron(np.eye(n // blk, dtype=np.float32), np.tril(np.ones((blk, blk), np.float32))), BF16)


def _hgrn(hq, hk, hg, hi, hgate, norm_g, tc):
    b, s, w = hq.shape
    tok_spec = pl.BlockSpec((1, tc, w), lambda bi, ci: (bi, ci, 0))
    return pl.pallas_call(
        functools.partial(_hgrn_kernel, tc=tc),
        grid=(b, s // tc),
        in_specs=[tok_spec] * 5 + [_resident((1, HGRN_DIM)), _resident((tc, tc)), _resident((tc, tc))],
        out_specs=tok_spec,
        out_shape=jax.ShapeDtypeStruct((b, s, w), BF16),
        scratch_shapes=[pltpu.VMEM((HGRN_HEADS, HGRN_DIM, HGRN_DIM), F32),
                        pltpu.VMEM((tc, w), F32), pltpu.VMEM((tc, w), F32)],
        compiler_params=pltpu.CompilerParams(dimension_semantics=("arbitrary", "arbitrary"),
                                             vmem_limit_bytes=VMEM_LIMIT),
        name="hgrn2",
    )(hq, hk, hg, hi, hgate, norm_g.reshape(1, HGRN_DIM), _block_tri(tc, HGRN_CHUNK), _block_tri(tc, HGRN_STEP))


def _memkv_kernel(mem_ref, g_ref, w_ref, k_ref, v_ref):
    d = mem_ref.shape[-1]
    kv = _dot(_rms(mem_ref[0], g_ref[...]).astype(BF16), w_ref[...].astype(BF16))
    k_ref[0] = kv[:, :d].astype(BF16)
    v_ref[0] = kv[:, d:].astype(BF16)


def _mem_kv(mem, norm_g, w_kv):
    b, m, d = mem.shape
    out = jax.ShapeDtypeStruct((b, m, d), BF16)
    spec = pl.BlockSpec((1, m, d), lambda bi: (bi, 0, 0))
    return pl.pallas_call(
        _memkv_kernel,
        grid=(b,),
        in_specs=[spec, pl.BlockSpec((1, d), lambda bi: (0, 0)), pl.BlockSpec((d, 2 * d), lambda bi: (0, 0))],
        out_specs=[spec, spec],
        out_shape=[out, out],
        compiler_params=pltpu.CompilerParams(dimension_semantics=("arbitrary",), vmem_limit_bytes=VMEM_LIMIT),
        name="mem_kv",
    )(mem, norm_g.reshape(1, d), w_kv)


def _mix_kernel(x_ref, foxt_ref, rec_ref, wo_ref, gx_ref, wq_ref, mk_ref, mv_ref, wxo_ref, o_ref,
                h1_sc, q_sc, att_sc, *, sub):
    d = x_ref.shape[-1]
    fw = foxt_ref.shape[1]
    hd = d // X_HEADS
    inv = 1.0 / math.sqrt(hd)
    n_sub = x_ref.shape[1] // sub

    def out_proj(i):
        r = slice(i * sub, (i + 1) * sub)
        h1_sc[r, :] = (x_ref[0, r, :] + _dot_tn(foxt_ref[0, :, r], wo_ref[0:fw, :].astype(BF16))
                       + _dot(rec_ref[0, r, :], wo_ref[fw:, :].astype(BF16)))

    def query(i):
        r = slice(i * sub, (i + 1) * sub)
        q_sc[r, :] = _dot(_rms(h1_sc[r, :], gx_ref[...]).astype(BF16), wq_ref[...].astype(BF16)).astype(BF16)

    def attend(i):
        r = slice(i * sub, (i + 1) * sub)
        for h in range(X_HEADS):
            cs = slice(h * hd, (h + 1) * hd)
            s = _dot_nt(q_sc[r, cs], mk_ref[0, :, cs]) * inv
            p = jnp.exp(s - jnp.max(s, axis=-1, keepdims=True))
            p = p / jnp.sum(p, axis=-1, keepdims=True)
            att_sc[r, cs] = _dot(p.astype(BF16), mv_ref[0, :, cs]).astype(BF16)

    def finish(i):
        r = slice(i * sub, (i + 1) * sub)
        o_ref[0, r, :] = h1_sc[r, :] + _dot(att_sc[r, :], wxo_ref[...].astype(BF16))

    stages = (out_proj, query, attend, finish)
    for t in range(n_sub + len(stages) - 1):
        for k, stage in enumerate(stages):
            if 0 <= t - k < n_sub:
                stage(t - k)


def _mix(x, foxt, rec, w_out, norm_g, w_xq, mem_k, mem_v, w_xo, tm):
    b, s, d = x.shape
    m = mem_k.shape[1]
    fw = foxt.shape[1]
    tok = lambda wd: pl.BlockSpec((1, tm, wd), lambda bi, ti: (bi, ti, 0))
    const = _resident
    mem_spec = pl.BlockSpec((1, m, d), lambda bi, ti: (bi, 0, 0))
    return pl.pallas_call(
        functools.partial(_mix_kernel, sub=min(tm, 256)),
        grid=(b, s // tm),
        in_specs=[tok(d), pl.BlockSpec((1, fw, tm), lambda bi, ti: (bi, 0, ti)), tok(rec.shape[-1]),
                  const((d, d)), const((1, d)), const((d, d)), mem_spec, mem_spec, const((d, d))],
        out_specs=tok(d),
        out_shape=jax.ShapeDtypeStruct((b, s, d), F32),
        scratch_shapes=[pltpu.VMEM((tm, d), F32), pltpu.VMEM((tm, d), BF16), pltpu.VMEM((tm, d), BF16)],
        compiler_params=pltpu.CompilerParams(dimension_semantics=("arbitrary", "arbitrary"),
                                             vmem_limit_bytes=VMEM_LIMIT),
        name="mix",
    )(x, foxt, rec, w_out, norm_g.reshape(1, d), w_xq, mem_k, mem_v, w_xo)


def _mlp_kernel(h_ref, g_ref, w1_ref, w2_ref, gf_ref, o_ref, acc_sc, *, fc):
    hn = _rms(h_ref[...], g_ref[...]).astype(BF16)
    n_chunks = w1_ref.shape[1] // fc
    up = lambda c: _dot(hn, w1_ref[:, c * fc:(c + 1) * fc].astype(BF16))
    u_next = up(0)
    for c in range(n_chunks):
        u = jnp.maximum(u_next, 0.0)
        if c + 1 < n_chunks:
            u_next = up(c + 1)
        part = _dot((u * u).astype(BF16), w2_ref[c * fc:(c + 1) * fc, :].astype(BF16))
        if c == 0:
            acc_sc[...] = part
        else:
            acc_sc[...] += part
    o_ref[...] = _rms(h_ref[...] + acc_sc[...], gf_ref[...])


def _mlp(h, norm_g, w1, w2, final_g, tm, fc):
    t, d = h.shape
    dff = w1.shape[1]
    return pl.pallas_call(
        functools.partial(_mlp_kernel, fc=fc),
        grid=(t // tm,),
        in_specs=[pl.BlockSpec((tm, d), lambda ti: (ti, 0)), _resident((1, d)),
                  _resident((d, dff)), _resident((dff, d)), _resident((1, d))],
        out_specs=pl.BlockSpec((tm, d), lambda ti: (ti, 0)),
        out_shape=jax.ShapeDtypeStruct((t, d), F32),
        scratch_shapes=[pltpu.VMEM((tm, d), F32)],
        compiler_params=pltpu.CompilerParams(dimension_semantics=("arbitrary",), vmem_limit_bytes=VMEM_LIMIT),
        name="mlp",
    )(h, norm_g.reshape(1, d), w1, w2, final_g.reshape(1, d))


def _tile(n, want):
    t = min(n, want)
    assert n % t == 0, (n, want)
    return t


def kernel(x, mem, norm_mix_g, w_in, fox_f_bias, hgrn_lb_logits, hgrn_norm_g, w_out, norm_x_g, norm_mem_g,
           w_xq, w_xkv, w_xo, norm_ff_g, w1, w2, final_norm_g):
    b, s, d = x.shape
    h = x
    for l in range(w_in.shape[0]):
        q, k, v, stats, hq, hk, hg, hi, hgate = _in_proj(h, norm_mix_g[l], w_in[l], fox_f_bias[l], hgrn_lb_logits, l,
                                                  _tile(s, 512))
        tq = _tile(s, 512)
        fox = _fox_attention(q, k, v, _fox_first_block(stats, tq), tq, 8)
        rec = _hgrn(hq, hk, hg, hi, hgate, hgrn_norm_g[l], _tile(s, 256))
        mem_k, mem_v = _mem_kv(mem, norm_mem_g[l], w_xkv[l])
        h = _mix(h, fox, rec, w_out[l], norm_x_g[l], w_xq[l], mem_k, mem_v, w_xo[l], _tile(s, 1024))
        is_last = l == w_in.shape[0] - 1
        assert is_last, "the MLP kernel fuses the final norm, so it must be the last layer"
        h = _mlp(h.reshape(b * s, d), norm_ff_g[l], w1[l], w2[l], final_norm_g,
                 _tile(b * s, 512), _tile(w1.shape[-1], 1024)).reshape(b, s, d)
    return h
```

```python
import functools
import math

import jax
import jax.numpy as jnp
import numpy as np
from jax import lax
from jax.experimental import pallas as pl
from jax.experimental.pallas import tpu as pltpu

EPS = 1e-6
LOG2E = math.log2(math.e)
LANES = 128
FOX_HEADS = 8
FOX_HEAD_DIM = 64
FOX_WIDTH = FOX_HEADS * FOX_HEAD_DIM
HGRN_HEADS = 4
HGRN_DIM = 128
HGRN_WIDTH = HGRN_HEADS * HGRN_DIM
X_HEADS = 4
HGRN_STEP = 16
HGRN_CHUNK = 64
HGRN_MIN_CHUNK_LOG_DECAY = -60.0
VMEM_LIMIT = 56 * 1024 * 1024

PAIR_LANES = 2 * LANES
AUG_F = 0
AUG_ONE = 3
AUG_SHIFT = 6
FCAT_ONE_LANE = 24
FCAT_SHIFT_LANE = 32
NORM_HEADROOM = 1.0 + 2.0 ** -7
FOX_SHIFT_MARGIN = 60.0
FOX_SPREAD_MAX = 120.0
STAT_BLOCK = 256
FOX_SKIP_LOG2 = 100.0

BF16 = jnp.bfloat16
F32 = jnp.float32


def _dot(a, b):
    return jnp.dot(a, b, preferred_element_type=F32)


def _dot_nt(a, b):
    return lax.dot_general(a, b, (((1,), (1,)), ((), ())), preferred_element_type=F32)


def _dot_tn(a, b):
    return lax.dot_general(a, b, (((0,), (0,)), ((), ())), preferred_element_type=F32)


def _split3(v):
    hi = v.astype(BF16)
    r1 = v - hi.astype(F32)
    mid = r1.astype(BF16)
    lo = (r1 - mid.astype(F32)).astype(BF16)
    return hi, mid, lo


def _tri_cumsum(tri, v):
    hi, mid, lo = _split3(v)
    return _dot(tri, hi) + _dot(tri, mid) + _dot(tri, lo)


def _rms(x, g):
    ms = jnp.mean(x * x, axis=-1, keepdims=True)
    return x * lax.rsqrt(ms + EPS) * g


def _resident(shape):
    return pl.BlockSpec(shape, lambda *_: (0,) * len(shape), pipeline_mode=pl.Buffered(1))


def _inproj_kernel(x_ref, g_ref, wt_ref, fb_ref, lbl_ref, sel_ref, tri_ref, hsum_ref,
                   q_ref, k_ref, vt_ref, stat_ref, hq_ref, hk_ref, hg_ref, hi_ref, hgate_ref,
                   carry_ref, kmax_ref, *, layer):
    @pl.when(pl.program_id(1) == 0)
    def _():
        carry_ref[...] = jnp.zeros_like(carry_ref)
        kmax_ref[...] = jnp.zeros_like(kmax_ref)

    hb = _rms(x_ref[0], g_ref[...]).astype(BF16)
    tm = hb.shape[0]
    lane = lax.broadcasted_iota(jnp.int32, (tm, LANES), 1)
    fw, w = FOX_WIDTH, HGRN_WIDTH
    proj = lambda r0, n: _dot_nt(hb, wt_ref[r0:r0 + n, :].astype(BF16))

    z = proj(3 * fw, LANES) + fb_ref[...]
    logf = jnp.minimum(z, 0.0) - jnp.log(1.0 + jnp.exp(-jnp.abs(z)))
    logf = jnp.where(lane < FOX_HEADS, logf, 0.0)
    fcum = _tri_cumsum(tri_ref[...], logf) + carry_ref[0:1, :]
    carry_ref[...] = jnp.broadcast_to(fcum[tm - 1:tm, :], carry_ref.shape)

    fsc = fcum * LOG2E
    q_scale = LOG2E / math.sqrt(FOX_HEAD_DIM)
    qk = jnp.concatenate([(proj(0, fw) * q_scale).astype(BF16), proj(fw, fw).astype(BF16)],
                         axis=1)

    lbl = lbl_ref[...]
    e = jnp.exp(lbl - jnp.max(lbl, axis=0, keepdims=True))
    lb = jnp.sum(e[0:layer + 1, :], axis=0, keepdims=True) / jnp.sum(e, axis=0, keepdims=True)
    seg = lambda n: proj(3 * fw + FOX_HEADS + n * w, w)
    gq = seg(0)
    hq_ref[0] = gq * jax.nn.sigmoid(gq)

    qkf = qk.astype(F32)
    norm2 = _dot((qkf * qkf).astype(BF16), hsum_ref[...]) * NORM_HEADROOM
    norm = jnp.sqrt(norm2)
    to_q_lanes = lambda v: pltpu.roll(v, LANES - FOX_HEADS, 1)
    k_run = jnp.maximum(kmax_ref[...], jnp.max(norm, axis=0, keepdims=True))
    kmax_ref[...] = k_run
    k_run_q = to_q_lanes(k_run)[0:1, :]
    shift = jnp.where(lane < FOX_HEADS, norm * k_run_q - FOX_SHIFT_MARGIN, 0.0)
    spread = norm * (k_run_q + to_q_lanes(norm))

    f = lb + (1.0 - lb) * jax.nn.sigmoid(seg(1))
    hk_ref[0] = 1.0 - f
    hg_ref[0] = jnp.log(f)

    def parts(v):
        hi = v.astype(BF16).astype(F32)
        mid = (v - hi).astype(BF16).astype(F32)
        return hi, mid, (v - hi - mid).astype(BF16).astype(F32)

    f_parts, s_parts = parts(fsc), parts(-shift)
    fcat = jnp.where(lane == FCAT_ONE_LANE, 1.0, 0.0)
    for j in range(3):
        fcat = fcat + (pltpu.roll(f_parts[j], j * FOX_HEADS, 1) if j else f_parts[j])
        fcat = fcat + pltpu.roll(s_parts[j], FCAT_SHIFT_LANE + j * FOX_HEADS, 1)
    aug = _dot(fcat.astype(BF16), sel_ref[...]).astype(BF16)
    pairs = FOX_HEADS // 2
    for p in range(pairs):
        q_ref[0, p, :, 0:LANES] = qk[:, p * LANES:(p + 1) * LANES]
        q_ref[0, p, :, LANES:PAIR_LANES] = aug[:, p * LANES:(p + 1) * LANES]
        k_ref[0, p, :, 0:LANES] = qk[:, (pairs + p) * LANES:(pairs + p + 1) * LANES]
        k_ref[0, p, :, LANES:PAIR_LANES] = aug[:, (pairs + p) * LANES:(pairs + p + 1) * LANES]

    vt_ref[0] = _dot_nt(wt_ref[2 * fw:3 * fw, :].astype(BF16), hb).astype(BF16)

    for sb in range(tm // STAT_BLOCK):
        rows = slice(sb * STAT_BLOCK, (sb + 1) * STAT_BLOCK)
        stat_ref[0, sb, 0:1, :] = jnp.max(norm2[rows, :], axis=0, keepdims=True)
        stat_ref[0, sb, 1:2, :] = fsc[rows, :][0:1, :]
        stat_ref[0, sb, 2:3, :] = fsc[rows, :][STAT_BLOCK - 1:STAT_BLOCK, :]
        stat_ref[0, sb, 3:4, :] = jnp.max(spread[rows, :], axis=0, keepdims=True)
        stat_ref[0, sb, 4:8, :] = jnp.zeros((4, LANES), F32)

    hi_ref[0] = seg(2).astype(BF16)
    gg = seg(3)
    hgate_ref[0] = (gg * jax.nn.sigmoid(gg)).astype(BF16)


def _in_proj(x, norm_g, w_in, fox_f_bias, lb_logits, layer, tm):
    b, s, d = x.shape
    fw, hw = FOX_WIDTH, HGRN_WIDTH
    assert w_in.shape == (d, 3 * fw + FOX_HEADS + 4 * hw)
    wt = w_in.T
    fb = jnp.pad(fox_f_bias.reshape(1, FOX_HEADS), ((0, 0), (0, LANES - FOX_HEADS)))

    sel = np.zeros((LANES, 2 * fw), np.float32)
    for h in range(FOX_HEADS):
        col = h * FOX_HEAD_DIM
        for j in range(3):
            sel[j * FOX_HEADS + h, col + AUG_F + j] = 1.0
            sel[FCAT_ONE_LANE, col + AUG_ONE + j] = 1.0
            sel[FCAT_ONE_LANE, fw + col + AUG_F + j] = 1.0
            sel[j * FOX_HEADS + h, fw + col + AUG_ONE + j] = -1.0
            sel[FCAT_SHIFT_LANE + j * FOX_HEADS + h, col + AUG_SHIFT + j] = 1.0
            sel[FCAT_ONE_LANE, fw + col + AUG_SHIFT + j] = 1.0
    sel = jnp.asarray(sel, BF16)
    tri = jnp.asarray(np.tril(np.ones((tm, tm), np.float32)), BF16)
    hsum = np.zeros((2 * fw, LANES), np.float32)
    hsum[np.arange(2 * fw), np.arange(2 * fw) // FOX_HEAD_DIM] = 1.0
    hsum = jnp.asarray(hsum, BF16)

    const = _resident
    pairs = FOX_HEADS // 2
    head_out = jax.ShapeDtypeStruct((b, pairs, s, PAIR_LANES), BF16)
    head_spec = pl.BlockSpec((1, pairs, tm, PAIR_LANES), lambda bi, ti: (bi, 0, ti, 0))
    tok_out = lambda dt: jax.ShapeDtypeStruct((b, s, hw), dt)
    tok_spec = pl.BlockSpec((1, tm, hw), lambda bi, ti: (bi, ti, 0))
    return pl.pallas_call(
        functools.partial(_inproj_kernel, layer=layer),
        grid=(b, s // tm),
        in_specs=[pl.BlockSpec((1, tm, d), lambda bi, ti: (bi, ti, 0)),
                  const((1, d)), const(wt.shape), const(fb.shape), const(lb_logits.shape), const(sel.shape), const(tri.shape), const(hsum.shape)],
        out_specs=[head_spec, head_spec, pl.BlockSpec((1, fw, tm), lambda bi, ti: (bi, 0, ti)),
                   pl.BlockSpec((1, tm // STAT_BLOCK, 8, LANES), lambda bi, ti: (bi, ti, 0, 0)),
                   tok_spec, tok_spec, tok_spec, tok_spec, tok_spec],
        out_shape=[head_out, head_out, jax.ShapeDtypeStruct((b, fw, s), BF16),
                   jax.ShapeDtypeStruct((b, s // STAT_BLOCK, 8, LANES), F32),
                   tok_out(F32), tok_out(F32), tok_out(F32), tok_out(BF16), tok_out(BF16)],
        scratch_shapes=[pltpu.VMEM((8, LANES), F32), pltpu.VMEM((8, LANES), F32)],
        compiler_params=pltpu.CompilerParams(dimension_semantics=("arbitrary", "arbitrary"),
                                             vmem_limit_bytes=VMEM_LIMIT),
        name="in_proj",
    )(x, norm_g.reshape(1, d), wt, fb, lb_logits, sel, tri, hsum)


def _fox_first_block(stats, tq):
    b, nsb = stats.shape[0], stats.shape[1]
    r = tq // STAT_BLOCK
    st = stats.reshape(b, nsb // r, r, 8, LANES)
    h = FOX_HEADS
    qn = jnp.sqrt(jnp.max(st[:, :, :, 0, 0:h], axis=2))
    kn = jnp.sqrt(jnp.max(st[:, :, :, 0, h:2 * h], axis=2))
    f_first, f_last = st[:, :, 0, 1, 0:h], st[:, :, r - 1, 2, 0:h]
    bound = (qn[:, :, None, :] * (kn[:, None, :, :] + kn[:, :, None, :])
             + f_first[:, :, None, :] - f_last[:, None, :, :])
    blk = jnp.arange(nsb // r)
    need = jnp.any(jnp.logical_not(bound <= -FOX_SKIP_LOG2), axis=-1) & (blk[None, None, :] < blk[None, :, None])
    first = jnp.min(jnp.where(need, blk[None, None, :], nsb // r), axis=-1)
    return jnp.minimum(first, blk[None, :]).astype(jnp.int32)


def _fox_direct_ok(stats, tq):
    b, nsb = stats.shape[0], stats.shape[1]
    spread = jnp.max(stats[:, :, 3, 0:FOX_HEADS].reshape(b, nsb * STAT_BLOCK // tq, -1), axis=-1)
    return (spread <= FOX_SPREAD_MAX).astype(jnp.int32)


def _fox_kernel(first_ref, direct_ref, q_ref, k_ref, vt_ref, o_ref, qm_sc, m_sc, l_sc, acc_sc, *, tq, hg):
    i = pl.program_id(2)
    hd = FOX_HEAD_DIM
    l_sc[...] = jnp.zeros_like(l_sc)
    acc_sc[...] = jnp.zeros_like(acc_sc)
    lane = lax.broadcasted_iota(jnp.int32, (tq, PAIR_LANES), 1)
    for hh in range(hg):
        own = (lane // hd) % 2 == hh % 2
        qm_sc[hh] = jnp.where(own, q_ref[0, hh // 2], jnp.zeros((), BF16))

    def accumulate_online(hh, st, off):
        m_prev = m_sc[hh]
        m_new = jnp.maximum(m_prev, jnp.max(st, axis=0, keepdims=True))
        alpha = jnp.exp2(m_prev - m_new)
        p = jnp.exp2(st - m_new)
        l_sc[hh] = alpha * l_sc[hh] + jnp.sum(p, axis=0, keepdims=True)
        vt = vt_ref[0, hh * hd:(hh + 1) * hd, pl.ds(off, tq)]
        acc_sc[hh] = alpha * acc_sc[hh] + _dot(vt, p.astype(BF16))
        m_sc[hh] = m_new

    def accumulate_direct(hh, st, off):
        p = jnp.exp2(st)
        l_sc[hh] += jnp.sum(p, axis=0, keepdims=True)
        acc_sc[hh] += _dot(vt_ref[0, hh * hd:(hh + 1) * hd, pl.ds(off, tq)], p.astype(BF16))

    def sweep(accumulate):
        def block(j, masked):
            off = pl.multiple_of(j * tq, tq)
            scores = lambda h_: _dot_nt(k_ref[0, h_ // 2, pl.ds(off, tq), :], qm_sc[h_])
            ahead = 2
            pending = [scores(h_) for h_ in range(min(ahead, hg))]
            for hh in range(hg):
                st = pending.pop(0)
                if hh + ahead < hg:
                    pending.append(scores(hh + ahead))
                if masked:
                    key = lax.broadcasted_iota(jnp.int32, st.shape, 0)
                    qry = lax.broadcasted_iota(jnp.int32, st.shape, 1)
                    st = jnp.where(key <= qry, st, -jnp.inf)
                accumulate(hh, st, off)

        def body(j, carry):
            block(j, False)
            return carry

        lax.fori_loop(first_ref[pl.program_id(0), i], i, body, 0)
        block(i, True)

    direct = direct_ref[pl.program_id(0), i] != 0

    @pl.when(direct)
    def _():
        sweep(accumulate_direct)

    @pl.when(jnp.logical_not(direct))
    def _():
        m_sc[...] = jnp.full_like(m_sc, -jnp.inf)
        sweep(accumulate_online)

    for hh in range(hg):
        o_ref[0, hh * hd:(hh + 1) * hd, :] = (acc_sc[hh] / l_sc[hh]).astype(o_ref.dtype)


def _fox_attention(q, k, vt, first_block, direct_ok, tq, hg):
    b, pairs, s, _ = q.shape
    assert hg % 2 == 0
    groups = 2 * pairs // hg
    rows = hg * FOX_HEAD_DIM
    grid_spec = pltpu.PrefetchScalarGridSpec(
        num_scalar_prefetch=2,
        grid=(b, groups, s // tq),
        in_specs=[pl.BlockSpec((1, hg // 2, tq, PAIR_LANES), lambda bi, pi, qi, *_: (bi, pi, qi, 0)),
                  pl.BlockSpec((1, hg // 2, s, PAIR_LANES), lambda bi, pi, qi, *_: (bi, pi, 0, 0)),
                  pl.BlockSpec((1, rows, s), lambda bi, pi, qi, *_: (bi, pi, 0))],
        out_specs=pl.BlockSpec((1, rows, tq), lambda bi, pi, qi, *_: (bi, pi, qi)),
        scratch_shapes=[pltpu.VMEM((hg, tq, PAIR_LANES), BF16),
                        pltpu.VMEM((hg, 1, tq), F32), pltpu.VMEM((hg, 1, tq), F32),
                        pltpu.VMEM((hg, FOX_HEAD_DIM, tq), F32)])
    return pl.pallas_call(
        functools.partial(_fox_kernel, tq=tq, hg=hg),
        grid_spec=grid_spec,
        out_shape=jax.ShapeDtypeStruct((b, groups * rows, s), BF16),
        compiler_params=pltpu.CompilerParams(dimension_semantics=("arbitrary",) * 3,
                                             vmem_limit_bytes=VMEM_LIMIT),
        name="fox_attention",
    )(first_block, direct_ok, q, k, vt)


def _hgrn_chunk_path(q_ref, k_ref, i_ref, st_sc, b_sc, o_sc, tc):
    c_len = HGRN_CHUNK
    n_chunks = tc // c_len
    t_idx = lax.broadcasted_iota(jnp.int32, (c_len, c_len), 0)
    s_idx = lax.broadcasted_iota(jnp.int32, (c_len, c_len), 1)
    causal = s_idx <= t_idx
    units = [(c, h) for c in range(n_chunks) for h in range(HGRN_HEADS)]
    rows = lambda c: slice(c * c_len, (c + 1) * c_len)
    lanes = lambda h: slice(h * HGRN_DIM, (h + 1) * HGRN_DIM)

    qe, ke, kl, decay = {}, {}, {}, {}
    for c, h in units:
        b = b_sc[rows(c), lanes(h)]
        kk = k_ref[0, rows(c), lanes(h)]
        b_last = b[c_len - 1:c_len, :]
        qe[c, h] = (q_ref[0, rows(c), lanes(h)] * jnp.exp(b)).astype(BF16)
        ke[c, h] = (kk * jnp.exp(-b)).astype(BF16)
        kl[c, h] = (kk * jnp.exp(b_last - b)).astype(BF16)
        decay[c, h] = jnp.exp(b_last)
    attn = {u: jnp.where(causal, _dot_nt(qe[u], ke[u]), 0.0).astype(BF16) for u in units}
    intra = {(c, h): _dot(attn[c, h], i_ref[0, rows(c), lanes(h)]) for c, h in units}
    d_state = {(c, h): _dot_tn(i_ref[0, rows(c), lanes(h)], kl[c, h]) for c, h in units}
    state = {}
    for h in range(HGRN_HEADS):
        st = st_sc[h]
        for c in range(n_chunks):
            state[c, h] = st.astype(BF16)
            st = st * decay[c, h] + d_state[c, h]
        st_sc[h] = st
    for c, h in units:
        o_sc[rows(c), lanes(h)] = intra[c, h] + _dot_nt(qe[c, h], state[c, h])


def _hgrn_step_path(q_ref, k_ref, i_ref, st_sc, b_sc, o_sc, tc):
    row8 = lax.broadcasted_iota(jnp.int32, (8, HGRN_DIM), 0)
    half = HGRN_STEP // 2

    def step(u, carry):
        base = pl.multiple_of(u * HGRN_STEP, HGRN_STEP)
        for h in range(HGRN_HEADS):
            ls = slice(h * HGRN_DIM, (h + 1) * HGRN_DIM)
            rows = pl.ds(base, HGRN_STEP)
            bq = b_sc[rows, ls]
            qq = q_ref[0, rows, ls]
            kk = k_ref[0, rows, ls]
            ii = i_ref[0, rows, ls].astype(F32)
            b_last = bq[HGRN_STEP - 1:HGRN_STEP, :]
            st = st_sc[h]
            inter = _dot_nt((qq * jnp.exp(bq)).astype(BF16), st.astype(BF16))
            k_dec = (kk * jnp.exp(b_last - bq)).astype(BF16)
            st_sc[h] = st * jnp.exp(b_last) + _dot_tn(ii.astype(BF16), k_dec)
            out = [inter[0:half], inter[half:HGRN_STEP]]
            qv = [qq[0:half], qq[half:HGRN_STEP]]
            bv = [bq[0:half], bq[half:HGRN_STEP]]
            for s_ in range(HGRN_STEP):
                k_s, b_s, i_s = kk[s_:s_ + 1, :], bq[s_:s_ + 1, :], ii[s_:s_ + 1, :]
                for v_ in range(s_ // half, 2):
                    diff = bv[v_] - b_s
                    if s_ > v_ * half:
                        diff = jnp.where(row8 + v_ * half >= s_, diff, -jnp.inf)
                    a = jnp.sum(qv[v_] * k_s * jnp.exp(diff), axis=-1, keepdims=True)
                    out[v_] = out[v_] + a * i_s
            o_sc[pl.ds(base, half), ls] = out[0]
            o_sc[pl.ds(base + half, half), ls] = out[1]
        return carry

    lax.fori_loop(0, tc // HGRN_STEP, step, 0)


def _hgrn_kernel(q_ref, k_ref, g_ref, i_ref, gate_ref, ng_ref, tri64_ref, tri16_ref, o_ref,
                 st_sc, b_sc, o_sc, *, tc):
    @pl.when(pl.program_id(1) == 0)
    def _():
        st_sc[...] = jnp.zeros_like(st_sc)

    b64 = _tri_cumsum(tri64_ref[...], g_ref[0])
    b_sc[...] = b64
    chunk_ok = jnp.min(b64) >= HGRN_MIN_CHUNK_LOG_DECAY

    @pl.when(chunk_ok)
    def _():
        _hgrn_chunk_path(q_ref, k_ref, i_ref, st_sc, b_sc, o_sc, tc)

    @pl.when(jnp.logical_not(chunk_ok))
    def _():
        b_sc[...] = _tri_cumsum(tri16_ref[...], g_ref[0])
        _hgrn_step_path(q_ref, k_ref, i_ref, st_sc, b_sc, o_sc, tc)

    for h in range(HGRN_HEADS):
        ls = slice(h * HGRN_DIM, (h + 1) * HGRN_DIM)
        o_ref[0, :, ls] = (_rms(o_sc[:, ls], ng_ref[...]) * gate_ref[0, :, ls]).astype(o_ref.dtype)


def _block_tri(n, blk):
    return jnp.asarray(np.kron(np.eye(n // blk, dtype=np.float32), np.tril(np.ones((blk, blk), np.float32))), BF16)


def _hgrn(hq, hk, hg, hi, hgate, norm_g, tc):
    b, s, w = hq.shape
    tok_spec = pl.BlockSpec((1, tc, w), lambda bi, ci: (bi, ci, 0))
    return pl.pallas_call(
        functools.partial(_hgrn_kernel, tc=tc),
        grid=(b, s // tc),
        in_specs=[tok_spec] * 5 + [_resident((1, HGRN_DIM)), _resident((tc, tc)), _resident((tc, tc))],
        out_specs=tok_spec,
        out_shape=jax.ShapeDtypeStruct((b, s, w), BF16),
        scratch_shapes=[pltpu.VMEM((HGRN_HEADS, HGRN_DIM, HGRN_DIM), F32),
                        pltpu.VMEM((tc, w), F32), pltpu.VMEM((tc, w), F32)],
        compiler_params=pltpu.CompilerParams(dimension_semantics=("arbitrary", "arbitrary"),
                                             vmem_limit_bytes=VMEM_LIMIT),
        name="hgrn2",
    )(hq, hk, hg, hi, hgate, norm_g.reshape(1, HGRN_DIM), _block_tri(tc, HGRN_CHUNK), _block_tri(tc, HGRN_STEP))


def _memkv_kernel(mem_ref, g_ref, w_ref, k_ref, v_ref):
    d = mem_ref.shape[-1]
    kv = _dot(_rms(mem_ref[0], g_ref[...]).astype(BF16), w_ref[...].astype(BF16))
    k_ref[0] = kv[:, :d].astype(BF16)
    v_ref[0] = kv[:, d:].astype(BF16)


def _mem_kv(mem, norm_g, w_kv):
    b, m, d = mem.shape
    out = jax.ShapeDtypeStruct((b, m, d), BF16)
    spec = pl.BlockSpec((1, m, d), lambda bi: (bi, 0, 0))
    return pl.pallas_call(
        _memkv_kernel,
        grid=(b,),
        in_specs=[spec, pl.BlockSpec((1, d), lambda bi: (0, 0)), pl.BlockSpec((d, 2 * d), lambda bi: (0, 0))],
        out_specs=[spec, spec],
        out_shape=[out, out],
        compiler_params=pltpu.CompilerParams(dimension_semantics=("arbitrary",), vmem_limit_bytes=VMEM_LIMIT),
        name="mem_kv",
    )(mem, norm_g.reshape(1, d), w_kv)


def _mix_kernel(x_ref, foxt_ref, rec_ref, wo_ref, gx_ref, wq_ref, mk_ref, mv_ref, wxo_ref, o_ref,
                h1_sc, q_sc, att_sc, *, sub):
    d = x_ref.shape[-1]
    fw = foxt_ref.shape[1]
    hd = d // X_HEADS
    inv = 1.0 / math.sqrt(hd)
    n_sub = x_ref.shape[1] // sub

    def out_proj(i):
        r = slice(i * sub, (i + 1) * sub)
        h1_sc[r, :] = (x_ref[0, r, :] + _dot_tn(foxt_ref[0, :, r], wo_ref[0:fw, :].astype(BF16))
                       + _dot(rec_ref[0, r, :], wo_ref[fw:, :].astype(BF16)))

    def query(i):
        r = slice(i * sub, (i + 1) * sub)
        q_sc[r, :] = _dot(_rms(h1_sc[r, :], gx_ref[...]).astype(BF16), wq_ref[...].astype(BF16)).astype(BF16)

    def attend(i):
        r = slice(i * sub, (i + 1) * sub)
        for h in range(X_HEADS):
            cs = slice(h * hd, (h + 1) * hd)
            s = _dot_nt(q_sc[r, cs], mk_ref[0, :, cs]) * inv
            p = jnp.exp(s - jnp.max(s, axis=-1, keepdims=True))
            p = p / jnp.sum(p, axis=-1, keepdims=True)
            att_sc[r, cs] = _dot(p.astype(BF16), mv_ref[0, :, cs]).astype(BF16)

    def finish(i):
        r = slice(i * sub, (i + 1) * sub)
        o_ref[0, r, :] = h1_sc[r, :] + _dot(att_sc[r, :], wxo_ref[...].astype(BF16))

    stages = (out_proj, query, attend, finish)
    for t in range(n_sub + len(stages) - 1):
        for k, stage in enumerate(stages):
            if 0 <= t - k < n_sub:
                stage(t - k)


def _mix(x, foxt, rec, w_out, norm_g, w_xq, mem_k, mem_v, w_xo, tm):
    b, s, d = x.shape
    m = mem_k.shape[1]
    fw = foxt.shape[1]
    tok = lambda wd: pl.BlockSpec((1, tm, wd), lambda bi, ti: (bi, ti, 0))
    const = _resident
    mem_spec = pl.BlockSpec((1, m, d), lambda bi, ti: (bi, 0, 0))
    return pl.pallas_call(
        functools.partial(_mix_kernel, sub=min(tm, 256)),
        grid=(b, s // tm),
        in_specs=[tok(d), pl.BlockSpec((1, fw, tm), lambda bi, ti: (bi, 0, ti)), tok(rec.shape[-1]),
                  const((d, d)), const((1, d)), const((d, d)), mem_spec, mem_spec, const((d, d))],
        out_specs=tok(d),
        out_shape=jax.ShapeDtypeStruct((b, s, d), F32),
        scratch_shapes=[pltpu.VMEM((tm, d), F32), pltpu.VMEM((tm, d), BF16), pltpu.VMEM((tm, d), BF16)],
        compiler_params=pltpu.CompilerParams(dimension_semantics=("arbitrary", "arbitrary"),
                                             vmem_limit_bytes=VMEM_LIMIT),
        name="mix",
    )(x, foxt, rec, w_out, norm_g.reshape(1, d), w_xq, mem_k, mem_v, w_xo)


def _mlp_kernel(h_ref, g_ref, w1_ref, w2_ref, gf_ref, o_ref, acc_sc, *, fc):
    hn = _rms(h_ref[...], g_ref[...]).astype(BF16)
    n_chunks = w1_ref.shape[1] // fc
    up = lambda c: _dot(hn, w1_ref[:, c * fc:(c + 1) * fc].astype(BF16))
    u_next = up(0)
    for c in range(n_chunks):
        u = jnp.maximum(u_next, 0.0)
        if c + 1 < n_chunks:
            u_next = up(c + 1)
        part = _dot((u * u).astype(BF16), w2_ref[c * fc:(c + 1) * fc, :].astype(BF16))
        if c == 0:
            acc_sc[...] = part
        else:
            acc_sc[...] += part
    o_ref[...] = _rms(h_ref[...] + acc_sc[...], gf_ref[...])


def _mlp(h, norm_g, w1, w2, final_g, tm, fc):
    t, d = h.shape
    dff = w1.shape[1]
    return pl.pallas_call(
        functools.partial(_mlp_kernel, fc=fc),
        grid=(t // tm,),
        in_specs=[pl.BlockSpec((tm, d), lambda ti: (ti, 0)), _resident((1, d)),
                  _resident((d, dff)), _resident((dff, d)), _resident((1, d))],
        out_specs=pl.BlockSpec((tm, d), lambda ti: (ti, 0)),
        out_shape=jax.ShapeDtypeStruct((t, d), F32),
        scratch_shapes=[pltpu.VMEM((tm, d), F32)],
        compiler_params=pltpu.CompilerParams(dimension_semantics=("arbitrary",), vmem_limit_bytes=VMEM_LIMIT),
        name="mlp",
    )(h, norm_g.reshape(1, d), w1, w2, final_g.reshape(1, d))


def _tile(n, want):
    t = min(n, want)
    assert n % t == 0, (n, want)
    return t


def kernel(x, mem, norm_mix_g, w_in, fox_f_bias, hgrn_lb_logits, hgrn_norm_g, w_out, norm_x_g, norm_mem_g,
           w_xq, w_xkv, w_xo, norm_ff_g, w1, w2, final_norm_g):
    b, s, d = x.shape
    h = x
    for l in range(w_in.shape[0]):
        q, k, v, stats, hq, hk, hg, hi, hgate = _in_proj(h, norm_mix_g[l], w_in[l], fox_f_bias[l], hgrn_lb_logits, l,
                                                  _tile(s, 512))
        tq = _tile(s, 512)
        fox = _fox_attention(q, k, v, _fox_first_block(stats, tq), _fox_direct_ok(stats, tq), tq, 8)
        rec = _hgrn(hq, hk, hg, hi, hgate, hgrn_norm_g[l], _tile(s, 256))
        mem_k, mem_v = _mem_kv(mem, norm_mem_g[l], w_xkv[l])
        h = _mix(h, fox, rec, w_out[l], norm_x_g[l], w_xq[l], mem_k, mem_v, w_xo[l], _tile(s, 1024))
        is_last = l == w_in.shape[0] - 1
        assert is_last, "the MLP kernel fuses the final norm, so it must be the last layer"
        h = _mlp(h.reshape(b * s, d), norm_ff_g[l], w1[l], w2[l], final_norm_g,
                 _tile(b * s, 512), _tile(w1.shape[-1], 1024)).reshape(b, s, d)
    return h
```

```python
import functools
import math

import jax
import jax.numpy as jnp
import numpy as np
from jax import lax
from jax.experimental import pallas as pl
from jax.experimental.pallas import tpu as pltpu

EPS = 1e-6
LOG2E = math.log2(math.e)
LANES = 128
FOX_HEADS = 8
FOX_HEAD_DIM = 64
FOX_WIDTH = FOX_HEADS * FOX_HEAD_DIM
HGRN_HEADS = 4
HGRN_DIM = 128
HGRN_WIDTH = HGRN_HEADS * HGRN_DIM
X_HEADS = 4
HGRN_STEP = 16
HGRN_CHUNK = 64
HGRN_MIN_CHUNK_LOG_DECAY = -60.0
VMEM_LIMIT = 56 * 1024 * 1024

PAIR_LANES = 2 * LANES
AUG_F = 0
AUG_ONE = 3
AUG_SHIFT = 6
FCAT_ONE_LANE = 24
FCAT_SHIFT_LANE = 32
NORM_HEADROOM = 1.0 + 2.0 ** -7
FOX_SHIFT_MARGIN = 60.0
FOX_SPREAD_MAX = 120.0
STAT_BLOCK = 256
FOX_SKIP_LOG2 = 100.0

BF16 = jnp.bfloat16
F32 = jnp.float32


def _dot(a, b):
    return jnp.dot(a, b, preferred_element_type=F32)


def _dot_nt(a, b):
    return lax.dot_general(a, b, (((1,), (1,)), ((), ())), preferred_element_type=F32)


def _dot_tn(a, b):
    return lax.dot_general(a, b, (((0,), (0,)), ((), ())), preferred_element_type=F32)


def _split3(v):
    hi = v.astype(BF16)
    r1 = v - hi.astype(F32)
    mid = r1.astype(BF16)
    lo = (r1 - mid.astype(F32)).astype(BF16)
    return hi, mid, lo


def _tri_cumsum(tri, v):
    hi, mid, lo = _split3(v)
    return _dot(tri, hi) + _dot(tri, mid) + _dot(tri, lo)


def _rms(x, g):
    ms = jnp.mean(x * x, axis=-1, keepdims=True)
    return x * lax.rsqrt(ms + EPS) * g


def _resident(shape):
    return pl.BlockSpec(shape, lambda *_: (0,) * len(shape), pipeline_mode=pl.Buffered(1))


def _front_kernel(x_ref, g_ref, wt_ref, fb_ref, lbl_ref, sel_ref, tri_ref, hsum_ref, ng_ref, tri64_ref, tri16_ref,
                  q_ref, k_ref, vt_ref, stat_ref, rec_ref,
                  carry_ref, kmax_ref, hb_sc, hq_sc, hk_sc, hg_sc, hi_sc, gate_sc, st_sc, b_sc, o_sc, *, layer):
    @pl.when(pl.program_id(1) == 0)
    def _():
        carry_ref[...] = jnp.zeros_like(carry_ref)
        kmax_ref[...] = jnp.zeros_like(kmax_ref)
        st_sc[...] = jnp.zeros_like(st_sc)

    tm = x_ref.shape[1]
    hb_sc[...] = _rms(x_ref[0], g_ref[...]).astype(BF16)
    lane = lax.broadcasted_iota(jnp.int32, (tm, LANES), 1)
    fw, w = FOX_WIDTH, HGRN_WIDTH
    proj = lambda r0, n: _dot_nt(hb_sc[...], wt_ref[r0:r0 + n, :].astype(BF16))

    lbl = lbl_ref[...]
    e = jnp.exp(lbl - jnp.max(lbl, axis=0, keepdims=True))
    lb = jnp.sum(e[0:layer + 1, :], axis=0, keepdims=True) / jnp.sum(e, axis=0, keepdims=True)
    seg = lambda n: proj(3 * fw + FOX_HEADS + n * w, w)
    gq = seg(0)
    hq_sc[0] = gq * jax.nn.sigmoid(gq)
    f = lb + (1.0 - lb) * jax.nn.sigmoid(seg(1))
    hk_sc[0] = 1.0 - f
    hg_sc[...] = jnp.log(f)
    hi_sc[0] = seg(2).astype(BF16)
    gg = seg(3)
    gate_sc[...] = (gg * jax.nn.sigmoid(gg)).astype(BF16)

    cs_rows = tri64_ref.shape[0]
    for r0 in range(0, tm, cs_rows):
        b_sc[r0:r0 + cs_rows, :] = _tri_cumsum(tri64_ref[...], hg_sc[r0:r0 + cs_rows, :])
    chunk_ok = jnp.min(b_sc[...]) >= HGRN_MIN_CHUNK_LOG_DECAY

    def hgrn_output():
        for h in range(HGRN_HEADS):
            ls = slice(h * HGRN_DIM, (h + 1) * HGRN_DIM)
            rec_ref[0, :, ls] = (_rms(o_sc[:, ls], ng_ref[...]) * gate_sc[:, ls]).astype(rec_ref.dtype)

    v = {}

    def fox_forget():
        z = proj(3 * fw, LANES) + fb_ref[...]
        logf = jnp.minimum(z, 0.0) - jnp.log(1.0 + jnp.exp(-jnp.abs(z)))
        logf = jnp.where(lane < FOX_HEADS, logf, 0.0)
        fcum = _tri_cumsum(tri_ref[...], logf) + carry_ref[0:1, :]
        carry_ref[...] = jnp.broadcast_to(fcum[tm - 1:tm, :], carry_ref.shape)
        v["fsc"] = fcum * LOG2E

    def fox_qk():
        q_scale = LOG2E / math.sqrt(FOX_HEAD_DIM)
        v["qk"] = jnp.concatenate([(proj(0, fw) * q_scale).astype(BF16), proj(fw, fw).astype(BF16)],
                                  axis=1)

    def fox_bounds():
        qkf = v["qk"].astype(F32)
        norm2 = _dot((qkf * qkf).astype(BF16), hsum_ref[...]) * NORM_HEADROOM
        norm = jnp.sqrt(norm2)
        to_q_lanes = lambda a: pltpu.roll(a, LANES - FOX_HEADS, 1)
        k_run = jnp.maximum(kmax_ref[...], jnp.max(norm, axis=0, keepdims=True))
        kmax_ref[...] = k_run
        k_run_q = to_q_lanes(k_run)[0:1, :]
        v["shift"] = jnp.where(lane < FOX_HEADS, norm * k_run_q - FOX_SHIFT_MARGIN, 0.0)
        v["spread"] = norm * (k_run_q + to_q_lanes(norm))
        v["norm2"] = norm2

    def fox_operands():
        parts = lambda a: [p.astype(F32) for p in _split3(a)]
        f_parts, s_parts = parts(v["fsc"]), parts(-v["shift"])
        fcat = jnp.where(lane == FCAT_ONE_LANE, 1.0, 0.0)
        for j in range(3):
            fcat = fcat + (pltpu.roll(f_parts[j], j * FOX_HEADS, 1) if j else f_parts[j])
            fcat = fcat + pltpu.roll(s_parts[j], FCAT_SHIFT_LANE + j * FOX_HEADS, 1)
        aug = _dot(fcat.astype(BF16), sel_ref[...]).astype(BF16)
        qk, pairs = v["qk"], FOX_HEADS // 2
        for p in range(pairs):
            q_ref[0, p, :, 0:LANES] = qk[:, p * LANES:(p + 1) * LANES]
            q_ref[0, p, :, LANES:PAIR_LANES] = aug[:, p * LANES:(p + 1) * LANES]
            k_ref[0, p, :, 0:LANES] = qk[:, (pairs + p) * LANES:(pairs + p + 1) * LANES]
            k_ref[0, p, :, LANES:PAIR_LANES] = aug[:, (pairs + p) * LANES:(pairs + p + 1) * LANES]

    def fox_values():
        vt_ref[0] = _dot_nt(wt_ref[2 * fw:3 * fw, :].astype(BF16), hb_sc[...]).astype(BF16)

    def fox_stats():
        for sb in range(tm // STAT_BLOCK):
            rows = slice(sb * STAT_BLOCK, (sb + 1) * STAT_BLOCK)
            stat_ref[0, sb, 0:1, :] = jnp.max(v["norm2"][rows, :], axis=0, keepdims=True)
            stat_ref[0, sb, 1:2, :] = v["fsc"][rows, :][0:1, :]
            stat_ref[0, sb, 2:3, :] = v["fsc"][rows, :][STAT_BLOCK - 1:STAT_BLOCK, :]
            stat_ref[0, sb, 3:4, :] = jnp.max(v["spread"][rows, :], axis=0, keepdims=True)
            stat_ref[0, sb, 4:8, :] = jnp.zeros((4, LANES), F32)

    fox_stages = [fox_forget, fox_qk, fox_bounds, fox_operands, fox_values, fox_stats]

    @pl.when(chunk_ok)
    def _():
        hgrn_stages = _hgrn_chunk_stages(hq_sc, hk_sc, hi_sc, st_sc, b_sc, o_sc, tm) + [hgrn_output]
        for a, b in zip(hgrn_stages, fox_stages):
            a()
            b()

    @pl.when(jnp.logical_not(chunk_ok))
    def _():
        for r0 in range(0, tm, cs_rows):
            b_sc[r0:r0 + cs_rows, :] = _tri_cumsum(tri16_ref[...], hg_sc[r0:r0 + cs_rows, :])
        _hgrn_step_path(hq_sc, hk_sc, hi_sc, st_sc, b_sc, o_sc, tm)
        hgrn_output()
        for stage in fox_stages:
            stage()


def _front(x, norm_g, w_in, fox_f_bias, lb_logits, hgrn_norm_g, layer, tm):
    b, s, d = x.shape
    fw, hw = FOX_WIDTH, HGRN_WIDTH
    assert w_in.shape == (d, 3 * fw + FOX_HEADS + 4 * hw)
    wt = w_in.T
    fb = jnp.pad(fox_f_bias.reshape(1, FOX_HEADS), ((0, 0), (0, LANES - FOX_HEADS)))

    sel = np.zeros((LANES, 2 * fw), np.float32)
    for h in range(FOX_HEADS):
        col = h * FOX_HEAD_DIM
        for j in range(3):
            sel[j * FOX_HEADS + h, col + AUG_F + j] = 1.0
            sel[FCAT_ONE_LANE, col + AUG_ONE + j] = 1.0
            sel[FCAT_ONE_LANE, fw + col + AUG_F + j] = 1.0
            sel[j * FOX_HEADS + h, fw + col + AUG_ONE + j] = -1.0
            sel[FCAT_SHIFT_LANE + j * FOX_HEADS + h, col + AUG_SHIFT + j] = 1.0
            sel[FCAT_ONE_LANE, fw + col + AUG_SHIFT + j] = 1.0
    sel = jnp.asarray(sel, BF16)
    tri = jnp.asarray(np.tril(np.ones((tm, tm), np.float32)), BF16)
    hsum = np.zeros((2 * fw, LANES), np.float32)
    hsum[np.arange(2 * fw), np.arange(2 * fw) // FOX_HEAD_DIM] = 1.0
    hsum = jnp.asarray(hsum, BF16)

    cs_rows = min(tm, 256)
    tri64, tri16 = _block_tri(cs_rows, HGRN_CHUNK), _block_tri(cs_rows, HGRN_STEP)

    const = _resident
    pairs = FOX_HEADS // 2
    head_out = jax.ShapeDtypeStruct((b, pairs, s, PAIR_LANES), BF16)
    head_spec = pl.BlockSpec((1, pairs, tm, PAIR_LANES), lambda bi, ti: (bi, 0, ti, 0))
    vm = pltpu.VMEM
    return pl.pallas_call(
        functools.partial(_front_kernel, layer=layer),
        grid=(b, s // tm),
        in_specs=[pl.BlockSpec((1, tm, d), lambda bi, ti: (bi, ti, 0)),
                  const((1, d)), const(wt.shape), const(fb.shape), const(lb_logits.shape), const(sel.shape),
                  const(tri.shape), const(hsum.shape), const((1, HGRN_DIM)), const(tri64.shape), const(tri16.shape)],
        out_specs=[head_spec, head_spec, pl.BlockSpec((1, fw, tm), lambda bi, ti: (bi, 0, ti)),
                   pl.BlockSpec((1, tm // STAT_BLOCK, 8, LANES), lambda bi, ti: (bi, ti, 0, 0)),
                   pl.BlockSpec((1, tm, hw), lambda bi, ti: (bi, ti, 0))],
        out_shape=[head_out, head_out, jax.ShapeDtypeStruct((b, fw, s), BF16),
                   jax.ShapeDtypeStruct((b, s // STAT_BLOCK, 8, LANES), F32),
                   jax.ShapeDtypeStruct((b, s, hw), BF16)],
        scratch_shapes=[vm((8, LANES), F32), vm((8, LANES), F32), vm((tm, d), BF16),
                        vm((1, tm, hw), F32), vm((1, tm, hw), F32), vm((tm, hw), F32),
                        vm((1, tm, hw), BF16), vm((tm, hw), BF16),
                        vm((HGRN_HEADS, HGRN_DIM, HGRN_DIM), F32), vm((tm, hw), F32), vm((tm, hw), F32)],
        compiler_params=pltpu.CompilerParams(dimension_semantics=("arbitrary", "arbitrary"),
                                             vmem_limit_bytes=VMEM_LIMIT),
        name="front",
    )(x, norm_g.reshape(1, d), wt, fb, lb_logits, sel, tri, hsum, hgrn_norm_g.reshape(1, HGRN_DIM), tri64, tri16)


def _fox_first_block(stats, tq):
    b, nsb = stats.shape[0], stats.shape[1]
    r = tq // STAT_BLOCK
    st = stats.reshape(b, nsb // r, r, 8, LANES)
    h = FOX_HEADS
    qn = jnp.sqrt(jnp.max(st[:, :, :, 0, 0:h], axis=2))
    kn = jnp.sqrt(jnp.max(st[:, :, :, 0, h:2 * h], axis=2))
    f_first, f_last = st[:, :, 0, 1, 0:h], st[:, :, r - 1, 2, 0:h]
    bound = (qn[:, :, None, :] * (kn[:, None, :, :] + kn[:, :, None, :])
             + f_first[:, :, None, :] - f_last[:, None, :, :])
    blk = jnp.arange(nsb // r)
    need = jnp.any(jnp.logical_not(bound <= -FOX_SKIP_LOG2), axis=-1) & (blk[None, None, :] < blk[None, :, None])
    first = jnp.min(jnp.where(need, blk[None, None, :], nsb // r), axis=-1)
    return jnp.minimum(first, blk[None, :]).astype(jnp.int32)


def _fox_direct_ok(stats, tq):
    b, nsb = stats.shape[0], stats.shape[1]
    spread = jnp.max(stats[:, :, 3, 0:FOX_HEADS].reshape(b, nsb * STAT_BLOCK // tq, -1), axis=-1)
    return (spread <= FOX_SPREAD_MAX).astype(jnp.int32)


def _fox_kernel(first_ref, direct_ref, q_ref, k_ref, vt_ref, o_ref, qm_sc, m_sc, l_sc, acc_sc, *, tq, hg):
    i = pl.program_id(2)
    hd = FOX_HEAD_DIM
    l_sc[...] = jnp.zeros_like(l_sc)
    acc_sc[...] = jnp.zeros_like(acc_sc)
    lane = lax.broadcasted_iota(jnp.int32, (tq, PAIR_LANES), 1)
    for hh in range(hg):
        own = (lane // hd) % 2 == hh % 2
        qm_sc[hh] = jnp.where(own, q_ref[0, hh // 2], jnp.zeros((), BF16))

    def accumulate_online(hh, st, off):
        m_prev = m_sc[hh]
        m_new = jnp.maximum(m_prev, jnp.max(st, axis=0, keepdims=True))
        alpha = jnp.exp2(m_prev - m_new)
        p = jnp.exp2(st - m_new)
        l_sc[hh] = alpha * l_sc[hh] + jnp.sum(p, axis=0, keepdims=True)
        vt = vt_ref[0, hh * hd:(hh + 1) * hd, pl.ds(off, tq)]
        acc_sc[hh] = alpha * acc_sc[hh] + _dot(vt, p.astype(BF16))
        m_sc[hh] = m_new

    def accumulate_direct(hh, st, off):
        p = jnp.exp2(st)
        l_sc[hh] += jnp.sum(p, axis=0, keepdims=True)
        acc_sc[hh] += _dot(vt_ref[0, hh * hd:(hh + 1) * hd, pl.ds(off, tq)], p.astype(BF16))

    def sweep(accumulate):
        def block(j, masked):
            off = pl.multiple_of(j * tq, tq)
            scores = lambda h_: _dot_nt(k_ref[0, h_ // 2, pl.ds(off, tq), :], qm_sc[h_])
            ahead = 2
            pending = [scores(h_) for h_ in range(min(ahead, hg))]
            for hh in range(hg):
                st = pending.pop(0)
                if hh + ahead < hg:
                    pending.append(scores(hh + ahead))
                if masked:
                    key = lax.broadcasted_iota(jnp.int32, st.shape, 0)
                    qry = lax.broadcasted_iota(jnp.int32, st.shape, 1)
                    st = jnp.where(key <= qry, st, -jnp.inf)
                accumulate(hh, st, off)

        def body(j, carry):
            block(j, False)
            return carry

        lax.fori_loop(first_ref[pl.program_id(0), i], i, body, 0)
        block(i, True)

    direct = direct_ref[pl.program_id(0), i] != 0

    @pl.when(direct)
    def _():
        sweep(accumulate_direct)

    @pl.when(jnp.logical_not(direct))
    def _():
        m_sc[...] = jnp.full_like(m_sc, -jnp.inf)
        sweep(accumulate_online)

    for hh in range(hg):
        o_ref[0, hh * hd:(hh + 1) * hd, :] = (acc_sc[hh] / l_sc[hh]).astype(o_ref.dtype)


def _fox_attention(q, k, vt, first_block, direct_ok, tq, hg):
    b, pairs, s, _ = q.shape
    assert hg % 2 == 0
    groups = 2 * pairs // hg
    rows = hg * FOX_HEAD_DIM
    grid_spec = pltpu.PrefetchScalarGridSpec(
        num_scalar_prefetch=2,
        grid=(b, groups, s // tq),
        in_specs=[pl.BlockSpec((1, hg // 2, tq, PAIR_LANES), lambda bi, pi, qi, *_: (bi, pi, qi, 0)),
                  pl.BlockSpec((1, hg // 2, s, PAIR_LANES), lambda bi, pi, qi, *_: (bi, pi, 0, 0)),
                  pl.BlockSpec((1, rows, s), lambda bi, pi, qi, *_: (bi, pi, 0))],
        out_specs=pl.BlockSpec((1, rows, tq), lambda bi, pi, qi, *_: (bi, pi, qi)),
        scratch_shapes=[pltpu.VMEM((hg, tq, PAIR_LANES), BF16),
                        pltpu.VMEM((hg, 1, tq), F32), pltpu.VMEM((hg, 1, tq), F32),
                        pltpu.VMEM((hg, FOX_HEAD_DIM, tq), F32)])
    return pl.pallas_call(
        functools.partial(_fox_kernel, tq=tq, hg=hg),
        grid_spec=grid_spec,
        out_shape=jax.ShapeDtypeStruct((b, groups * rows, s), BF16),
        compiler_params=pltpu.CompilerParams(dimension_semantics=("arbitrary",) * 3,
                                             vmem_limit_bytes=VMEM_LIMIT),
        name="fox_attention",
    )(first_block, direct_ok, q, k, vt)


def _hgrn_chunk_stages(q_ref, k_ref, i_ref, st_sc, b_sc, o_sc, tc):
    c_len = HGRN_CHUNK
    n_chunks = tc // c_len
    t_idx = lax.broadcasted_iota(jnp.int32, (c_len, c_len), 0)
    s_idx = lax.broadcasted_iota(jnp.int32, (c_len, c_len), 1)
    causal = s_idx <= t_idx
    units = [(c, h) for c in range(n_chunks) for h in range(HGRN_HEADS)]
    rows = lambda c: slice(c * c_len, (c + 1) * c_len)
    lanes = lambda h: slice(h * HGRN_DIM, (h + 1) * HGRN_DIM)
    qe, ke, kl, decay, attn, intra, d_state, state = ({} for _ in range(8))

    def decay_operands():
        for c, h in units:
            b = b_sc[rows(c), lanes(h)]
            kk = k_ref[0, rows(c), lanes(h)]
            b_last = b[c_len - 1:c_len, :]
            qe[c, h] = (q_ref[0, rows(c), lanes(h)] * jnp.exp(b)).astype(BF16)
            ke[c, h] = (kk * jnp.exp(-b)).astype(BF16)
            kl[c, h] = (kk * jnp.exp(b_last - b)).astype(BF16)
            decay[c, h] = jnp.exp(b_last)

    def scores():
        for u in units:
            attn[u] = jnp.where(causal, _dot_nt(qe[u], ke[u]), 0.0).astype(BF16)

    def products():
        for c, h in units:
            intra[c, h] = _dot(attn[c, h], i_ref[0, rows(c), lanes(h)])
            d_state[c, h] = _dot_tn(i_ref[0, rows(c), lanes(h)], kl[c, h])

    def recurrence():
        for h in range(HGRN_HEADS):
            st = st_sc[h]
            for c in range(n_chunks):
                state[c, h] = st.astype(BF16)
                st = st * decay[c, h] + d_state[c, h]
            st_sc[h] = st

    def outputs():
        for c, h in units:
            o_sc[rows(c), lanes(h)] = intra[c, h] + _dot_nt(qe[c, h], state[c, h])

    return [decay_operands, scores, products, recurrence, outputs]


def _hgrn_step_path(q_ref, k_ref, i_ref, st_sc, b_sc, o_sc, tc):
    row8 = lax.broadcasted_iota(jnp.int32, (8, HGRN_DIM), 0)
    half = HGRN_STEP // 2

    def step(u, carry):
        base = pl.multiple_of(u * HGRN_STEP, HGRN_STEP)
        for h in range(HGRN_HEADS):
            ls = slice(h * HGRN_DIM, (h + 1) * HGRN_DIM)
            rows = pl.ds(base, HGRN_STEP)
            bq = b_sc[rows, ls]
            qq = q_ref[0, rows, ls]
            kk = k_ref[0, rows, ls]
            ii = i_ref[0, rows, ls].astype(F32)
            b_last = bq[HGRN_STEP - 1:HGRN_STEP, :]
            st = st_sc[h]
            inter = _dot_nt((qq * jnp.exp(bq)).astype(BF16), st.astype(BF16))
            k_dec = (kk * jnp.exp(b_last - bq)).astype(BF16)
            st_sc[h] = st * jnp.exp(b_last) + _dot_tn(ii.astype(BF16), k_dec)
            out = [inter[0:half], inter[half:HGRN_STEP]]
            qv = [qq[0:half], qq[half:HGRN_STEP]]
            bv = [bq[0:half], bq[half:HGRN_STEP]]
            for s_ in range(HGRN_STEP):
                k_s, b_s, i_s = kk[s_:s_ + 1, :], bq[s_:s_ + 1, :], ii[s_:s_ + 1, :]
                for v_ in range(s_ // half, 2):
                    diff = bv[v_] - b_s
                    if s_ > v_ * half:
                        diff = jnp.where(row8 + v_ * half >= s_, diff, -jnp.inf)
                    a = jnp.sum(qv[v_] * k_s * jnp.exp(diff), axis=-1, keepdims=True)
                    out[v_] = out[v_] + a * i_s
            o_sc[pl.ds(base, half), ls] = out[0]
            o_sc[pl.ds(base + half, half), ls] = out[1]
        return carry

    lax.fori_loop(0, tc // HGRN_STEP, step, 0)


def _block_tri(n, blk):
    return jnp.asarray(np.kron(np.eye(n // blk, dtype=np.float32), np.tril(np.ones((blk, blk), np.float32))), BF16)


def _memkv_kernel(mem_ref, g_ref, w_ref, k_ref, v_ref):
    d = mem_ref.shape[-1]
    kv = _dot(_rms(mem_ref[0], g_ref[...]).astype(BF16), w_ref[...].astype(BF16))
    k_ref[0] = kv[:, :d].astype(BF16)
    v_ref[0] = kv[:, d:].astype(BF16)


def _mem_kv(mem, norm_g, w_kv):
    b, m, d = mem.shape
    out = jax.ShapeDtypeStruct((b, m, d), BF16)
    spec = pl.BlockSpec((1, m, d), lambda bi: (bi, 0, 0))
    return pl.pallas_call(
        _memkv_kernel,
        grid=(b,),
        in_specs=[spec, pl.BlockSpec((1, d), lambda bi: (0, 0)), pl.BlockSpec((d, 2 * d), lambda bi: (0, 0))],
        out_specs=[spec, spec],
        out_shape=[out, out],
        compiler_params=pltpu.CompilerParams(dimension_semantics=("arbitrary",), vmem_limit_bytes=VMEM_LIMIT),
        name="mem_kv",
    )(mem, norm_g.reshape(1, d), w_kv)


def _mix_kernel(x_ref, foxt_ref, rec_ref, wo_ref, gx_ref, wq_ref, mk_ref, mv_ref, wxo_ref, o_ref,
                h1_sc, q_sc, att_sc, *, sub):
    d = x_ref.shape[-1]
    fw = foxt_ref.shape[1]
    hd = d // X_HEADS
    inv = 1.0 / math.sqrt(hd)
    n_sub = x_ref.shape[1] // sub

    def out_proj(i):
        r = slice(i * sub, (i + 1) * sub)
        h1_sc[r, :] = (x_ref[0, r, :] + _dot_tn(foxt_ref[0, :, r], wo_ref[0:fw, :].astype(BF16))
                       + _dot(rec_ref[0, r, :], wo_ref[fw:, :].astype(BF16)))

    def query(i):
        r = slice(i * sub, (i + 1) * sub)
        q_sc[r, :] = _dot(_rms(h1_sc[r, :], gx_ref[...]).astype(BF16), wq_ref[...].astype(BF16)).astype(BF16)

    def attend(i):
        r = slice(i * sub, (i + 1) * sub)
        for h in range(X_HEADS):
            cs = slice(h * hd, (h + 1) * hd)
            s = _dot_nt(q_sc[r, cs], mk_ref[0, :, cs]) * inv
            p = jnp.exp(s - jnp.max(s, axis=-1, keepdims=True))
            p = p / jnp.sum(p, axis=-1, keepdims=True)
            att_sc[r, cs] = _dot(p.astype(BF16), mv_ref[0, :, cs]).astype(BF16)

    def finish(i):
        r = slice(i * sub, (i + 1) * sub)
        o_ref[0, r, :] = h1_sc[r, :] + _dot(att_sc[r, :], wxo_ref[...].astype(BF16))

    stages = (out_proj, query, attend, finish)
    for t in range(n_sub + len(stages) - 1):
        for k, stage in enumerate(stages):
            if 0 <= t - k < n_sub:
                stage(t - k)


def _mix(x, foxt, rec, w_out, norm_g, w_xq, mem_k, mem_v, w_xo, tm):
    b, s, d = x.shape
    m = mem_k.shape[1]
    fw = foxt.shape[1]
    tok = lambda wd: pl.BlockSpec((1, tm, wd), lambda bi, ti: (bi, ti, 0))
    const = _resident
    mem_spec = pl.BlockSpec((1, m, d), lambda bi, ti: (bi, 0, 0))
    return pl.pallas_call(
        functools.partial(_mix_kernel, sub=min(tm, 256)),
        grid=(b, s // tm),
        in_specs=[tok(d), pl.BlockSpec((1, fw, tm), lambda bi, ti: (bi, 0, ti)), tok(rec.shape[-1]),
                  const((d, d)), const((1, d)), const((d, d)), mem_spec, mem_spec, const((d, d))],
        out_specs=tok(d),
        out_shape=jax.ShapeDtypeStruct((b, s, d), F32),
        scratch_shapes=[pltpu.VMEM((tm, d), F32), pltpu.VMEM((tm, d), BF16), pltpu.VMEM((tm, d), BF16)],
        compiler_params=pltpu.CompilerParams(dimension_semantics=("arbitrary", "arbitrary"),
                                             vmem_limit_bytes=VMEM_LIMIT),
        name="mix",
    )(x, foxt, rec, w_out, norm_g.reshape(1, d), w_xq, mem_k, mem_v, w_xo)


def _mlp_kernel(h_ref, g_ref, w1_ref, w2_ref, gf_ref, o_ref, acc_sc, *, fc):
    hn = _rms(h_ref[...], g_ref[...]).astype(BF16)
    n_chunks = w1_ref.shape[1] // fc
    up = lambda c: _dot(hn, w1_ref[:, c * fc:(c + 1) * fc].astype(BF16))
    u_next = up(0)
    for c in range(n_chunks):
        u = jnp.maximum(u_next, 0.0)
        if c + 1 < n_chunks:
            u_next = up(c + 1)
        part = _dot((u * u).astype(BF16), w2_ref[c * fc:(c + 1) * fc, :].astype(BF16))
        if c == 0:
            acc_sc[...] = part
        else:
            acc_sc[...] += part
    o_ref[...] = _rms(h_ref[...] + acc_sc[...], gf_ref[...])


def _mlp(h, norm_g, w1, w2, final_g, tm, fc):
    t, d = h.shape
    dff = w1.shape[1]
    return pl.pallas_call(
        functools.partial(_mlp_kernel, fc=fc),
        grid=(t // tm,),
        in_specs=[pl.BlockSpec((tm, d), lambda ti: (ti, 0)), _resident((1, d)),
                  _resident((d, dff)), _resident((dff, d)), _resident((1, d))],
        out_specs=pl.BlockSpec((tm, d), lambda ti: (ti, 0)),
        out_shape=jax.ShapeDtypeStruct((t, d), F32),
        scratch_shapes=[pltpu.VMEM((tm, d), F32)],
        compiler_params=pltpu.CompilerParams(dimension_semantics=("arbitrary",), vmem_limit_bytes=VMEM_LIMIT),
        name="mlp",
    )(h, norm_g.reshape(1, d), w1, w2, final_g.reshape(1, d))


def _tile(n, want):
    t = min(n, want)
    assert n % t == 0, (n, want)
    return t


def kernel(x, mem, norm_mix_g, w_in, fox_f_bias, hgrn_lb_logits, hgrn_norm_g, w_out, norm_x_g, norm_mem_g,
           w_xq, w_xkv, w_xo, norm_ff_g, w1, w2, final_norm_g):
    b, s, d = x.shape
    h = x
    for l in range(w_in.shape[0]):
        q, k, v, stats, rec = _front(h, norm_mix_g[l], w_in[l], fox_f_bias[l], hgrn_lb_logits, hgrn_norm_g[l], l,
                                     _tile(s, 512))
        tq = _tile(s, 512)
        fox = _fox_attention(q, k, v, _fox_first_block(stats, tq), _fox_direct_ok(stats, tq), tq, 8)
        mem_k, mem_v = _mem_kv(mem, norm_mem_g[l], w_xkv[l])
        h = _mix(h, fox, rec, w_out[l], norm_x_g[l], w_xq[l], mem_k, mem_v, w_xo[l], _tile(s, 1024))
        is_last = l == w_in.shape[0] - 1
        assert is_last, "the MLP kernel fuses the final norm, so it must be the last layer"
        h = _mlp(h.reshape(b * s, d), norm_ff_g[l], w1[l], w2[l], final_norm_g,
                 _tile(b * s, 512), _tile(w1.shape[-1], 1024)).reshape(b, s, d)
    return h
```

```python
import functools
import math

import jax
import jax.numpy as jnp
import numpy as np
from jax import lax
from jax.experimental import pallas as pl
from jax.experimental.pallas import tpu as pltpu

EPS = 1e-6
LOG2E = math.log2(math.e)
LANES = 128
FOX_HEADS = 8
FOX_HEAD_DIM = 64
FOX_WIDTH = FOX_HEADS * FOX_HEAD_DIM
HGRN_HEADS = 4
HGRN_DIM = 128
HGRN_WIDTH = HGRN_HEADS * HGRN_DIM
X_HEADS = 4
HGRN_STEP = 16
HGRN_CHUNK = 64
HGRN_MIN_CHUNK_LOG_DECAY = -60.0
VMEM_LIMIT = 56 * 1024 * 1024

PAIR_LANES = 2 * LANES
AUG_F = 0
AUG_ONE = 3
AUG_SHIFT = 6
FCAT_ONE_LANE = 24
FCAT_SHIFT_LANE = 32
NORM_HEADROOM = 1.0 + 2.0 ** -7
FOX_SHIFT_MARGIN = 60.0
FOX_SPREAD_MAX = 120.0
STAT_BLOCK = 256
FOX_SKIP_LOG2 = 100.0

BF16 = jnp.bfloat16
F32 = jnp.float32


def _dot(a, b):
    return jnp.dot(a, b, preferred_element_type=F32)


def _dot_nt(a, b):
    return lax.dot_general(a, b, (((1,), (1,)), ((), ())), preferred_element_type=F32)


def _dot_tn(a, b):
    return lax.dot_general(a, b, (((0,), (0,)), ((), ())), preferred_element_type=F32)


def _split3(v):
    hi = v.astype(BF16)
    r1 = v - hi.astype(F32)
    mid = r1.astype(BF16)
    lo = (r1 - mid.astype(F32)).astype(BF16)
    return hi, mid, lo


def _tri_cumsum(tri, v):
    hi, mid, lo = _split3(v)
    return _dot(tri, hi) + _dot(tri, mid) + _dot(tri, lo)


def _rms(x, g):
    ms = jnp.mean(x * x, axis=-1, keepdims=True)
    return x * lax.rsqrt(ms + EPS) * g


def _resident(shape):
    return pl.BlockSpec(shape, lambda *_: (0,) * len(shape), pipeline_mode=pl.Buffered(1))


def _front_kernel(x_ref, g_ref, wt_ref, fb_ref, lbl_ref, sel_ref, tri_ref, hsum_ref, ng_ref, tri64_ref, tri16_ref,
                  q_ref, k_ref, vt_ref, stat_ref, rec_ref,
                  carry_ref, kmax_ref, hb_sc, hq_sc, hk_sc, hg_sc, hi_sc, gate_sc, st_sc, b_sc, o_sc, *, layer):
    @pl.when(pl.program_id(1) == 0)
    def _():
        carry_ref[...] = jnp.zeros_like(carry_ref)
        kmax_ref[...] = jnp.zeros_like(kmax_ref)
        st_sc[...] = jnp.zeros_like(st_sc)

    tm = x_ref.shape[1]
    hb_sc[...] = _rms(x_ref[0], g_ref[...]).astype(BF16)
    lane = lax.broadcasted_iota(jnp.int32, (tm, LANES), 1)
    fw, w = FOX_WIDTH, HGRN_WIDTH
    proj = lambda r0, n: _dot_nt(hb_sc[...], wt_ref[r0:r0 + n, :].astype(BF16))

    lbl = lbl_ref[...]
    e = jnp.exp(lbl - jnp.max(lbl, axis=0, keepdims=True))
    lb = jnp.sum(e[0:layer + 1, :], axis=0, keepdims=True) / jnp.sum(e, axis=0, keepdims=True)
    seg = lambda n: proj(3 * fw + FOX_HEADS + n * w, w)
    gq = seg(0)
    hq_sc[0] = gq * jax.nn.sigmoid(gq)
    f = lb + (1.0 - lb) * jax.nn.sigmoid(seg(1))
    hk_sc[0] = 1.0 - f
    hg_sc[...] = jnp.log(f)
    hi_sc[0] = seg(2).astype(BF16)
    gg = seg(3)
    gate_sc[...] = (gg * jax.nn.sigmoid(gg)).astype(BF16)

    cs_rows = tri64_ref.shape[0]
    for r0 in range(0, tm, cs_rows):
        b_sc[r0:r0 + cs_rows, :] = _tri_cumsum(tri64_ref[...], hg_sc[r0:r0 + cs_rows, :])
    chunk_ok = jnp.min(b_sc[...]) >= HGRN_MIN_CHUNK_LOG_DECAY

    def hgrn_output():
        for h in range(HGRN_HEADS):
            ls = slice(h * HGRN_DIM, (h + 1) * HGRN_DIM)
            rec_ref[0, :, ls] = (_rms(o_sc[:, ls], ng_ref[...]) * gate_sc[:, ls]).astype(rec_ref.dtype)

    v = {}

    def fox_forget():
        z = proj(3 * fw, LANES) + fb_ref[...]
        logf = jnp.minimum(z, 0.0) - jnp.log(1.0 + jnp.exp(-jnp.abs(z)))
        logf = jnp.where(lane < FOX_HEADS, logf, 0.0)
        fcum = _tri_cumsum(tri_ref[...], logf) + carry_ref[0:1, :]
        carry_ref[...] = jnp.broadcast_to(fcum[tm - 1:tm, :], carry_ref.shape)
        v["fsc"] = fcum * LOG2E

    def fox_qk():
        q_scale = LOG2E / math.sqrt(FOX_HEAD_DIM)
        v["qk"] = jnp.concatenate([(proj(0, fw) * q_scale).astype(BF16), proj(fw, fw).astype(BF16)],
                                  axis=1)

    def fox_bounds():
        qkf = v["qk"].astype(F32)
        norm2 = _dot((qkf * qkf).astype(BF16), hsum_ref[...]) * NORM_HEADROOM
        norm = jnp.sqrt(norm2)
        to_q_lanes = lambda a: pltpu.roll(a, LANES - FOX_HEADS, 1)
        k_run = jnp.maximum(kmax_ref[...], jnp.max(norm, axis=0, keepdims=True))
        kmax_ref[...] = k_run
        k_run_q = to_q_lanes(k_run)[0:1, :]
        v["shift"] = jnp.where(lane < FOX_HEADS, norm * k_run_q - FOX_SHIFT_MARGIN, 0.0)
        v["spread"] = norm * (k_run_q + to_q_lanes(norm))
        v["norm2"] = norm2

    def fox_operands():
        parts = lambda a: [p.astype(F32) for p in _split3(a)]
        f_parts, s_parts = parts(v["fsc"]), parts(-v["shift"])
        fcat = jnp.where(lane == FCAT_ONE_LANE, 1.0, 0.0)
        for j in range(3):
            fcat = fcat + (pltpu.roll(f_parts[j], j * FOX_HEADS, 1) if j else f_parts[j])
            fcat = fcat + pltpu.roll(s_parts[j], FCAT_SHIFT_LANE + j * FOX_HEADS, 1)
        aug = _dot(fcat.astype(BF16), sel_ref[...]).astype(BF16)
        qk, pairs = v["qk"], FOX_HEADS // 2
        for p in range(pairs):
            q_ref[0, p, :, 0:LANES] = qk[:, p * LANES:(p + 1) * LANES]
            q_ref[0, p, :, LANES:PAIR_LANES] = aug[:, p * LANES:(p + 1) * LANES]
            k_ref[0, p, :, 0:LANES] = qk[:, (pairs + p) * LANES:(pairs + p + 1) * LANES]
            k_ref[0, p, :, LANES:PAIR_LANES] = aug[:, (pairs + p) * LANES:(pairs + p + 1) * LANES]

    def fox_values():
        vt_ref[0] = _dot_nt(wt_ref[2 * fw:3 * fw, :].astype(BF16), hb_sc[...]).astype(BF16)

    def fox_stats():
        for sb in range(tm // STAT_BLOCK):
            rows = slice(sb * STAT_BLOCK, (sb + 1) * STAT_BLOCK)
            stat_ref[0, sb, 0:1, :] = jnp.max(v["norm2"][rows, :], axis=0, keepdims=True)
            stat_ref[0, sb, 1:2, :] = v["fsc"][rows, :][0:1, :]
            stat_ref[0, sb, 2:3, :] = v["fsc"][rows, :][STAT_BLOCK - 1:STAT_BLOCK, :]
            stat_ref[0, sb, 3:4, :] = jnp.max(v["spread"][rows, :], axis=0, keepdims=True)
            stat_ref[0, sb, 4:8, :] = jnp.zeros((4, LANES), F32)

    fox_stages = [fox_forget, fox_qk, fox_bounds, fox_operands, fox_values, fox_stats]

    @pl.when(chunk_ok)
    def _():
        hgrn_stages = _hgrn_chunk_stages(hq_sc, hk_sc, hi_sc, st_sc, b_sc, o_sc, tm) + [hgrn_output]
        for a, b in zip(hgrn_stages, fox_stages):
            a()
            b()

    @pl.when(jnp.logical_not(chunk_ok))
    def _():
        for r0 in range(0, tm, cs_rows):
            b_sc[r0:r0 + cs_rows, :] = _tri_cumsum(tri16_ref[...], hg_sc[r0:r0 + cs_rows, :])
        _hgrn_step_path(hq_sc, hk_sc, hi_sc, st_sc, b_sc, o_sc, tm)
        hgrn_output()
        for stage in fox_stages:
            stage()


def _front(x, norm_g, w_in, fox_f_bias, lb_logits, hgrn_norm_g, layer, tm):
    b, s, d = x.shape
    fw, hw = FOX_WIDTH, HGRN_WIDTH
    assert w_in.shape == (d, 3 * fw + FOX_HEADS + 4 * hw)
    wt = w_in.T
    fb = jnp.pad(fox_f_bias.reshape(1, FOX_HEADS), ((0, 0), (0, LANES - FOX_HEADS)))

    sel = np.zeros((LANES, 2 * fw), np.float32)
    for h in range(FOX_HEADS):
        col = h * FOX_HEAD_DIM
        for j in range(3):
            sel[j * FOX_HEADS + h, col + AUG_F + j] = 1.0
            sel[FCAT_ONE_LANE, col + AUG_ONE + j] = 1.0
            sel[FCAT_ONE_LANE, fw + col + AUG_F + j] = 1.0
            sel[j * FOX_HEADS + h, fw + col + AUG_ONE + j] = -1.0
            sel[FCAT_SHIFT_LANE + j * FOX_HEADS + h, col + AUG_SHIFT + j] = 1.0
            sel[FCAT_ONE_LANE, fw + col + AUG_SHIFT + j] = 1.0
    sel = jnp.asarray(sel, BF16)
    tri = jnp.asarray(np.tril(np.ones((tm, tm), np.float32)), BF16)
    hsum = np.zeros((2 * fw, LANES), np.float32)
    hsum[np.arange(2 * fw), np.arange(2 * fw) // FOX_HEAD_DIM] = 1.0
    hsum = jnp.asarray(hsum, BF16)

    cs_rows = min(tm, 256)
    tri64, tri16 = _block_tri(cs_rows, HGRN_CHUNK), _block_tri(cs_rows, HGRN_STEP)

    const = _resident
    pairs = FOX_HEADS // 2
    head_out = jax.ShapeDtypeStruct((b, pairs, s, PAIR_LANES), BF16)
    head_spec = pl.BlockSpec((1, pairs, tm, PAIR_LANES), lambda bi, ti: (bi, 0, ti, 0))
    vm = pltpu.VMEM
    return pl.pallas_call(
        functools.partial(_front_kernel, layer=layer),
        grid=(b, s // tm),
        in_specs=[pl.BlockSpec((1, tm, d), lambda bi, ti: (bi, ti, 0)),
                  const((1, d)), const(wt.shape), const(fb.shape), const(lb_logits.shape), const(sel.shape),
                  const(tri.shape), const(hsum.shape), const((1, HGRN_DIM)), const(tri64.shape), const(tri16.shape)],
        out_specs=[head_spec, head_spec, pl.BlockSpec((1, fw, tm), lambda bi, ti: (bi, 0, ti)),
                   pl.BlockSpec((1, tm // STAT_BLOCK, 8, LANES), lambda bi, ti: (bi, ti, 0, 0)),
                   pl.BlockSpec((1, tm, hw), lambda bi, ti: (bi, ti, 0))],
        out_shape=[head_out, head_out, jax.ShapeDtypeStruct((b, fw, s), BF16),
                   jax.ShapeDtypeStruct((b, s // STAT_BLOCK, 8, LANES), F32),
                   jax.ShapeDtypeStruct((b, s, hw), BF16)],
        scratch_shapes=[vm((8, LANES), F32), vm((8, LANES), F32), vm((tm, d), BF16),
                        vm((1, tm, hw), F32), vm((1, tm, hw), F32), vm((tm, hw), F32),
                        vm((1, tm, hw), BF16), vm((tm, hw), BF16),
                        vm((HGRN_HEADS, HGRN_DIM, HGRN_DIM), F32), vm((tm, hw), F32), vm((tm, hw), F32)],
        compiler_params=pltpu.CompilerParams(dimension_semantics=("arbitrary", "arbitrary"),
                                             vmem_limit_bytes=VMEM_LIMIT),
        name="front",
    )(x, norm_g.reshape(1, d), wt, fb, lb_logits, sel, tri, hsum, hgrn_norm_g.reshape(1, HGRN_DIM), tri64, tri16)


def _fox_first_block(stats, tq):
    b, nsb = stats.shape[0], stats.shape[1]
    assert tq % STAT_BLOCK == 0, (tq, STAT_BLOCK)
    r = tq // STAT_BLOCK
    st = stats.reshape(b, nsb // r, r, 8, LANES)
    h = FOX_HEADS
    qn = jnp.sqrt(jnp.max(st[:, :, :, 0, 0:h], axis=2))
    kn = jnp.sqrt(jnp.max(st[:, :, :, 0, h:2 * h], axis=2))
    f_first, f_last = st[:, :, 0, 1, 0:h], st[:, :, r - 1, 2, 0:h]
    bound = (qn[:, :, None, :] * (kn[:, None, :, :] + kn[:, :, None, :])
             + f_first[:, :, None, :] - f_last[:, None, :, :])
    blk = jnp.arange(nsb // r)
    need = jnp.any(jnp.logical_not(bound <= -FOX_SKIP_LOG2), axis=-1) & (blk[None, None, :] < blk[None, :, None])
    first = jnp.min(jnp.where(need, blk[None, None, :], nsb // r), axis=-1)
    return jnp.minimum(first, blk[None, :]).astype(jnp.int32)


def _fox_direct_ok(stats, tq):
    b, nsb = stats.shape[0], stats.shape[1]
    spread = jnp.max(stats[:, :, 3, 0:FOX_HEADS].reshape(b, nsb * STAT_BLOCK // tq, -1), axis=-1)
    return (spread <= FOX_SPREAD_MAX).astype(jnp.int32)


def _fox_kernel(first_ref, direct_ref, q_ref, k_ref, vt_ref, o_ref, qm_sc, m_sc, l_sc, acc_sc, *, tq, hg):
    i = pl.program_id(2)
    hd = FOX_HEAD_DIM
    l_sc[...] = jnp.zeros_like(l_sc)
    acc_sc[...] = jnp.zeros_like(acc_sc)
    lane = lax.broadcasted_iota(jnp.int32, (tq, PAIR_LANES), 1)
    for hh in range(hg):
        own = (lane // hd) % 2 == hh % 2
        qm_sc[hh] = jnp.where(own, q_ref[0, hh // 2], jnp.zeros((), BF16))

    def accumulate_online(hh, st, off):
        m_prev = m_sc[hh]
        m_new = jnp.maximum(m_prev, jnp.max(st, axis=0, keepdims=True))
        alpha = jnp.exp2(m_prev - m_new)
        p = jnp.exp2(st - m_new)
        l_sc[hh] = alpha * l_sc[hh] + jnp.sum(p, axis=0, keepdims=True)
        vt = vt_ref[0, hh * hd:(hh + 1) * hd, pl.ds(off, tq)]
        acc_sc[hh] = alpha * acc_sc[hh] + _dot(vt, p.astype(BF16))
        m_sc[hh] = m_new

    def accumulate_direct(hh, st, off):
        p = jnp.exp2(st)
        l_sc[hh] += jnp.sum(p, axis=0, keepdims=True)
        acc_sc[hh] += _dot(vt_ref[0, hh * hd:(hh + 1) * hd, pl.ds(off, tq)], p.astype(BF16))

    def sweep(accumulate):
        def block(j, masked):
            off = pl.multiple_of(j * tq, tq)
            scores = lambda h_: _dot_nt(k_ref[0, h_ // 2, pl.ds(off, tq), :], qm_sc[h_])
            ahead = 2
            pending = [scores(h_) for h_ in range(min(ahead, hg))]
            for hh in range(hg):
                st = pending.pop(0)
                if hh + ahead < hg:
                    pending.append(scores(hh + ahead))
                if masked:
                    key = lax.broadcasted_iota(jnp.int32, st.shape, 0)
                    qry = lax.broadcasted_iota(jnp.int32, st.shape, 1)
                    st = jnp.where(key <= qry, st, -jnp.inf)
                accumulate(hh, st, off)

        def body(j, carry):
            block(j, False)
            return carry

        lax.fori_loop(first_ref[pl.program_id(0), i], i, body, 0)
        block(i, True)

    direct = direct_ref[pl.program_id(0), i] != 0

    @pl.when(direct)
    def _():
        sweep(accumulate_direct)

    @pl.when(jnp.logical_not(direct))
    def _():
        m_sc[...] = jnp.full_like(m_sc, -jnp.inf)
        sweep(accumulate_online)

    for hh in range(hg):
        o_ref[0, hh * hd:(hh + 1) * hd, :] = (acc_sc[hh] / l_sc[hh]).astype(o_ref.dtype)


def _fox_attention(q, k, vt, first_block, direct_ok, tq, hg):
    b, pairs, s, _ = q.shape
    assert hg % 2 == 0
    groups = 2 * pairs // hg
    rows = hg * FOX_HEAD_DIM
    grid_spec = pltpu.PrefetchScalarGridSpec(
        num_scalar_prefetch=2,
        grid=(b, groups, s // tq),
        in_specs=[pl.BlockSpec((1, hg // 2, tq, PAIR_LANES), lambda bi, pi, qi, *_: (bi, pi, qi, 0)),
                  pl.BlockSpec((1, hg // 2, s, PAIR_LANES), lambda bi, pi, qi, *_: (bi, pi, 0, 0)),
                  pl.BlockSpec((1, rows, s), lambda bi, pi, qi, *_: (bi, pi, 0))],
        out_specs=pl.BlockSpec((1, rows, tq), lambda bi, pi, qi, *_: (bi, pi, qi)),
        scratch_shapes=[pltpu.VMEM((hg, tq, PAIR_LANES), BF16),
                        pltpu.VMEM((hg, 1, tq), F32), pltpu.VMEM((hg, 1, tq), F32),
                        pltpu.VMEM((hg, FOX_HEAD_DIM, tq), F32)])
    return pl.pallas_call(
        functools.partial(_fox_kernel, tq=tq, hg=hg),
        grid_spec=grid_spec,
        out_shape=jax.ShapeDtypeStruct((b, groups * rows, s), BF16),
        compiler_params=pltpu.CompilerParams(dimension_semantics=("arbitrary",) * 3,
                                             vmem_limit_bytes=VMEM_LIMIT),
        name="fox_attention",
    )(first_block, direct_ok, q, k, vt)


def _hgrn_chunk_stages(q_ref, k_ref, i_ref, st_sc, b_sc, o_sc, tc):
    c_len = HGRN_CHUNK
    n_chunks = tc // c_len
    t_idx = lax.broadcasted_iota(jnp.int32, (c_len, c_len), 0)
    s_idx = lax.broadcasted_iota(jnp.int32, (c_len, c_len), 1)
    causal = s_idx <= t_idx
    units = [(c, h) for c in range(n_chunks) for h in range(HGRN_HEADS)]
    rows = lambda c: slice(c * c_len, (c + 1) * c_len)
    lanes = lambda h: slice(h * HGRN_DIM, (h + 1) * HGRN_DIM)
    qe, ke, kl, decay, attn, intra, d_state, state = ({} for _ in range(8))

    def decay_operands():
        for c, h in units:
            b = b_sc[rows(c), lanes(h)]
            kk = k_ref[0, rows(c), lanes(h)]
            b_last = b[c_len - 1:c_len, :]
            qe[c, h] = (q_ref[0, rows(c), lanes(h)] * jnp.exp(b)).astype(BF16)
            ke[c, h] = (kk * jnp.exp(-b)).astype(BF16)
            kl[c, h] = (kk * jnp.exp(b_last - b)).astype(BF16)
            decay[c, h] = jnp.exp(b_last)

    def scores():
        for u in units:
            attn[u] = jnp.where(causal, _dot_nt(qe[u], ke[u]), 0.0).astype(BF16)

    def products():
        for c, h in units:
            intra[c, h] = _dot(attn[c, h], i_ref[0, rows(c), lanes(h)])
            d_state[c, h] = _dot_tn(i_ref[0, rows(c), lanes(h)], kl[c, h])

    def recurrence():
        for h in range(HGRN_HEADS):
            st = st_sc[h]
            for c in range(n_chunks):
                state[c, h] = st.astype(BF16)
                st = st * decay[c, h] + d_state[c, h]
            st_sc[h] = st

    def outputs():
        for c, h in units:
            o_sc[rows(c), lanes(h)] = intra[c, h] + _dot_nt(qe[c, h], state[c, h])

    return [decay_operands, scores, products, recurrence, outputs]


def _hgrn_step_path(q_ref, k_ref, i_ref, st_sc, b_sc, o_sc, tc):
    row8 = lax.broadcasted_iota(jnp.int32, (8, HGRN_DIM), 0)
    half = HGRN_STEP // 2

    def step(u, carry):
        base = pl.multiple_of(u * HGRN_STEP, HGRN_STEP)
        for h in range(HGRN_HEADS):
            ls = slice(h * HGRN_DIM, (h + 1) * HGRN_DIM)
            rows = pl.ds(base, HGRN_STEP)
            bq = b_sc[rows, ls]
            qq = q_ref[0, rows, ls]
            kk = k_ref[0, rows, ls]
            ii = i_ref[0, rows, ls].astype(F32)
            b_last = bq[HGRN_STEP - 1:HGRN_STEP, :]
            st = st_sc[h]
            inter = _dot_nt((qq * jnp.exp(bq)).astype(BF16), st.astype(BF16))
            k_dec = (kk * jnp.exp(b_last - bq)).astype(BF16)
            st_sc[h] = st * jnp.exp(b_last) + _dot_tn(ii.astype(BF16), k_dec)
            out = [inter[0:half], inter[half:HGRN_STEP]]
            qv = [qq[0:half], qq[half:HGRN_STEP]]
            bv = [bq[0:half], bq[half:HGRN_STEP]]
            for s_ in range(HGRN_STEP):
                k_s, b_s, i_s = kk[s_:s_ + 1, :], bq[s_:s_ + 1, :], ii[s_:s_ + 1, :]
                for v_ in range(s_ // half, 2):
                    diff = bv[v_] - b_s
                    if s_ > v_ * half:
                        diff = jnp.where(row8 + v_ * half >= s_, diff, -jnp.inf)
                    a = jnp.sum(qv[v_] * k_s * jnp.exp(diff), axis=-1, keepdims=True)
                    out[v_] = out[v_] + a * i_s
            o_sc[pl.ds(base, half), ls] = out[0]
            o_sc[pl.ds(base + half, half), ls] = out[1]
        return carry

    lax.fori_loop(0, tc // HGRN_STEP, step, 0)


def _block_tri(n, blk):
    return jnp.asarray(np.kron(np.eye(n // blk, dtype=np.float32), np.tril(np.ones((blk, blk), np.float32))), BF16)


def _memkv_kernel(mem_ref, g_ref, wkv_ref, wq_ref, wo_ref, sm_ref, om_ref):
    d = mem_ref.shape[-1]
    m = mem_ref.shape[1]
    hd = d // X_HEADS
    inv = 1.0 / math.sqrt(hd)
    kv = _dot(_rms(mem_ref[0], g_ref[...]).astype(BF16), wkv_ref[...].astype(BF16))
    for h in range(X_HEADS):
        cs = slice(h * hd, (h + 1) * hd)
        k_h = kv[:, cs].astype(BF16)
        v_h = kv[:, d + h * hd:d + (h + 1) * hd].astype(BF16)
        sm_ref[0, :, h * m:(h + 1) * m] = (_dot_nt(wq_ref[:, cs].astype(BF16), k_h) * inv).astype(BF16)
        om_ref[0, h * m:(h + 1) * m, :] = _dot(v_h, wo_ref[cs, :].astype(BF16)).astype(BF16)


def _mem_kv(mem, norm_g, w_kv, w_xq, w_xo):
    b, m, d = mem.shape
    return pl.pallas_call(
        _memkv_kernel,
        grid=(b,),
        in_specs=[pl.BlockSpec((1, m, d), lambda bi: (bi, 0, 0)), _resident((1, d)), _resident((d, 2 * d)),
                  _resident((d, d)), _resident((d, d))],
        out_specs=[pl.BlockSpec((1, d, X_HEADS * m), lambda bi: (bi, 0, 0)),
                   pl.BlockSpec((1, X_HEADS * m, d), lambda bi: (bi, 0, 0))],
        out_shape=[jax.ShapeDtypeStruct((b, d, X_HEADS * m), BF16), jax.ShapeDtypeStruct((b, X_HEADS * m, d), BF16)],
        compiler_params=pltpu.CompilerParams(dimension_semantics=("arbitrary",), vmem_limit_bytes=VMEM_LIMIT),
        name="mem_kv",
    )(mem, norm_g.reshape(1, d), w_kv, w_xq, w_xo)


def _mix_kernel(x_ref, foxt_ref, rec_ref, wo_ref, gx_ref, sm_ref, om_ref, o_ref, h1_sc, p_sc, *, sub):
    fw = foxt_ref.shape[1]
    n_mem = sm_ref.shape[2] // X_HEADS
    n_sub = x_ref.shape[1] // sub

    def out_proj(i):
        r = slice(i * sub, (i + 1) * sub)
        h1_sc[r, :] = (x_ref[0, r, :] + _dot_tn(foxt_ref[0, :, r], wo_ref[0:fw, :].astype(BF16))
                       + _dot(rec_ref[0, r, :], wo_ref[fw:, :].astype(BF16)))

    def attend(i):
        r = slice(i * sub, (i + 1) * sub)
        s = _dot(_rms(h1_sc[r, :], gx_ref[...]).astype(BF16), sm_ref[0])
        for h in range(X_HEADS):
            cs = slice(h * n_mem, (h + 1) * n_mem)
            p = jnp.exp(s[:, cs] - jnp.max(s[:, cs], axis=-1, keepdims=True))
            p_sc[r, cs] = (p / jnp.sum(p, axis=-1, keepdims=True)).astype(BF16)

    def finish(i):
        r = slice(i * sub, (i + 1) * sub)
        o_ref[0, r, :] = h1_sc[r, :] + _dot(p_sc[r, :], om_ref[0])

    stages = (out_proj, attend, finish)
    for t in range(n_sub + len(stages) - 1):
        for k, stage in enumerate(stages):
            if 0 <= t - k < n_sub:
                stage(t - k)


def _mix(x, foxt, rec, w_out, norm_g, score_m, out_m, tm):
    b, s, d = x.shape
    fw = foxt.shape[1]
    hm = score_m.shape[2]
    tok = lambda wd: pl.BlockSpec((1, tm, wd), lambda bi, ti: (bi, ti, 0))
    return pl.pallas_call(
        functools.partial(_mix_kernel, sub=min(tm, 256)),
        grid=(b, s // tm),
        in_specs=[tok(d), pl.BlockSpec((1, fw, tm), lambda bi, ti: (bi, 0, ti)), tok(rec.shape[-1]),
                  _resident((d, d)), _resident((1, d)),
                  pl.BlockSpec((1, d, hm), lambda bi, ti: (bi, 0, 0)), pl.BlockSpec((1, hm, d), lambda bi, ti: (bi, 0, 0))],
        out_specs=tok(d),
        out_shape=jax.ShapeDtypeStruct((b, s, d), F32),
        scratch_shapes=[pltpu.VMEM((tm, d), F32), pltpu.VMEM((tm, hm), BF16)],
        compiler_params=pltpu.CompilerParams(dimension_semantics=("arbitrary", "arbitrary"),
                                             vmem_limit_bytes=VMEM_LIMIT),
        name="mix",
    )(x, foxt, rec, w_out, norm_g.reshape(1, d), score_m, out_m)


def _mlp_kernel(h_ref, g_ref, w1_ref, w2_ref, gf_ref, o_ref, acc_sc, *, fc):
    hn = _rms(h_ref[...], g_ref[...]).astype(BF16)
    n_chunks = w1_ref.shape[1] // fc
    up = lambda c: _dot(hn, w1_ref[:, c * fc:(c + 1) * fc].astype(BF16))
    u_next = up(0)
    for c in range(n_chunks):
        u = jnp.maximum(u_next, 0.0)
        if c + 1 < n_chunks:
            u_next = up(c + 1)
        part = _dot((u * u).astype(BF16), w2_ref[c * fc:(c + 1) * fc, :].astype(BF16))
        if c == 0:
            acc_sc[...] = part
        else:
            acc_sc[...] += part
    o_ref[...] = _rms(h_ref[...] + acc_sc[...], gf_ref[...])


def _mlp(h, norm_g, w1, w2, final_g, tm, fc):
    t, d = h.shape
    dff = w1.shape[1]
    return pl.pallas_call(
        functools.partial(_mlp_kernel, fc=fc),
        grid=(t // tm,),
        in_specs=[pl.BlockSpec((tm, d), lambda ti: (ti, 0)), _resident((1, d)),
                  _resident((d, dff)), _resident((dff, d)), _resident((1, d))],
        out_specs=pl.BlockSpec((tm, d), lambda ti: (ti, 0)),
        out_shape=jax.ShapeDtypeStruct((t, d), F32),
        scratch_shapes=[pltpu.VMEM((tm, d), F32)],
        compiler_params=pltpu.CompilerParams(dimension_semantics=("arbitrary",), vmem_limit_bytes=VMEM_LIMIT),
        name="mlp",
    )(h, norm_g.reshape(1, d), w1, w2, final_g.reshape(1, d))


def _tile(n, want):
    t = min(n, want)
    assert n % t == 0, (n, want)
    return t


def _tiles(b, s, d_ff):
    return dict(
        front=_tile(s, 512),
        fox_q=_tile(s, 512),
        fox_heads=FOX_HEADS,
        mix=_tile(s, 1024),
        mlp=_tile(b * s, 512),
        mlp_ff=_tile(d_ff, 1024),
    )


def kernel(x, mem, norm_mix_g, w_in, fox_f_bias, hgrn_lb_logits, hgrn_norm_g, w_out, norm_x_g, norm_mem_g,
           w_xq, w_xkv, w_xo, norm_ff_g, w1, w2, final_norm_g):
    b, s, d = x.shape
    t = _tiles(b, s, w1.shape[-1])
    h = x
    for l in range(w_in.shape[0]):
        q, k, v, stats, rec = _front(h, norm_mix_g[l], w_in[l], fox_f_bias[l], hgrn_lb_logits, hgrn_norm_g[l], l,
                                     t["front"])
        fox = _fox_attention(q, k, v, _fox_first_block(stats, t["fox_q"]), _fox_direct_ok(stats, t["fox_q"]),
                             t["fox_q"], t["fox_heads"])
        score_m, out_m = _mem_kv(mem, norm_mem_g[l], w_xkv[l], w_xq[l], w_xo[l])
        h = _mix(h, fox, rec, w_out[l], norm_x_g[l], score_m, out_m, t["mix"])
        is_last = l == w_in.shape[0] - 1
        assert is_last, "the MLP kernel fuses the final norm, so it must be the last layer"
        h = _mlp(h.reshape(b * s, d), norm_ff_g[l], w1[l], w2[l], final_norm_g, t["mlp"], t["mlp_ff"]).reshape(b, s, d)
    return h
```

```python
import functools
import math

import jax
import jax.numpy as jnp
import numpy as np
from jax import lax
from jax.experimental import pallas as pl
from jax.experimental.pallas import tpu as pltpu

EPS = 1e-6
LOG2E = math.log2(math.e)
LANES = 128
FOX_HEADS = 8
FOX_HEAD_DIM = 64
FOX_WIDTH = FOX_HEADS * FOX_HEAD_DIM
HGRN_HEADS = 4
HGRN_DIM = 128
HGRN_WIDTH = HGRN_HEADS * HGRN_DIM
X_HEADS = 4
HGRN_STEP = 16
HGRN_CHUNK = 64
HGRN_MIN_CHUNK_LOG_DECAY = -60.0
VMEM_LIMIT = 56 * 1024 * 1024

PAIR_LANES = 2 * LANES
AUG_F = 0
AUG_ONE = 3
AUG_SHIFT = 6
FCAT_ONE_LANE = 24
FCAT_SHIFT_LANE = 32
NORM_HEADROOM = 1.0 + 2.0 ** -7
FOX_SHIFT_MARGIN = 60.0
FOX_SPREAD_MAX = 120.0
STAT_BLOCK = 256
FOX_SKIP_LOG2 = 100.0

BF16 = jnp.bfloat16
F32 = jnp.float32


def _dot(a, b):
    return jnp.dot(a, b, preferred_element_type=F32)


def _dot_nt(a, b):
    return lax.dot_general(a, b, (((1,), (1,)), ((), ())), preferred_element_type=F32)


def _dot_tn(a, b):
    return lax.dot_general(a, b, (((0,), (0,)), ((), ())), preferred_element_type=F32)


def _split3(v):
    hi = v.astype(BF16)
    r1 = v - hi.astype(F32)
    mid = r1.astype(BF16)
    lo = (r1 - mid.astype(F32)).astype(BF16)
    return hi, mid, lo


def _tri_cumsum(tri, v):
    hi, mid, lo = _split3(v)
    return _dot(tri, hi) + _dot(tri, mid) + _dot(tri, lo)


def _rms(x, g):
    ms = jnp.mean(x * x, axis=-1, keepdims=True)
    return x * lax.rsqrt(ms + EPS) * g


def _resident(shape):
    return pl.BlockSpec(shape, lambda *_: (0,) * len(shape), pipeline_mode=pl.Buffered(1))


def _front_kernel(x_ref, g_ref, wt_ref, fb_ref, lbl_ref, sel_ref, tri_ref, hsum_ref, ng_ref, tri64_ref, tri16_ref,
                  q_ref, k_ref, vt_ref, stat_ref, rec_ref,
                  carry_ref, kmax_ref, hb_sc, hq_sc, hk_sc, hg_sc, hi_sc, gate_sc, st_sc, st_prev_sc, b_sc, o_sc,
                  *, layer):
    @pl.when(pl.program_id(1) == 0)
    def _():
        carry_ref[...] = jnp.zeros_like(carry_ref)
        kmax_ref[...] = jnp.zeros_like(kmax_ref)
        st_sc[...] = jnp.zeros_like(st_sc)

    tm = x_ref.shape[1]
    hb_sc[...] = _rms(x_ref[0], g_ref[...]).astype(BF16)
    lane = lax.broadcasted_iota(jnp.int32, (tm, LANES), 1)
    fw, w = FOX_WIDTH, HGRN_WIDTH
    proj = lambda r0, n: _dot_nt(hb_sc[...], wt_ref[r0:r0 + n, :].astype(BF16))

    lbl = lbl_ref[...]
    e = jnp.exp(lbl - jnp.max(lbl, axis=0, keepdims=True))
    lb = jnp.sum(e[0:layer + 1, :], axis=0, keepdims=True) / jnp.sum(e, axis=0, keepdims=True)
    seg = lambda n: proj(3 * fw + FOX_HEADS + n * w, w)
    gf = seg(1)
    gq = seg(0)
    f = lb + (1.0 - lb) * jax.nn.sigmoid(gf)
    hk_sc[0] = 1.0 - f
    hg_sc[...] = jnp.log(f)
    gi = seg(2)
    hq_sc[0] = gq * jax.nn.sigmoid(gq)
    gg = seg(3)
    cs_rows = tri64_ref.shape[0]
    for r0 in range(0, tm, cs_rows):
        b_sc[r0:r0 + cs_rows, :] = _tri_cumsum(tri64_ref[...], hg_sc[r0:r0 + cs_rows, :])
    hi_sc[0] = gi.astype(BF16)
    gate_sc[...] = (gg * jax.nn.sigmoid(gg)).astype(BF16)
    chunk_ok = jnp.min(b_sc[...]) >= HGRN_MIN_CHUNK_LOG_DECAY
    st_prev_sc[...] = st_sc[...]

    def hgrn_output():
        for h in range(HGRN_HEADS):
            ls = slice(h * HGRN_DIM, (h + 1) * HGRN_DIM)
            rec_ref[0, :, ls] = (_rms(o_sc[:, ls], ng_ref[...]) * gate_sc[:, ls]).astype(rec_ref.dtype)

    v = {}

    def fox_forget():
        z = proj(3 * fw, LANES) + fb_ref[...]
        logf = jnp.minimum(z, 0.0) - jnp.log(1.0 + jnp.exp(-jnp.abs(z)))
        logf = jnp.where(lane < FOX_HEADS, logf, 0.0)
        fcum = _tri_cumsum(tri_ref[...], logf) + carry_ref[0:1, :]
        carry_ref[...] = jnp.broadcast_to(fcum[tm - 1:tm, :], carry_ref.shape)
        v["fsc"] = fcum * LOG2E

    def fox_qk():
        q_scale = LOG2E / math.sqrt(FOX_HEAD_DIM)
        v["qk"] = jnp.concatenate([(proj(0, fw) * q_scale).astype(BF16), proj(fw, fw).astype(BF16)],
                                  axis=1)

    def fox_bounds():
        qkf = v["qk"].astype(F32)
        norm2 = _dot((qkf * qkf).astype(BF16), hsum_ref[...]) * NORM_HEADROOM
        norm = jnp.sqrt(norm2)
        to_q_lanes = lambda a: pltpu.roll(a, LANES - FOX_HEADS, 1)
        k_run = jnp.maximum(kmax_ref[...], jnp.max(norm, axis=0, keepdims=True))
        kmax_ref[...] = k_run
        k_run_q = to_q_lanes(k_run)[0:1, :]
        v["shift"] = jnp.where(lane < FOX_HEADS, norm * k_run_q - FOX_SHIFT_MARGIN, 0.0)
        v["spread"] = norm * (k_run_q + to_q_lanes(norm))
        v["norm2"] = norm2

    def fox_operands():
        parts = lambda a: [p.astype(F32) for p in _split3(a)]
        f_parts, s_parts = parts(v["fsc"]), parts(-v["shift"])
        fcat = jnp.where(lane == FCAT_ONE_LANE, 1.0, 0.0)
        for j in range(3):
            fcat = fcat + (pltpu.roll(f_parts[j], j * FOX_HEADS, 1) if j else f_parts[j])
            fcat = fcat + pltpu.roll(s_parts[j], FCAT_SHIFT_LANE + j * FOX_HEADS, 1)
        aug = _dot(fcat.astype(BF16), sel_ref[...]).astype(BF16)
        qk, pairs = v["qk"], FOX_HEADS // 2
        for p in range(pairs):
            q_ref[0, p, :, 0:LANES] = qk[:, p * LANES:(p + 1) * LANES]
            q_ref[0, p, :, LANES:PAIR_LANES] = aug[:, p * LANES:(p + 1) * LANES]
            k_ref[0, p, :, 0:LANES] = qk[:, (pairs + p) * LANES:(pairs + p + 1) * LANES]
            k_ref[0, p, :, LANES:PAIR_LANES] = aug[:, (pairs + p) * LANES:(pairs + p + 1) * LANES]

    def fox_values():
        vt_ref[0] = _dot_nt(wt_ref[2 * fw:3 * fw, :].astype(BF16), hb_sc[...]).astype(BF16)

    def fox_stats():
        for sb in range(tm // STAT_BLOCK):
            rows = slice(sb * STAT_BLOCK, (sb + 1) * STAT_BLOCK)
            stat_ref[0, sb, 0:1, :] = jnp.max(v["norm2"][rows, :], axis=0, keepdims=True)
            stat_ref[0, sb, 1:2, :] = v["fsc"][rows, :][0:1, :]
            stat_ref[0, sb, 2:3, :] = v["fsc"][rows, :][STAT_BLOCK - 1:STAT_BLOCK, :]
            stat_ref[0, sb, 3:4, :] = jnp.max(v["spread"][rows, :], axis=0, keepdims=True)
            stat_ref[0, sb, 4:8, :] = jnp.zeros((4, LANES), F32)

    fox_stages = [fox_forget, fox_qk, fox_bounds, fox_operands, fox_values, fox_stats]

    hgrn_stages = _hgrn_chunk_stages(hq_sc, hk_sc, hi_sc, st_sc, b_sc, o_sc, tm) + [hgrn_output]
    for a, b in zip(hgrn_stages, fox_stages):
        a()
        b()

    @pl.when(jnp.logical_not(chunk_ok))
    def _():
        st_sc[...] = st_prev_sc[...]
        for r0 in range(0, tm, cs_rows):
            b_sc[r0:r0 + cs_rows, :] = _tri_cumsum(tri16_ref[...], hg_sc[r0:r0 + cs_rows, :])
        _hgrn_step_path(hq_sc, hk_sc, hi_sc, st_sc, b_sc, o_sc, tm)
        hgrn_output()


def _front(x, norm_g, w_in, fox_f_bias, lb_logits, hgrn_norm_g, layer, tm):
    b, s, d = x.shape
    fw, hw = FOX_WIDTH, HGRN_WIDTH
    assert w_in.shape == (d, 3 * fw + FOX_HEADS + 4 * hw)
    wt = w_in.T
    fb = jnp.pad(fox_f_bias.reshape(1, FOX_HEADS), ((0, 0), (0, LANES - FOX_HEADS)))

    sel = np.zeros((LANES, 2 * fw), np.float32)
    for h in range(FOX_HEADS):
        col = h * FOX_HEAD_DIM
        for j in range(3):
            sel[j * FOX_HEADS + h, col + AUG_F + j] = 1.0
            sel[FCAT_ONE_LANE, col + AUG_ONE + j] = 1.0
            sel[FCAT_ONE_LANE, fw + col + AUG_F + j] = 1.0
            sel[j * FOX_HEADS + h, fw + col + AUG_ONE + j] = -1.0
            sel[FCAT_SHIFT_LANE + j * FOX_HEADS + h, col + AUG_SHIFT + j] = 1.0
            sel[FCAT_ONE_LANE, fw + col + AUG_SHIFT + j] = 1.0
    sel = jnp.asarray(sel, BF16)
    tri = jnp.asarray(np.tril(np.ones((tm, tm), np.float32)), BF16)
    hsum = np.zeros((2 * fw, LANES), np.float32)
    hsum[np.arange(2 * fw), np.arange(2 * fw) // FOX_HEAD_DIM] = 1.0
    hsum = jnp.asarray(hsum, BF16)

    cs_rows = min(tm, 256)
    tri64, tri16 = _block_tri(cs_rows, HGRN_CHUNK), _block_tri(cs_rows, HGRN_STEP)

    const = _resident
    pairs = FOX_HEADS // 2
    head_out = jax.ShapeDtypeStruct((b, pairs, s, PAIR_LANES), BF16)
    head_spec = pl.BlockSpec((1, pairs, tm, PAIR_LANES), lambda bi, ti: (bi, 0, ti, 0))
    vm = pltpu.VMEM
    return pl.pallas_call(
        functools.partial(_front_kernel, layer=layer),
        grid=(b, s // tm),
        in_specs=[pl.BlockSpec((1, tm, d), lambda bi, ti: (bi, ti, 0)),
                  const((1, d)), const(wt.shape), const(fb.shape), const(lb_logits.shape), const(sel.shape),
                  const(tri.shape), const(hsum.shape), const((1, HGRN_DIM)), const(tri64.shape), const(tri16.shape)],
        out_specs=[head_spec, head_spec, pl.BlockSpec((1, fw, tm), lambda bi, ti: (bi, 0, ti)),
                   pl.BlockSpec((1, tm // STAT_BLOCK, 8, LANES), lambda bi, ti: (bi, ti, 0, 0)),
                   pl.BlockSpec((1, tm, hw), lambda bi, ti: (bi, ti, 0))],
        out_shape=[head_out, head_out, jax.ShapeDtypeStruct((b, fw, s), BF16),
                   jax.ShapeDtypeStruct((b, s // STAT_BLOCK, 8, LANES), F32),
                   jax.ShapeDtypeStruct((b, s, hw), BF16)],
        scratch_shapes=[vm((8, LANES), F32), vm((8, LANES), F32), vm((tm, d), BF16),
                        vm((1, tm, hw), F32), vm((1, tm, hw), F32), vm((tm, hw), F32),
                        vm((1, tm, hw), BF16), vm((tm, hw), BF16),
                        vm((HGRN_HEADS, HGRN_DIM, HGRN_DIM), F32), vm((HGRN_HEADS, HGRN_DIM, HGRN_DIM), F32),
                        vm((tm, hw), F32), vm((tm, hw), F32)],
        compiler_params=pltpu.CompilerParams(dimension_semantics=("arbitrary", "arbitrary"),
                                             vmem_limit_bytes=VMEM_LIMIT),
        name="front",
    )(x, norm_g.reshape(1, d), wt, fb, lb_logits, sel, tri, hsum, hgrn_norm_g.reshape(1, HGRN_DIM), tri64, tri16)


def _fox_first_block(stats, tq):
    b, nsb = stats.shape[0], stats.shape[1]
    assert tq % STAT_BLOCK == 0, (tq, STAT_BLOCK)
    r = tq // STAT_BLOCK
    st = stats.reshape(b, nsb // r, r, 8, LANES)
    h = FOX_HEADS
    qn = jnp.sqrt(jnp.max(st[:, :, :, 0, 0:h], axis=2))
    kn = jnp.sqrt(jnp.max(st[:, :, :, 0, h:2 * h], axis=2))
    f_first, f_last = st[:, :, 0, 1, 0:h], st[:, :, r - 1, 2, 0:h]
    bound = (qn[:, :, None, :] * (kn[:, None, :, :] + kn[:, :, None, :])
             + f_first[:, :, None, :] - f_last[:, None, :, :])
    blk = jnp.arange(nsb // r)
    need = jnp.any(jnp.logical_not(bound <= -FOX_SKIP_LOG2), axis=-1) & (blk[None, None, :] < blk[None, :, None])
    first = jnp.min(jnp.where(need, blk[None, None, :], nsb // r), axis=-1)
    return jnp.minimum(first, blk[None, :]).astype(jnp.int32)


def _fox_direct_ok(stats, tq):
    b, nsb = stats.shape[0], stats.shape[1]
    spread = jnp.max(stats[:, :, 3, 0:FOX_HEADS].reshape(b, nsb * STAT_BLOCK // tq, -1), axis=-1)
    return (spread <= FOX_SPREAD_MAX).astype(jnp.int32)


def _fox_kernel(first_ref, direct_ref, q_ref, k_ref, vt_ref, o_ref, qm_sc, m_sc, l_sc, acc_sc, *, tq, hg):
    i = pl.program_id(2)
    hd = FOX_HEAD_DIM
    l_sc[...] = jnp.zeros_like(l_sc)
    acc_sc[...] = jnp.zeros_like(acc_sc)
    lane = lax.broadcasted_iota(jnp.int32, (tq, PAIR_LANES), 1)
    for hh in range(hg):
        own = (lane // hd) % 2 == hh % 2
        qm_sc[hh] = jnp.where(own, q_ref[0, hh // 2], jnp.zeros((), BF16))

    def accumulate_online(hh, st, off):
        m_prev = m_sc[hh]
        m_new = jnp.maximum(m_prev, jnp.max(st, axis=0, keepdims=True))
        alpha = jnp.exp2(m_prev - m_new)
        p = jnp.exp2(st - m_new)
        l_sc[hh] = alpha * l_sc[hh] + jnp.sum(p, axis=0, keepdims=True)
        vt = vt_ref[0, hh * hd:(hh + 1) * hd, pl.ds(off, tq)]
        acc_sc[hh] = alpha * acc_sc[hh] + _dot(vt, p.astype(BF16))
        m_sc[hh] = m_new

    def accumulate_direct(hh, st, off):
        p = jnp.exp2(st)
        l_sc[hh] += jnp.sum(p, axis=0, keepdims=True)
        acc_sc[hh] += _dot(vt_ref[0, hh * hd:(hh + 1) * hd, pl.ds(off, tq)], p.astype(BF16))

    def sweep(accumulate):
        def block(j, masked):
            off = pl.multiple_of(j * tq, tq)
            scores = lambda h_: _dot_nt(k_ref[0, h_ // 2, pl.ds(off, tq), :], qm_sc[h_])
            ahead = 2
            pending = [scores(h_) for h_ in range(min(ahead, hg))]
            for hh in range(hg):
                st = pending.pop(0)
                if hh + ahead < hg:
                    pending.append(scores(hh + ahead))
                if masked:
                    key = lax.broadcasted_iota(jnp.int32, st.shape, 0)
                    qry = lax.broadcasted_iota(jnp.int32, st.shape, 1)
                    st = jnp.where(key <= qry, st, -jnp.inf)
                accumulate(hh, st, off)

        def body(j, carry):
            block(j, False)
            return carry

        lax.fori_loop(first_ref[pl.program_id(0), i], i, body, 0)
        block(i, True)

    direct = direct_ref[pl.program_id(0), i] != 0

    @pl.when(direct)
    def _():
        sweep(accumulate_direct)

    @pl.when(jnp.logical_not(direct))
    def _():
        m_sc[...] = jnp.full_like(m_sc, -jnp.inf)
        sweep(accumulate_online)

    for hh in range(hg):
        o_ref[0, hh * hd:(hh + 1) * hd, :] = (acc_sc[hh] / l_sc[hh]).astype(o_ref.dtype)


def _fox_attention(q, k, vt, first_block, direct_ok, tq, hg):
    b, pairs, s, _ = q.shape
    assert hg % 2 == 0
    groups = 2 * pairs // hg
    rows = hg * FOX_HEAD_DIM
    grid_spec = pltpu.PrefetchScalarGridSpec(
        num_scalar_prefetch=2,
        grid=(b, groups, s // tq),
        in_specs=[pl.BlockSpec((1, hg // 2, tq, PAIR_LANES), lambda bi, pi, qi, *_: (bi, pi, qi, 0)),
                  pl.BlockSpec((1, hg // 2, s, PAIR_LANES), lambda bi, pi, qi, *_: (bi, pi, 0, 0)),
                  pl.BlockSpec((1, rows, s), lambda bi, pi, qi, *_: (bi, pi, 0))],
        out_specs=pl.BlockSpec((1, rows, tq), lambda bi, pi, qi, *_: (bi, pi, qi)),
        scratch_shapes=[pltpu.VMEM((hg, tq, PAIR_LANES), BF16),
                        pltpu.VMEM((hg, 1, tq), F32), pltpu.VMEM((hg, 1, tq), F32),
                        pltpu.VMEM((hg, FOX_HEAD_DIM, tq), F32)])
    return pl.pallas_call(
        functools.partial(_fox_kernel, tq=tq, hg=hg),
        grid_spec=grid_spec,
        out_shape=jax.ShapeDtypeStruct((b, groups * rows, s), BF16),
        compiler_params=pltpu.CompilerParams(dimension_semantics=("arbitrary",) * 3,
                                             vmem_limit_bytes=VMEM_LIMIT),
        name="fox_attention",
    )(first_block, direct_ok, q, k, vt)


def _hgrn_chunk_stages(q_ref, k_ref, i_ref, st_sc, b_sc, o_sc, tc):
    c_len = HGRN_CHUNK
    n_chunks = tc // c_len
    t_idx = lax.broadcasted_iota(jnp.int32, (c_len, c_len), 0)
    s_idx = lax.broadcasted_iota(jnp.int32, (c_len, c_len), 1)
    causal = s_idx <= t_idx
    units = [(c, h) for c in range(n_chunks) for h in range(HGRN_HEADS)]
    rows = lambda c: slice(c * c_len, (c + 1) * c_len)
    lanes = lambda h: slice(h * HGRN_DIM, (h + 1) * HGRN_DIM)
    qe, ke, kl, decay, attn, intra, d_state, state = ({} for _ in range(8))

    def decay_operands():
        for c, h in units:
            b = b_sc[rows(c), lanes(h)]
            kk = k_ref[0, rows(c), lanes(h)]
            b_last = b[c_len - 1:c_len, :]
            qe[c, h] = (q_ref[0, rows(c), lanes(h)] * jnp.exp(b)).astype(BF16)
            ke[c, h] = (kk * jnp.exp(-b)).astype(BF16)
            kl[c, h] = (kk * jnp.exp(b_last - b)).astype(BF16)
            decay[c, h] = jnp.exp(b_last)

    def scores():
        for u in units:
            attn[u] = jnp.where(causal, _dot_nt(qe[u], ke[u]), 0.0).astype(BF16)

    def products():
        for c, h in units:
            intra[c, h] = _dot(attn[c, h], i_ref[0, rows(c), lanes(h)])
            d_state[c, h] = _dot_tn(i_ref[0, rows(c), lanes(h)], kl[c, h])

    def recurrence():
        for h in range(HGRN_HEADS):
            st = st_sc[h]
            for c in range(n_chunks):
                state[c, h] = st.astype(BF16)
                st = st * decay[c, h] + d_state[c, h]
            st_sc[h] = st

    def outputs():
        for c, h in units:
            o_sc[rows(c), lanes(h)] = intra[c, h] + _dot_nt(qe[c, h], state[c, h])

    return [decay_operands, scores, products, recurrence, outputs]


def _hgrn_step_path(q_ref, k_ref, i_ref, st_sc, b_sc, o_sc, tc):
    row8 = lax.broadcasted_iota(jnp.int32, (8, HGRN_DIM), 0)
    half = HGRN_STEP // 2

    def step(u, carry):
        base = pl.multiple_of(u * HGRN_STEP, HGRN_STEP)
        for h in range(HGRN_HEADS):
            ls = slice(h * HGRN_DIM, (h + 1) * HGRN_DIM)
            rows = pl.ds(base, HGRN_STEP)
            bq = b_sc[rows, ls]
            qq = q_ref[0, rows, ls]
            kk = k_ref[0, rows, ls]
            ii = i_ref[0, rows, ls].astype(F32)
            b_last = bq[HGRN_STEP - 1:HGRN_STEP, :]
            st = st_sc[h]
            inter = _dot_nt((qq * jnp.exp(bq)).astype(BF16), st.astype(BF16))
            k_dec = (kk * jnp.exp(b_last - bq)).astype(BF16)
            st_sc[h] = st * jnp.exp(b_last) + _dot_tn(ii.astype(BF16), k_dec)
            out = [inter[0:half], inter[half:HGRN_STEP]]
            qv = [qq[0:half], qq[half:HGRN_STEP]]
            bv = [bq[0:half], bq[half:HGRN_STEP]]
            for s_ in range(HGRN_STEP):
                k_s, b_s, i_s = kk[s_:s_ + 1, :], bq[s_:s_ + 1, :], ii[s_:s_ + 1, :]
                for v_ in range(s_ // half, 2):
                    diff = bv[v_] - b_s
                    if s_ > v_ * half:
                        diff = jnp.where(row8 + v_ * half >= s_, diff, -jnp.inf)
                    a = jnp.sum(qv[v_] * k_s * jnp.exp(diff), axis=-1, keepdims=True)
                    out[v_] = out[v_] + a * i_s
            o_sc[pl.ds(base, half), ls] = out[0]
            o_sc[pl.ds(base + half, half), ls] = out[1]
        return carry

    lax.fori_loop(0, tc // HGRN_STEP, step, 0)


def _block_tri(n, blk):
    return jnp.asarray(np.kron(np.eye(n // blk, dtype=np.float32), np.tril(np.ones((blk, blk), np.float32))), BF16)


def _memkv_kernel(mem_ref, g_ref, wkv_ref, wq_ref, wo_ref, sm_ref, om_ref):
    d = mem_ref.shape[-1]
    m = mem_ref.shape[1]
    hd = d // X_HEADS
    inv = 1.0 / math.sqrt(hd)
    kv = _dot(_rms(mem_ref[0], g_ref[...]).astype(BF16), wkv_ref[...].astype(BF16))
    for h in range(X_HEADS):
        cs = slice(h * hd, (h + 1) * hd)
        k_h = kv[:, cs].astype(BF16)
        v_h = kv[:, d + h * hd:d + (h + 1) * hd].astype(BF16)
        sm_ref[0, :, h * m:(h + 1) * m] = (_dot_nt(wq_ref[:, cs].astype(BF16), k_h) * inv).astype(BF16)
        om_ref[0, h * m:(h + 1) * m, :] = _dot(v_h, wo_ref[cs, :].astype(BF16)).astype(BF16)


def _mem_kv(mem, norm_g, w_kv, w_xq, w_xo):
    b, m, d = mem.shape
    return pl.pallas_call(
        _memkv_kernel,
        grid=(b,),
        in_specs=[pl.BlockSpec((1, m, d), lambda bi: (bi, 0, 0)), _resident((1, d)), _resident((d, 2 * d)),
                  _resident((d, d)), _resident((d, d))],
        out_specs=[pl.BlockSpec((1, d, X_HEADS * m), lambda bi: (bi, 0, 0)),
                   pl.BlockSpec((1, X_HEADS * m, d), lambda bi: (bi, 0, 0))],
        out_shape=[jax.ShapeDtypeStruct((b, d, X_HEADS * m), BF16), jax.ShapeDtypeStruct((b, X_HEADS * m, d), BF16)],
        compiler_params=pltpu.CompilerParams(dimension_semantics=("arbitrary",), vmem_limit_bytes=VMEM_LIMIT),
        name="mem_kv",
    )(mem, norm_g.reshape(1, d), w_kv, w_xq, w_xo)


def _mix_kernel(x_ref, foxt_ref, rec_ref, wo_ref, gx_ref, sm_ref, om_ref, o_ref, h1_sc, p_sc, *, sub):
    fw = foxt_ref.shape[1]
    n_mem = sm_ref.shape[2] // X_HEADS
    n_sub = x_ref.shape[1] // sub

    def out_proj(i):
        r = slice(i * sub, (i + 1) * sub)
        h1_sc[r, :] = (x_ref[0, r, :] + _dot_tn(foxt_ref[0, :, r], wo_ref[0:fw, :].astype(BF16))
                       + _dot(rec_ref[0, r, :], wo_ref[fw:, :].astype(BF16)))

    def attend(i):
        r = slice(i * sub, (i + 1) * sub)
        s = _dot(_rms(h1_sc[r, :], gx_ref[...]).astype(BF16), sm_ref[0])
        for h in range(X_HEADS):
            cs = slice(h * n_mem, (h + 1) * n_mem)
            p = jnp.exp(s[:, cs] - jnp.max(s[:, cs], axis=-1, keepdims=True))
            p_sc[r, cs] = (p / jnp.sum(p, axis=-1, keepdims=True)).astype(BF16)

    def finish(i):
        r = slice(i * sub, (i + 1) * sub)
        o_ref[0, r, :] = h1_sc[r, :] + _dot(p_sc[r, :], om_ref[0])

    stages = (out_proj, attend, finish)
    for t in range(n_sub + len(stages) - 1):
        for k, stage in enumerate(stages):
            if 0 <= t - k < n_sub:
                stage(t - k)


def _mix(x, foxt, rec, w_out, norm_g, score_m, out_m, tm):
    b, s, d = x.shape
    fw = foxt.shape[1]
    hm = score_m.shape[2]
    tok = lambda wd: pl.BlockSpec((1, tm, wd), lambda bi, ti: (bi, ti, 0))
    return pl.pallas_call(
        functools.partial(_mix_kernel, sub=min(tm, 256)),
        grid=(b, s // tm),
        in_specs=[tok(d), pl.BlockSpec((1, fw, tm), lambda bi, ti: (bi, 0, ti)), tok(rec.shape[-1]),
                  _resident((d, d)), _resident((1, d)),
                  pl.BlockSpec((1, d, hm), lambda bi, ti: (bi, 0, 0)), pl.BlockSpec((1, hm, d), lambda bi, ti: (bi, 0, 0))],
        out_specs=tok(d),
        out_shape=jax.ShapeDtypeStruct((b, s, d), F32),
        scratch_shapes=[pltpu.VMEM((tm, d), F32), pltpu.VMEM((tm, hm), BF16)],
        compiler_params=pltpu.CompilerParams(dimension_semantics=("arbitrary", "arbitrary"),
                                             vmem_limit_bytes=VMEM_LIMIT),
        name="mix",
    )(x, foxt, rec, w_out, norm_g.reshape(1, d), score_m, out_m)


def _mlp_kernel(h_ref, g_ref, w1_ref, w2_ref, gf_ref, o_ref, acc_sc, *, fc):
    hn = _rms(h_ref[...], g_ref[...]).astype(BF16)
    n_chunks = w1_ref.shape[1] // fc
    up = lambda c: _dot(hn, w1_ref[:, c * fc:(c + 1) * fc].astype(BF16))
    u_next = up(0)
    for c in range(n_chunks):
        u = jnp.maximum(u_next, 0.0)
        if c + 1 < n_chunks:
            u_next = up(c + 1)
        part = _dot((u * u).astype(BF16), w2_ref[c * fc:(c + 1) * fc, :].astype(BF16))
        if c == 0:
            acc_sc[...] = part
        else:
            acc_sc[...] += part
    o_ref[...] = _rms(h_ref[...] + acc_sc[...], gf_ref[...])


def _mlp(h, norm_g, w1, w2, final_g, tm, fc):
    t, d = h.shape
    dff = w1.shape[1]
    return pl.pallas_call(
        functools.partial(_mlp_kernel, fc=fc),
        grid=(t // tm,),
        in_specs=[pl.BlockSpec((tm, d), lambda ti: (ti, 0)), _resident((1, d)),
                  _resident((d, dff)), _resident((dff, d)), _resident((1, d))],
        out_specs=pl.BlockSpec((tm, d), lambda ti: (ti, 0)),
        out_shape=jax.ShapeDtypeStruct((t, d), F32),
        scratch_shapes=[pltpu.VMEM((tm, d), F32)],
        compiler_params=pltpu.CompilerParams(dimension_semantics=("arbitrary",), vmem_limit_bytes=VMEM_LIMIT),
        name="mlp",
    )(h, norm_g.reshape(1, d), w1, w2, final_g.reshape(1, d))


def _tile(n, want):
    t = min(n, want)
    assert n % t == 0, (n, want)
    return t


def _tiles(b, s, d_ff):
    return dict(
        front=_tile(s, 512),
        fox_q=_tile(s, 512),
        fox_heads=FOX_HEADS,
        mix=_tile(s, 1024),
        mlp=_tile(b * s, 512),
        mlp_ff=_tile(d_ff, 1024),
    )


def kernel(x, mem, norm_mix_g, w_in, fox_f_bias, hgrn_lb_logits, hgrn_norm_g, w_out, norm_x_g, norm_mem_g,
           w_xq, w_xkv, w_xo, norm_ff_g, w1, w2, final_norm_g):
    b, s, d = x.shape
    t = _tiles(b, s, w1.shape[-1])
    h = x
    for l in range(w_in.shape[0]):
        q, k, v, stats, rec = _front(h, norm_mix_g[l], w_in[l], fox_f_bias[l], hgrn_lb_logits, hgrn_norm_g[l], l,
                                     t["front"])
        fox = _fox_attention(q, k, v, _fox_first_block(stats, t["fox_q"]), _fox_direct_ok(stats, t["fox_q"]),
                             t["fox_q"], t["fox_heads"])
        score_m, out_m = _mem_kv(mem, norm_mem_g[l], w_xkv[l], w_xq[l], w_xo[l])
        h = _mix(h, fox, rec, w_out[l], norm_x_g[l], score_m, out_m, t["mix"])
        is_last = l == w_in.shape[0] - 1
        assert is_last, "the MLP kernel fuses the final norm, so it must be the last layer"
        h = _mlp(h.reshape(b * s, d), norm_ff_g[l], w1[l], w2[l], final_norm_g, t["mlp"], t["mlp_ff"]).reshape(b, s, d)
    return h
```

```python
import functools
import math

import jax
import jax.numpy as jnp
import numpy as np
from jax import lax
from jax.experimental import pallas as pl
from jax.experimental.pallas import tpu as pltpu

EPS = 1e-6
LOG2E = math.log2(math.e)
LANES = 128
FOX_HEADS = 8
FOX_HEAD_DIM = 64
FOX_WIDTH = FOX_HEADS * FOX_HEAD_DIM
HGRN_HEADS = 4
HGRN_DIM = 128
HGRN_WIDTH = HGRN_HEADS * HGRN_DIM
X_HEADS = 4
HGRN_STEP = 16
HGRN_CHUNK = 64
HGRN_MIN_CHUNK_LOG_DECAY = -60.0
VMEM_LIMIT = 56 * 1024 * 1024

PAIR_LANES = 2 * LANES
AUG_F = 0
AUG_ONE = 3
AUG_SHIFT = 6
FCAT_ONE_LANE = 24
FCAT_SHIFT_LANE = 32
NORM_HEADROOM = 1.0 + 2.0 ** -7
FOX_SHIFT_MARGIN = 60.0
FOX_SPREAD_MAX = 120.0
STAT_BLOCK = 256
FOX_SKIP_LOG2 = 100.0

BF16 = jnp.bfloat16
F32 = jnp.float32


def _dot(a, b):
    return jnp.dot(a, b, preferred_element_type=F32)


def _dot_nt(a, b):
    return lax.dot_general(a, b, (((1,), (1,)), ((), ())), preferred_element_type=F32)


def _dot_tn(a, b):
    return lax.dot_general(a, b, (((0,), (0,)), ((), ())), preferred_element_type=F32)


def _split3(v):
    hi = v.astype(BF16)
    r1 = v - hi.astype(F32)
    mid = r1.astype(BF16)
    lo = (r1 - mid.astype(F32)).astype(BF16)
    return hi, mid, lo


def _tri_cumsum(tri, v):
    hi, mid, lo = _split3(v)
    return _dot(tri, hi) + _dot(tri, mid) + _dot(tri, lo)


def _rms(x, g):
    ms = jnp.mean(x * x, axis=-1, keepdims=True)
    return x * lax.rsqrt(ms + EPS) * g


def _resident(shape):
    return pl.BlockSpec(shape, lambda *_: (0,) * len(shape), pipeline_mode=pl.Buffered(1))


def _front_kernel(x_ref, g_ref, wt_ref, fb_ref, lbl_ref, sel_ref, tri_ref, hsum_ref, ng_ref, tri64_ref, tri16_ref,
                  q_ref, k_ref, vt_ref, stat_ref, rec_ref,
                  carry_ref, kmax_ref, hb_sc, hq_sc, hk_sc, hg_sc, hi_sc, gate_sc, st_sc, st_prev_sc, b_sc, o_sc,
                  *, layer):
    @pl.when(pl.program_id(1) == 0)
    def _():
        carry_ref[...] = jnp.zeros_like(carry_ref)
        kmax_ref[...] = jnp.zeros_like(kmax_ref)
        st_sc[...] = jnp.zeros_like(st_sc)

    tm = x_ref.shape[1]
    hb_sc[...] = _rms(x_ref[0], g_ref[...]).astype(BF16)
    lane = lax.broadcasted_iota(jnp.int32, (tm, LANES), 1)
    fw, w = FOX_WIDTH, HGRN_WIDTH
    proj = lambda r0, n: _dot_nt(hb_sc[...], wt_ref[r0:r0 + n, :].astype(BF16))

    lbl = lbl_ref[...]
    e = jnp.exp(lbl - jnp.max(lbl, axis=0, keepdims=True))
    lb = jnp.sum(e[0:layer + 1, :], axis=0, keepdims=True) / jnp.sum(e, axis=0, keepdims=True)
    seg = lambda n: proj(3 * fw + FOX_HEADS + n * w, w)
    gf = seg(1)
    gq = seg(0)
    f = lb + (1.0 - lb) * jax.nn.sigmoid(gf)
    hk_sc[0] = 1.0 - f
    hg_sc[...] = jnp.log(f)
    gi = seg(2)
    hq_sc[0] = gq * jax.nn.sigmoid(gq)
    gg = seg(3)
    cs_rows = tri64_ref.shape[0]
    for r0 in range(0, tm, cs_rows):
        b_sc[r0:r0 + cs_rows, :] = _tri_cumsum(tri64_ref[...], hg_sc[r0:r0 + cs_rows, :])
    hi_sc[0] = gi.astype(BF16)
    gate_sc[...] = (gg * jax.nn.sigmoid(gg)).astype(BF16)
    chunk_ok = jnp.min(b_sc[...]) >= HGRN_MIN_CHUNK_LOG_DECAY
    st_prev_sc[...] = st_sc[...]

    def hgrn_output():
        for h in range(HGRN_HEADS):
            ls = slice(h * HGRN_DIM, (h + 1) * HGRN_DIM)
            rec_ref[0, :, ls] = (_rms(o_sc[:, ls], ng_ref[...]) * gate_sc[:, ls]).astype(rec_ref.dtype)

    v = {}

    def fox_forget():
        z = proj(3 * fw, LANES) + fb_ref[...]
        logf = jnp.minimum(z, 0.0) - jnp.log(1.0 + jnp.exp(-jnp.abs(z)))
        logf = jnp.where(lane < FOX_HEADS, logf, 0.0)
        fcum = _tri_cumsum(tri_ref[...], logf) + carry_ref[0:1, :]
        carry_ref[...] = jnp.broadcast_to(fcum[tm - 1:tm, :], carry_ref.shape)
        v["fsc"] = fcum * LOG2E

    def fox_qk():
        q_scale = LOG2E / math.sqrt(FOX_HEAD_DIM)
        v["qk"] = jnp.concatenate([(proj(0, fw) * q_scale).astype(BF16), proj(fw, fw).astype(BF16)],
                                  axis=1)

    def fox_bounds():
        qkf = v["qk"].astype(F32)
        norm2 = _dot((qkf * qkf).astype(BF16), hsum_ref[...]) * NORM_HEADROOM
        norm = jnp.sqrt(norm2)
        to_q_lanes = lambda a: pltpu.roll(a, LANES - FOX_HEADS, 1)
        k_run = jnp.maximum(kmax_ref[...], jnp.max(norm, axis=0, keepdims=True))
        kmax_ref[...] = k_run
        k_run_q = to_q_lanes(k_run)[0:1, :]
        v["shift"] = jnp.where(lane < FOX_HEADS, norm * k_run_q - FOX_SHIFT_MARGIN, 0.0)
        v["spread"] = norm * (k_run_q + to_q_lanes(norm))
        v["norm2"] = norm2

    def fox_operands():
        parts = lambda a: [p.astype(F32) for p in _split3(a)]
        f_parts, s_parts = parts(v["fsc"]), parts(-v["shift"])
        fcat = jnp.where(lane == FCAT_ONE_LANE, 1.0, 0.0)
        for j in range(3):
            fcat = fcat + (pltpu.roll(f_parts[j], j * FOX_HEADS, 1) if j else f_parts[j])
            fcat = fcat + pltpu.roll(s_parts[j], FCAT_SHIFT_LANE + j * FOX_HEADS, 1)
        aug = _dot(fcat.astype(BF16), sel_ref[...]).astype(BF16)
        qk, pairs = v["qk"], FOX_HEADS // 2
        for p in range(pairs):
            q_ref[0, p, :, 0:LANES] = qk[:, p * LANES:(p + 1) * LANES]
            q_ref[0, p, :, LANES:PAIR_LANES] = aug[:, p * LANES:(p + 1) * LANES]
            k_ref[0, p, :, 0:LANES] = qk[:, (pairs + p) * LANES:(pairs + p + 1) * LANES]
            k_ref[0, p, :, LANES:PAIR_LANES] = aug[:, (pairs + p) * LANES:(pairs + p + 1) * LANES]

    def fox_values():
        vt_ref[0] = _dot_nt(wt_ref[2 * fw:3 * fw, :].astype(BF16), hb_sc[...]).astype(BF16)

    def fox_stats():
        for sb in range(tm // STAT_BLOCK):
            rows = slice(sb * STAT_BLOCK, (sb + 1) * STAT_BLOCK)
            stat_ref[0, sb, 0:1, :] = jnp.max(v["norm2"][rows, :], axis=0, keepdims=True)
            stat_ref[0, sb, 1:2, :] = v["fsc"][rows, :][0:1, :]
            stat_ref[0, sb, 2:3, :] = v["fsc"][rows, :][STAT_BLOCK - 1:STAT_BLOCK, :]
            stat_ref[0, sb, 3:4, :] = jnp.max(v["spread"][rows, :], axis=0, keepdims=True)
            stat_ref[0, sb, 4:8, :] = jnp.zeros((4, LANES), F32)

    fox_stages = [fox_forget, fox_qk, fox_bounds, fox_operands, fox_values, fox_stats]

    hgrn_stages = _hgrn_chunk_stages(hq_sc, hk_sc, hi_sc, st_sc, b_sc, o_sc, tm) + [hgrn_output]
    for a, b in zip(hgrn_stages, fox_stages):
        a()
        b()

    @pl.when(jnp.logical_not(chunk_ok))
    def _():
        st_sc[...] = st_prev_sc[...]
        for r0 in range(0, tm, cs_rows):
            b_sc[r0:r0 + cs_rows, :] = _tri_cumsum(tri16_ref[...], hg_sc[r0:r0 + cs_rows, :])
        _hgrn_step_path(hq_sc, hk_sc, hi_sc, st_sc, b_sc, o_sc, tm)
        hgrn_output()


def _front(x, norm_g, w_in, fox_f_bias, lb_logits, hgrn_norm_g, layer, tm):
    b, s, d = x.shape
    fw, hw = FOX_WIDTH, HGRN_WIDTH
    assert w_in.shape == (d, 3 * fw + FOX_HEADS + 4 * hw)
    wt = w_in.T
    fb = jnp.pad(fox_f_bias.reshape(1, FOX_HEADS), ((0, 0), (0, LANES - FOX_HEADS)))

    sel = np.zeros((LANES, 2 * fw), np.float32)
    for h in range(FOX_HEADS):
        col = h * FOX_HEAD_DIM
        for j in range(3):
            sel[j * FOX_HEADS + h, col + AUG_F + j] = 1.0
            sel[FCAT_ONE_LANE, col + AUG_ONE + j] = 1.0
            sel[FCAT_ONE_LANE, fw + col + AUG_F + j] = 1.0
            sel[j * FOX_HEADS + h, fw + col + AUG_ONE + j] = -1.0
            sel[FCAT_SHIFT_LANE + j * FOX_HEADS + h, col + AUG_SHIFT + j] = 1.0
            sel[FCAT_ONE_LANE, fw + col + AUG_SHIFT + j] = 1.0
    sel = jnp.asarray(sel, BF16)
    tri = jnp.asarray(np.tril(np.ones((tm, tm), np.float32)), BF16)
    hsum = np.zeros((2 * fw, LANES), np.float32)
    hsum[np.arange(2 * fw), np.arange(2 * fw) // FOX_HEAD_DIM] = 1.0
    hsum = jnp.asarray(hsum, BF16)

    cs_rows = min(tm, 256)
    tri64, tri16 = _block_tri(cs_rows, HGRN_CHUNK), _block_tri(cs_rows, HGRN_STEP)

    const = _resident
    pairs = FOX_HEADS // 2
    head_out = jax.ShapeDtypeStruct((b, pairs, s, PAIR_LANES), BF16)
    head_spec = pl.BlockSpec((1, pairs, tm, PAIR_LANES), lambda bi, ti: (bi, 0, ti, 0))
    vm = pltpu.VMEM
    return pl.pallas_call(
        functools.partial(_front_kernel, layer=layer),
        grid=(b, s // tm),
        in_specs=[pl.BlockSpec((1, tm, d), lambda bi, ti: (bi, ti, 0)),
                  const((1, d)), const(wt.shape), const(fb.shape), const(lb_logits.shape), const(sel.shape),
                  const(tri.shape), const(hsum.shape), const((1, HGRN_DIM)), const(tri64.shape), const(tri16.shape)],
        out_specs=[head_spec, head_spec, pl.BlockSpec((1, fw, tm), lambda bi, ti: (bi, 0, ti)),
                   pl.BlockSpec((1, tm // STAT_BLOCK, 8, LANES), lambda bi, ti: (bi, ti, 0, 0)),
                   pl.BlockSpec((1, tm, hw), lambda bi, ti: (bi, ti, 0))],
        out_shape=[head_out, head_out, jax.ShapeDtypeStruct((b, fw, s), BF16),
                   jax.ShapeDtypeStruct((b, s // STAT_BLOCK, 8, LANES), F32),
                   jax.ShapeDtypeStruct((b, s, hw), BF16)],
        scratch_shapes=[vm((8, LANES), F32), vm((8, LANES), F32), vm((tm, d), BF16),
                        vm((1, tm, hw), F32), vm((1, tm, hw), F32), vm((tm, hw), F32),
                        vm((1, tm, hw), BF16), vm((tm, hw), BF16),
                        vm((HGRN_HEADS, HGRN_DIM, HGRN_DIM), F32), vm((HGRN_HEADS, HGRN_DIM, HGRN_DIM), F32),
                        vm((tm, hw), F32), vm((tm, hw), F32)],
        compiler_params=pltpu.CompilerParams(dimension_semantics=("arbitrary", "arbitrary"),
                                             vmem_limit_bytes=VMEM_LIMIT),
        name="front",
    )(x, norm_g.reshape(1, d), wt, fb, lb_logits, sel, tri, hsum, hgrn_norm_g.reshape(1, HGRN_DIM), tri64, tri16)


def _fox_first_block(stats, tq):
    b, nsb = stats.shape[0], stats.shape[1]
    assert tq % STAT_BLOCK == 0, (tq, STAT_BLOCK)
    r = tq // STAT_BLOCK
    st = stats.reshape(b, nsb // r, r, 8, LANES)
    h = FOX_HEADS
    qn = jnp.sqrt(jnp.max(st[:, :, :, 0, 0:h], axis=2))
    kn = jnp.sqrt(jnp.max(st[:, :, :, 0, h:2 * h], axis=2))
    f_first, f_last = st[:, :, 0, 1, 0:h], st[:, :, r - 1, 2, 0:h]
    bound = (qn[:, :, None, :] * (kn[:, None, :, :] + kn[:, :, None, :])
             + f_first[:, :, None, :] - f_last[:, None, :, :])
    blk = jnp.arange(nsb // r)
    need = jnp.any(jnp.logical_not(bound <= -FOX_SKIP_LOG2), axis=-1) & (blk[None, None, :] < blk[None, :, None])
    first = jnp.min(jnp.where(need, blk[None, None, :], nsb // r), axis=-1)
    return jnp.minimum(first, blk[None, :]).astype(jnp.int32)


def _fox_direct_ok(stats, tq):
    b, nsb = stats.shape[0], stats.shape[1]
    spread = jnp.max(stats[:, :, 3, 0:FOX_HEADS].reshape(b, nsb * STAT_BLOCK // tq, -1), axis=-1)
    return (spread <= FOX_SPREAD_MAX).astype(jnp.int32)


def _fox_kernel(first_ref, direct_ref, q_ref, k_ref, vt_ref, o_ref, qm_sc, m_sc, l_sc, acc_sc, *, tq, hg):
    i = pl.program_id(2)
    hd = FOX_HEAD_DIM
    l_sc[...] = jnp.zeros_like(l_sc)
    acc_sc[...] = jnp.zeros_like(acc_sc)
    lane = lax.broadcasted_iota(jnp.int32, (tq, PAIR_LANES), 1)
    for hh in range(hg):
        own = (lane // hd) % 2 == hh % 2
        qm_sc[hh] = jnp.where(own, q_ref[0, hh // 2], jnp.zeros((), BF16))

    def accumulate_online(hh, st, off):
        m_prev = m_sc[hh]
        m_new = jnp.maximum(m_prev, jnp.max(st, axis=0, keepdims=True))
        alpha = jnp.exp2(m_prev - m_new)
        p = jnp.exp2(st - m_new)
        l_sc[hh] = alpha * l_sc[hh] + jnp.sum(p, axis=0, keepdims=True)
        vt = vt_ref[0, hh * hd:(hh + 1) * hd, pl.ds(off, tq)]
        acc_sc[hh] = alpha * acc_sc[hh] + _dot(vt, p.astype(BF16))
        m_sc[hh] = m_new

    def accumulate_direct(hh, st, off):
        p = jnp.exp2(st)
        l_sc[hh] += jnp.sum(p, axis=0, keepdims=True)
        acc_sc[hh] += _dot(vt_ref[0, hh * hd:(hh + 1) * hd, pl.ds(off, tq)], p.astype(BF16))

    def diagonal_direct():
        half = tq // 2
        off0 = pl.multiple_of(i * tq, tq)
        off1 = pl.multiple_of(i * tq + half, half)
        tri = (lax.broadcasted_iota(jnp.int32, (half, half), 0) <= lax.broadcasted_iota(jnp.int32, (half, half), 1))

        def scores(h_):
            top = _dot_nt(k_ref[0, h_ // 2, pl.ds(off0, half), :], qm_sc[h_])
            low = _dot_nt(k_ref[0, h_ // 2, pl.ds(off1, half), :], qm_sc[h_, half:tq, :])
            return top, low

        ahead = 2
        pending = [scores(h_) for h_ in range(min(ahead, hg))]
        for hh in range(hg):
            top, low = pending.pop(0)
            if hh + ahead < hg:
                pending.append(scores(hh + ahead))
            p_top = jnp.exp2(jnp.concatenate([jnp.where(tri, top[:, 0:half], -jnp.inf), top[:, half:tq]], axis=1))
            p_low = jnp.exp2(jnp.where(tri, low, -jnp.inf))
            rows = slice(hh * hd, (hh + 1) * hd)
            l_sc[hh] += jnp.sum(p_top, axis=0, keepdims=True)
            l_sc[hh, :, half:tq] += jnp.sum(p_low, axis=0, keepdims=True)
            acc_sc[hh] += _dot(vt_ref[0, rows, pl.ds(off0, half)], p_top.astype(BF16))
            acc_sc[hh, :, half:tq] += _dot(vt_ref[0, rows, pl.ds(off1, half)], p_low.astype(BF16))

    def sweep(accumulate, diagonal=None):
        def block(j, masked):
            off = pl.multiple_of(j * tq, tq)
            scores = lambda h_: _dot_nt(k_ref[0, h_ // 2, pl.ds(off, tq), :], qm_sc[h_])
            ahead = 4
            pending = [scores(h_) for h_ in range(min(ahead, hg))]
            for hh in range(hg):
                st = pending.pop(0)
                if hh + ahead < hg:
                    pending.append(scores(hh + ahead))
                if masked:
                    key = lax.broadcasted_iota(jnp.int32, st.shape, 0)
                    qry = lax.broadcasted_iota(jnp.int32, st.shape, 1)
                    st = jnp.where(key <= qry, st, -jnp.inf)
                accumulate(hh, st, off)

        def body(j, carry):
            block(j, False)
            return carry

        lax.fori_loop(first_ref[pl.program_id(0), i], i, body, 0)
        if diagonal is None:
            block(i, True)
        else:
            diagonal()

    direct = direct_ref[pl.program_id(0), i] != 0

    @pl.when(direct)
    def _():
        sweep(accumulate_direct, diagonal_direct)

    @pl.when(jnp.logical_not(direct))
    def _():
        m_sc[...] = jnp.full_like(m_sc, -jnp.inf)
        sweep(accumulate_online)

    for hh in range(hg):
        o_ref[0, hh * hd:(hh + 1) * hd, :] = (acc_sc[hh] / l_sc[hh]).astype(o_ref.dtype)


def _fox_attention(q, k, vt, first_block, direct_ok, tq, hg):
    b, pairs, s, _ = q.shape
    assert hg % 2 == 0
    groups = 2 * pairs // hg
    rows = hg * FOX_HEAD_DIM
    grid_spec = pltpu.PrefetchScalarGridSpec(
        num_scalar_prefetch=2,
        grid=(b, groups, s // tq),
        in_specs=[pl.BlockSpec((1, hg // 2, tq, PAIR_LANES), lambda bi, pi, qi, *_: (bi, pi, qi, 0)),
                  pl.BlockSpec((1, hg // 2, s, PAIR_LANES), lambda bi, pi, qi, *_: (bi, pi, 0, 0)),
                  pl.BlockSpec((1, rows, s), lambda bi, pi, qi, *_: (bi, pi, 0))],
        out_specs=pl.BlockSpec((1, rows, tq), lambda bi, pi, qi, *_: (bi, pi, qi)),
        scratch_shapes=[pltpu.VMEM((hg, tq, PAIR_LANES), BF16),
                        pltpu.VMEM((hg, 1, tq), F32), pltpu.VMEM((hg, 1, tq), F32),
                        pltpu.VMEM((hg, FOX_HEAD_DIM, tq), F32)])
    return pl.pallas_call(
        functools.partial(_fox_kernel, tq=tq, hg=hg),
        grid_spec=grid_spec,
        out_shape=jax.ShapeDtypeStruct((b, groups * rows, s), BF16),
        compiler_params=pltpu.CompilerParams(dimension_semantics=("arbitrary",) * 3,
                                             vmem_limit_bytes=VMEM_LIMIT),
        name="fox_attention",
    )(first_block, direct_ok, q, k, vt)


def _hgrn_chunk_stages(q_ref, k_ref, i_ref, st_sc, b_sc, o_sc, tc):
    c_len = HGRN_CHUNK
    n_chunks = tc // c_len
    t_idx = lax.broadcasted_iota(jnp.int32, (c_len, c_len), 0)
    s_idx = lax.broadcasted_iota(jnp.int32, (c_len, c_len), 1)
    causal = s_idx <= t_idx
    units = [(c, h) for c in range(n_chunks) for h in range(HGRN_HEADS)]
    rows = lambda c: slice(c * c_len, (c + 1) * c_len)
    lanes = lambda h: slice(h * HGRN_DIM, (h + 1) * HGRN_DIM)
    qe, ke, kl, decay, attn, intra, d_state, state = ({} for _ in range(8))

    def decay_operands():
        for c, h in units:
            b = b_sc[rows(c), lanes(h)]
            kk = k_ref[0, rows(c), lanes(h)]
            b_last = b[c_len - 1:c_len, :]
            qe[c, h] = (q_ref[0, rows(c), lanes(h)] * jnp.exp(b)).astype(BF16)
            ke[c, h] = (kk * jnp.exp(-b)).astype(BF16)
            kl[c, h] = (kk * jnp.exp(b_last - b)).astype(BF16)
            decay[c, h] = jnp.exp(b_last)

    def scores():
        for u in units:
            attn[u] = jnp.where(causal, _dot_nt(qe[u], ke[u]), 0.0).astype(BF16)

    def products():
        for c, h in units:
            intra[c, h] = _dot(attn[c, h], i_ref[0, rows(c), lanes(h)])
            d_state[c, h] = _dot_tn(i_ref[0, rows(c), lanes(h)], kl[c, h])

    def recurrence():
        for h in range(HGRN_HEADS):
            st = st_sc[h]
            for c in range(n_chunks):
                state[c, h] = st.astype(BF16)
                st = st * decay[c, h] + d_state[c, h]
            st_sc[h] = st

    def outputs():
        for c, h in units:
            o_sc[rows(c), lanes(h)] = intra[c, h] + _dot_nt(qe[c, h], state[c, h])

    return [decay_operands, scores, products, recurrence, outputs]


def _hgrn_step_path(q_ref, k_ref, i_ref, st_sc, b_sc, o_sc, tc):
    row8 = lax.broadcasted_iota(jnp.int32, (8, HGRN_DIM), 0)
    half = HGRN_STEP // 2

    def step(u, carry):
        base = pl.multiple_of(u * HGRN_STEP, HGRN_STEP)
        for h in range(HGRN_HEADS):
            ls = slice(h * HGRN_DIM, (h + 1) * HGRN_DIM)
            rows = pl.ds(base, HGRN_STEP)
            bq = b_sc[rows, ls]
            qq = q_ref[0, rows, ls]
            kk = k_ref[0, rows, ls]
            ii = i_ref[0, rows, ls].astype(F32)
            b_last = bq[HGRN_STEP - 1:HGRN_STEP, :]
            st = st_sc[h]
            inter = _dot_nt((qq * jnp.exp(bq)).astype(BF16), st.astype(BF16))
            k_dec = (kk * jnp.exp(b_last - bq)).astype(BF16)
            st_sc[h] = st * jnp.exp(b_last) + _dot_tn(ii.astype(BF16), k_dec)
            out = [inter[0:half], inter[half:HGRN_STEP]]
            qv = [qq[0:half], qq[half:HGRN_STEP]]
            bv = [bq[0:half], bq[half:HGRN_STEP]]
            for s_ in range(HGRN_STEP):
                k_s, b_s, i_s = kk[s_:s_ + 1, :], bq[s_:s_ + 1, :], ii[s_:s_ + 1, :]
                for v_ in range(s_ // half, 2):
                    diff = bv[v_] - b_s
                    if s_ > v_ * half:
                        diff = jnp.where(row8 + v_ * half >= s_, diff, -jnp.inf)
                    a = jnp.sum(qv[v_] * k_s * jnp.exp(diff), axis=-1, keepdims=True)
                    out[v_] = out[v_] + a * i_s
            o_sc[pl.ds(base, half), ls] = out[0]
            o_sc[pl.ds(base + half, half), ls] = out[1]
        return carry

    lax.fori_loop(0, tc // HGRN_STEP, step, 0)


def _block_tri(n, blk):
    return jnp.asarray(np.kron(np.eye(n // blk, dtype=np.float32), np.tril(np.ones((blk, blk), np.float32))), BF16)


def _memkv_kernel(mem_ref, g_ref, wkv_ref, wq_ref, wo_ref, sm_ref, om_ref):
    d = mem_ref.shape[-1]
    m = mem_ref.shape[1]
    hd = d // X_HEADS
    inv = 1.0 / math.sqrt(hd)
    kv = _dot(_rms(mem_ref[0], g_ref[...]).astype(BF16), wkv_ref[...].astype(BF16))
    for h in range(X_HEADS):
        cs = slice(h * hd, (h + 1) * hd)
        k_h = kv[:, cs].astype(BF16)
        v_h = kv[:, d + h * hd:d + (h + 1) * hd].astype(BF16)
        sm_ref[0, :, h * m:(h + 1) * m] = (_dot_nt(wq_ref[:, cs].astype(BF16), k_h) * inv).astype(BF16)
        om_ref[0, h * m:(h + 1) * m, :] = _dot(v_h, wo_ref[cs, :].astype(BF16)).astype(BF16)


def _mem_kv(mem, norm_g, w_kv, w_xq, w_xo):
    b, m, d = mem.shape
    return pl.pallas_call(
        _memkv_kernel,
        grid=(b,),
        in_specs=[pl.BlockSpec((1, m, d), lambda bi: (bi, 0, 0)), _resident((1, d)), _resident((d, 2 * d)),
                  _resident((d, d)), _resident((d, d))],
        out_specs=[pl.BlockSpec((1, d, X_HEADS * m), lambda bi: (bi, 0, 0)),
                   pl.BlockSpec((1, X_HEADS * m, d), lambda bi: (bi, 0, 0))],
        out_shape=[jax.ShapeDtypeStruct((b, d, X_HEADS * m), BF16), jax.ShapeDtypeStruct((b, X_HEADS * m, d), BF16)],
        compiler_params=pltpu.CompilerParams(dimension_semantics=("arbitrary",), vmem_limit_bytes=VMEM_LIMIT),
        name="mem_kv",
    )(mem, norm_g.reshape(1, d), w_kv, w_xq, w_xo)


def _mix_kernel(x_ref, foxt_ref, rec_ref, wo_ref, gx_ref, sm_ref, om_ref, o_ref, h1_sc, p_sc, *, sub):
    fw = foxt_ref.shape[1]
    n_mem = sm_ref.shape[2] // X_HEADS
    n_sub = x_ref.shape[1] // sub

    def out_proj(i):
        r = slice(i * sub, (i + 1) * sub)
        h1_sc[r, :] = (x_ref[0, r, :] + _dot_tn(foxt_ref[0, :, r], wo_ref[0:fw, :].astype(BF16))
                       + _dot(rec_ref[0, r, :], wo_ref[fw:, :].astype(BF16)))

    def attend(i):
        r = slice(i * sub, (i + 1) * sub)
        s = _dot(_rms(h1_sc[r, :], gx_ref[...]).astype(BF16), sm_ref[0])
        for h in range(X_HEADS):
            cs = slice(h * n_mem, (h + 1) * n_mem)
            p = jnp.exp(s[:, cs] - jnp.max(s[:, cs], axis=-1, keepdims=True))
            p_sc[r, cs] = (p / jnp.sum(p, axis=-1, keepdims=True)).astype(BF16)

    def finish(i):
        r = slice(i * sub, (i + 1) * sub)
        o_ref[0, r, :] = h1_sc[r, :] + _dot(p_sc[r, :], om_ref[0])

    stages = (out_proj, attend, finish)
    for t in range(n_sub + len(stages) - 1):
        for k, stage in enumerate(stages):
            if 0 <= t - k < n_sub:
                stage(t - k)


def _mix(x, foxt, rec, w_out, norm_g, score_m, out_m, tm):
    b, s, d = x.shape
    fw = foxt.shape[1]
    hm = score_m.shape[2]
    tok = lambda wd: pl.BlockSpec((1, tm, wd), lambda bi, ti: (bi, ti, 0))
    return pl.pallas_call(
        functools.partial(_mix_kernel, sub=min(tm, 256)),
        grid=(b, s // tm),
        in_specs=[tok(d), pl.BlockSpec((1, fw, tm), lambda bi, ti: (bi, 0, ti)), tok(rec.shape[-1]),
                  _resident((d, d)), _resident((1, d)),
                  pl.BlockSpec((1, d, hm), lambda bi, ti: (bi, 0, 0)), pl.BlockSpec((1, hm, d), lambda bi, ti: (bi, 0, 0))],
        out_specs=tok(d),
        out_shape=jax.ShapeDtypeStruct((b, s, d), F32),
        scratch_shapes=[pltpu.VMEM((tm, d), F32), pltpu.VMEM((tm, hm), BF16)],
        compiler_params=pltpu.CompilerParams(dimension_semantics=("arbitrary", "arbitrary"),
                                             vmem_limit_bytes=VMEM_LIMIT),
        name="mix",
    )(x, foxt, rec, w_out, norm_g.reshape(1, d), score_m, out_m)


def _mlp_kernel(h_ref, g_ref, w1_ref, w2_ref, gf_ref, o_ref, acc_sc, *, fc):
    hn = _rms(h_ref[...], g_ref[...]).astype(BF16)
    n_chunks = w1_ref.shape[1] // fc
    up = lambda c: _dot(hn, w1_ref[:, c * fc:(c + 1) * fc].astype(BF16))
    u_next = up(0)
    for c in range(n_chunks):
        u = jnp.maximum(u_next, 0.0)
        if c + 1 < n_chunks:
            u_next = up(c + 1)
        part = _dot((u * u).astype(BF16), w2_ref[c * fc:(c + 1) * fc, :].astype(BF16))
        if c == 0:
            acc_sc[...] = part
        else:
            acc_sc[...] += part
    o_ref[...] = _rms(h_ref[...] + acc_sc[...], gf_ref[...])


def _mlp(h, norm_g, w1, w2, final_g, tm, fc):
    t, d = h.shape
    dff = w1.shape[1]
    return pl.pallas_call(
        functools.partial(_mlp_kernel, fc=fc),
        grid=(t // tm,),
        in_specs=[pl.BlockSpec((tm, d), lambda ti: (ti, 0)), _resident((1, d)),
                  _resident((d, dff)), _resident((dff, d)), _resident((1, d))],
        out_specs=pl.BlockSpec((tm, d), lambda ti: (ti, 0)),
        out_shape=jax.ShapeDtypeStruct((t, d), F32),
        scratch_shapes=[pltpu.VMEM((tm, d), F32)],
        compiler_params=pltpu.CompilerParams(dimension_semantics=("arbitrary",), vmem_limit_bytes=VMEM_LIMIT),
        name="mlp",
    )(h, norm_g.reshape(1, d), w1, w2, final_g.reshape(1, d))


def _tile(n, want):
    t = min(n, want)
    assert n % t == 0, (n, want)
    return t


def _tiles(b, s, d_ff):
    return dict(
        front=_tile(s, 512),
        fox_q=_tile(s, 512),
        fox_heads=FOX_HEADS,
        mix=_tile(s, 1024),
        mlp=_tile(b * s, 512),
        mlp_ff=_tile(d_ff, 1024),
    )


def kernel(x, mem, norm_mix_g, w_in, fox_f_bias, hgrn_lb_logits, hgrn_norm_g, w_out, norm_x_g, norm_mem_g,
           w_xq, w_xkv, w_xo, norm_ff_g, w1, w2, final_norm_g):
    b, s, d = x.shape
    t = _tiles(b, s, w1.shape[-1])
    h = x
    for l in range(w_in.shape[0]):
        q, k, v, stats, rec = _front(h, norm_mix_g[l], w_in[l], fox_f_bias[l], hgrn_lb_logits, hgrn_norm_g[l], l,
                                     t["front"])
        fox = _fox_attention(q, k, v, _fox_first_block(stats, t["fox_q"]), _fox_direct_ok(stats, t["fox_q"]),
                             t["fox_q"], t["fox_heads"])
        score_m, out_m = _mem_kv(mem, norm_mem_g[l], w_xkv[l], w_xq[l], w_xo[l])
        h = _mix(h, fox, rec, w_out[l], norm_x_g[l], score_m, out_m, t["mix"])
        is_last = l == w_in.shape[0] - 1
        assert is_last, "the MLP kernel fuses the final norm, so it must be the last layer"
        h = _mlp(h.reshape(b * s, d), norm_ff_g[l], w1[l], w2[l], final_norm_g, t["mlp"], t["mlp_ff"]).reshape(b, s, d)
    return h
```

```python
import functools
import math

import jax
import jax.numpy as jnp
import numpy as np
from jax import lax
from jax.experimental import pallas as pl
from jax.experimental.pallas import tpu as pltpu

EPS = 1e-6
LOG2E = math.log2(math.e)
LANES = 128
FOX_HEADS = 8
FOX_HEAD_DIM = 64
FOX_WIDTH = FOX_HEADS * FOX_HEAD_DIM
HGRN_HEADS = 4
HGRN_DIM = 128
HGRN_WIDTH = HGRN_HEADS * HGRN_DIM
X_HEADS = 4
HGRN_STEP = 16
HGRN_CHUNK = 64
HGRN_MIN_CHUNK_LOG_DECAY = -60.0
VMEM_LIMIT = 56 * 1024 * 1024

PAIR_LANES = 2 * LANES
AUG_F = 0
AUG_ONE = 3
AUG_SHIFT = 6
FCAT_ONE_LANE = 24
FCAT_SHIFT_LANE = 32
NORM_HEADROOM = 1.0 + 2.0 ** -7
FOX_SHIFT_MARGIN = 60.0
FOX_SPREAD_MAX = 120.0
STAT_BLOCK = 256
FOX_SKIP_LOG2 = 100.0

BF16 = jnp.bfloat16
F32 = jnp.float32


def _dot(a, b):
    return jnp.dot(a, b, preferred_element_type=F32)


def _dot_nt(a, b):
    return lax.dot_general(a, b, (((1,), (1,)), ((), ())), preferred_element_type=F32)


def _dot_tn(a, b):
    return lax.dot_general(a, b, (((0,), (0,)), ((), ())), preferred_element_type=F32)


def _split3(v):
    hi = v.astype(BF16)
    r1 = v - hi.astype(F32)
    mid = r1.astype(BF16)
    lo = (r1 - mid.astype(F32)).astype(BF16)
    return hi, mid, lo


def _tri_cumsum(tri, v):
    hi, mid, lo = _split3(v)
    return _dot(tri, hi) + _dot(tri, mid) + _dot(tri, lo)


def _rms(x, g):
    ms = jnp.mean(x * x, axis=-1, keepdims=True)
    return x * lax.rsqrt(ms + EPS) * g


def _resident(shape):
    return pl.BlockSpec(shape, lambda *_: (0,) * len(shape), pipeline_mode=pl.Buffered(1))


def _front_kernel(x_ref, g_ref, wt_ref, fb_ref, lbl_ref, sel_ref, tri_ref, hsum_ref, ng_ref, tri64_ref, tri16_ref,
                  q_ref, k_ref, vt_ref, stat_ref, rec_ref,
                  carry_ref, kmax_ref, hb_sc, hq_sc, hk_sc, hg_sc, hi_sc, gate_sc, st_sc, st_prev_sc, b_sc, o_sc,
                  *, layer):
    @pl.when(pl.program_id(1) == 0)
    def _():
        carry_ref[...] = jnp.zeros_like(carry_ref)
        kmax_ref[...] = jnp.zeros_like(kmax_ref)
        st_sc[...] = jnp.zeros_like(st_sc)

    tm = x_ref.shape[1]
    hb_sc[...] = _rms(x_ref[0], g_ref[...]).astype(BF16)
    lane = lax.broadcasted_iota(jnp.int32, (tm, LANES), 1)
    fw, w = FOX_WIDTH, HGRN_WIDTH
    proj = lambda r0, n: _dot_nt(hb_sc[...], wt_ref[r0:r0 + n, :].astype(BF16))

    lbl = lbl_ref[...]
    e = jnp.exp(lbl - jnp.max(lbl, axis=0, keepdims=True))
    lb = jnp.sum(e[0:layer + 1, :], axis=0, keepdims=True) / jnp.sum(e, axis=0, keepdims=True)
    seg = lambda n: proj(3 * fw + FOX_HEADS + n * w, w)
    gf = seg(1)
    gq = seg(0)
    f = lb + (1.0 - lb) * jax.nn.sigmoid(gf)
    hk_sc[0] = 1.0 - f
    hg_sc[...] = jnp.log(f)
    gi = seg(2)
    hq_sc[0] = gq * jax.nn.sigmoid(gq)
    gg = seg(3)
    cs_rows = tri64_ref.shape[0]
    for r0 in range(0, tm, cs_rows):
        b_sc[r0:r0 + cs_rows, :] = _tri_cumsum(tri64_ref[...], hg_sc[r0:r0 + cs_rows, :])
    hi_sc[0] = gi.astype(BF16)
    gate_sc[...] = (gg * jax.nn.sigmoid(gg)).astype(BF16)
    chunk_ok = jnp.min(b_sc[...]) >= HGRN_MIN_CHUNK_LOG_DECAY
    st_prev_sc[...] = st_sc[...]

    def hgrn_output():
        for h in range(HGRN_HEADS):
            ls = slice(h * HGRN_DIM, (h + 1) * HGRN_DIM)
            rec_ref[0, :, ls] = (_rms(o_sc[:, ls], ng_ref[...]) * gate_sc[:, ls]).astype(rec_ref.dtype)

    v = {}

    def fox_forget():
        z = proj(3 * fw, LANES) + fb_ref[...]
        logf = jnp.minimum(z, 0.0) - jnp.log(1.0 + jnp.exp(-jnp.abs(z)))
        logf = jnp.where(lane < FOX_HEADS, logf, 0.0)
        grp = tri_ref.shape[0]
        total = carry_ref[0:1, :]
        groups = []
        for r0 in range(0, tm, grp):
            groups.append(_tri_cumsum(tri_ref[...], logf[r0:r0 + grp, :]) + total)
            total = groups[-1][grp - 1:grp, :]
        carry_ref[...] = jnp.broadcast_to(total, carry_ref.shape)
        v["fsc"] = jnp.concatenate(groups, axis=0) * LOG2E

    def fox_qk():
        q_scale = LOG2E / math.sqrt(FOX_HEAD_DIM)
        v["qk"] = jnp.concatenate([(proj(0, fw) * q_scale).astype(BF16), proj(fw, fw).astype(BF16)],
                                  axis=1)

    def fox_bounds():
        qkf = v["qk"].astype(F32)
        norm2 = _dot((qkf * qkf).astype(BF16), hsum_ref[...]) * NORM_HEADROOM
        norm = jnp.sqrt(norm2)
        to_q_lanes = lambda a: pltpu.roll(a, LANES - FOX_HEADS, 1)
        k_run = jnp.maximum(kmax_ref[...], jnp.max(norm, axis=0, keepdims=True))
        kmax_ref[...] = k_run
        k_run_q = to_q_lanes(k_run)[0:1, :]
        v["shift"] = jnp.where(lane < FOX_HEADS, norm * k_run_q - FOX_SHIFT_MARGIN, 0.0)
        v["spread"] = norm * (k_run_q + to_q_lanes(norm))
        v["norm2"] = norm2

    def fox_operands():
        parts = lambda a: [p.astype(F32) for p in _split3(a)]
        f_parts, s_parts = parts(v["fsc"]), parts(-v["shift"])
        fcat = jnp.where(lane == FCAT_ONE_LANE, 1.0, 0.0)
        for j in range(3):
            fcat = fcat + (pltpu.roll(f_parts[j], j * FOX_HEADS, 1) if j else f_parts[j])
            fcat = fcat + pltpu.roll(s_parts[j], FCAT_SHIFT_LANE + j * FOX_HEADS, 1)
        aug = _dot(fcat.astype(BF16), sel_ref[...]).astype(BF16)
        qk, pairs = v["qk"], FOX_HEADS // 2
        for p in range(pairs):
            q_ref[0, p, :, 0:LANES] = qk[:, p * LANES:(p + 1) * LANES]
            q_ref[0, p, :, LANES:PAIR_LANES] = aug[:, p * LANES:(p + 1) * LANES]
            k_ref[0, p, :, 0:LANES] = qk[:, (pairs + p) * LANES:(pairs + p + 1) * LANES]
            k_ref[0, p, :, LANES:PAIR_LANES] = aug[:, (pairs + p) * LANES:(pairs + p + 1) * LANES]

    def fox_values():
        vt_ref[0] = _dot_nt(wt_ref[2 * fw:3 * fw, :].astype(BF16), hb_sc[...]).astype(BF16)

    def fox_stats():
        for sb in range(tm // STAT_BLOCK):
            rows = slice(sb * STAT_BLOCK, (sb + 1) * STAT_BLOCK)
            stat_ref[0, sb, 0:1, :] = jnp.max(v["norm2"][rows, :], axis=0, keepdims=True)
            stat_ref[0, sb, 1:2, :] = v["fsc"][rows, :][0:1, :]
            stat_ref[0, sb, 2:3, :] = v["fsc"][rows, :][STAT_BLOCK - 1:STAT_BLOCK, :]
            stat_ref[0, sb, 3:4, :] = jnp.max(v["spread"][rows, :], axis=0, keepdims=True)
            stat_ref[0, sb, 4:8, :] = jnp.zeros((4, LANES), F32)

    fox_stages = [fox_forget, fox_qk, fox_bounds, fox_operands, fox_values, fox_stats]

    hgrn_stages = _hgrn_chunk_stages(hq_sc, hk_sc, hi_sc, st_sc, b_sc, o_sc, tm) + [hgrn_output]
    for a, b in zip(hgrn_stages, fox_stages):
        a()
        b()

    @pl.when(jnp.logical_not(chunk_ok))
    def _():
        st_sc[...] = st_prev_sc[...]
        for r0 in range(0, tm, cs_rows):
            b_sc[r0:r0 + cs_rows, :] = _tri_cumsum(tri16_ref[...], hg_sc[r0:r0 + cs_rows, :])
        _hgrn_step_path(hq_sc, hk_sc, hi_sc, st_sc, b_sc, o_sc, tm)
        hgrn_output()


def _front(x, norm_g, w_in, fox_f_bias, lb_logits, hgrn_norm_g, layer, tm):
    b, s, d = x.shape
    fw, hw = FOX_WIDTH, HGRN_WIDTH
    assert w_in.shape == (d, 3 * fw + FOX_HEADS + 4 * hw)
    wt = w_in.T
    fb = jnp.pad(fox_f_bias.reshape(1, FOX_HEADS), ((0, 0), (0, LANES - FOX_HEADS)))

    sel = np.zeros((LANES, 2 * fw), np.float32)
    for h in range(FOX_HEADS):
        col = h * FOX_HEAD_DIM
        for j in range(3):
            sel[j * FOX_HEADS + h, col + AUG_F + j] = 1.0
            sel[FCAT_ONE_LANE, col + AUG_ONE + j] = 1.0
            sel[FCAT_ONE_LANE, fw + col + AUG_F + j] = 1.0
            sel[j * FOX_HEADS + h, fw + col + AUG_ONE + j] = -1.0
            sel[FCAT_SHIFT_LANE + j * FOX_HEADS + h, col + AUG_SHIFT + j] = 1.0
            sel[FCAT_ONE_LANE, fw + col + AUG_SHIFT + j] = 1.0
    sel = jnp.asarray(sel, BF16)
    fg = min(tm, LANES)
    tri = jnp.asarray(np.tril(np.ones((fg, fg), np.float32)), BF16)
    hsum = np.zeros((2 * fw, LANES), np.float32)
    hsum[np.arange(2 * fw), np.arange(2 * fw) // FOX_HEAD_DIM] = 1.0
    hsum = jnp.asarray(hsum, BF16)

    cs_rows = min(tm, 256)
    tri64, tri16 = _block_tri(cs_rows, HGRN_CHUNK), _block_tri(cs_rows, HGRN_STEP)

    const = _resident
    pairs = FOX_HEADS // 2
    head_out = jax.ShapeDtypeStruct((b, pairs, s, PAIR_LANES), BF16)
    head_spec = pl.BlockSpec((1, pairs, tm, PAIR_LANES), lambda bi, ti: (bi, 0, ti, 0))
    vm = pltpu.VMEM
    return pl.pallas_call(
        functools.partial(_front_kernel, layer=layer),
        grid=(b, s // tm),
        in_specs=[pl.BlockSpec((1, tm, d), lambda bi, ti: (bi, ti, 0)),
                  const((1, d)), const(wt.shape), const(fb.shape), const(lb_logits.shape), const(sel.shape),
                  const(tri.shape), const(hsum.shape), const((1, HGRN_DIM)), const(tri64.shape), const(tri16.shape)],
        out_specs=[head_spec, head_spec, pl.BlockSpec((1, fw, tm), lambda bi, ti: (bi, 0, ti)),
                   pl.BlockSpec((1, tm // STAT_BLOCK, 8, LANES), lambda bi, ti: (bi, ti, 0, 0)),
                   pl.BlockSpec((1, tm, hw), lambda bi, ti: (bi, ti, 0))],
        out_shape=[head_out, head_out, jax.ShapeDtypeStruct((b, fw, s), BF16),
                   jax.ShapeDtypeStruct((b, s // STAT_BLOCK, 8, LANES), F32),
                   jax.ShapeDtypeStruct((b, s, hw), BF16)],
        scratch_shapes=[vm((8, LANES), F32), vm((8, LANES), F32), vm((tm, d), BF16),
                        vm((1, tm, hw), F32), vm((1, tm, hw), F32), vm((tm, hw), F32),
                        vm((1, tm, hw), BF16), vm((tm, hw), BF16),
                        vm((HGRN_HEADS, HGRN_DIM, HGRN_DIM), F32), vm((HGRN_HEADS, HGRN_DIM, HGRN_DIM), F32),
                        vm((tm, hw), F32), vm((tm, hw), F32)],
        compiler_params=pltpu.CompilerParams(dimension_semantics=("arbitrary", "arbitrary"),
                                             vmem_limit_bytes=VMEM_LIMIT),
        name="front",
    )(x, norm_g.reshape(1, d), wt, fb, lb_logits, sel, tri, hsum, hgrn_norm_g.reshape(1, HGRN_DIM), tri64, tri16)


def _fox_first_block(stats, tq):
    b, nsb = stats.shape[0], stats.shape[1]
    assert tq % STAT_BLOCK == 0, (tq, STAT_BLOCK)
    r = tq // STAT_BLOCK
    st = stats.reshape(b, nsb // r, r, 8, LANES)
    h = FOX_HEADS
    qn = jnp.sqrt(jnp.max(st[:, :, :, 0, 0:h], axis=2))
    kn = jnp.sqrt(jnp.max(st[:, :, :, 0, h:2 * h], axis=2))
    f_first, f_last = st[:, :, 0, 1, 0:h], st[:, :, r - 1, 2, 0:h]
    bound = (qn[:, :, None, :] * (kn[:, None, :, :] + kn[:, :, None, :])
             + f_first[:, :, None, :] - f_last[:, None, :, :])
    blk = jnp.arange(nsb // r)
    need = jnp.any(jnp.logical_not(bound <= -FOX_SKIP_LOG2), axis=-1) & (blk[None, None, :] < blk[None, :, None])
    first = jnp.min(jnp.where(need, blk[None, None, :], nsb // r), axis=-1)
    return jnp.minimum(first, blk[None, :]).astype(jnp.int32)


def _fox_direct_ok(stats, tq):
    b, nsb = stats.shape[0], stats.shape[1]
    spread = jnp.max(stats[:, :, 3, 0:FOX_HEADS].reshape(b, nsb * STAT_BLOCK // tq, -1), axis=-1)
    return (spread <= FOX_SPREAD_MAX).astype(jnp.int32)


def _fox_kernel(first_ref, direct_ref, q_ref, k_ref, vt_ref, o_ref, qm_sc, m_sc, l_sc, acc_sc, *, tq, hg):
    i = pl.program_id(2)
    hd = FOX_HEAD_DIM
    l_sc[...] = jnp.zeros_like(l_sc)
    acc_sc[...] = jnp.zeros_like(acc_sc)
    lane = lax.broadcasted_iota(jnp.int32, (tq, PAIR_LANES), 1)
    for hh in range(hg):
        own = (lane // hd) % 2 == hh % 2
        qm_sc[hh] = jnp.where(own, q_ref[0, hh // 2], jnp.zeros((), BF16))

    def accumulate_online(hh, st, off):
        m_prev = m_sc[hh]
        m_new = jnp.maximum(m_prev, jnp.max(st, axis=0, keepdims=True))
        alpha = jnp.exp2(m_prev - m_new)
        p = jnp.exp2(st - m_new)
        l_sc[hh] = alpha * l_sc[hh] + jnp.sum(p, axis=0, keepdims=True)
        vt = vt_ref[0, hh * hd:(hh + 1) * hd, pl.ds(off, tq)]
        acc_sc[hh] = alpha * acc_sc[hh] + _dot(vt, p.astype(BF16))
        m_sc[hh] = m_new

    def accumulate_direct(hh, st, off):
        p = jnp.exp2(st)
        l_sc[hh] += jnp.sum(p, axis=0, keepdims=True)
        acc_sc[hh] += _dot(vt_ref[0, hh * hd:(hh + 1) * hd, pl.ds(off, tq)], p.astype(BF16))

    def diagonal_direct():
        half = tq // 2
        off0 = pl.multiple_of(i * tq, tq)
        off1 = pl.multiple_of(i * tq + half, half)
        tri = (lax.broadcasted_iota(jnp.int32, (half, half), 0) <= lax.broadcasted_iota(jnp.int32, (half, half), 1))

        def scores(h_):
            top = _dot_nt(k_ref[0, h_ // 2, pl.ds(off0, half), :], qm_sc[h_])
            low = _dot_nt(k_ref[0, h_ // 2, pl.ds(off1, half), :], qm_sc[h_, half:tq, :])
            return top, low

        ahead = 2
        pending = [scores(h_) for h_ in range(min(ahead, hg))]
        for hh in range(hg):
            top, low = pending.pop(0)
            if hh + ahead < hg:
                pending.append(scores(hh + ahead))
            p_top = jnp.exp2(jnp.concatenate([jnp.where(tri, top[:, 0:half], -jnp.inf), top[:, half:tq]], axis=1))
            p_low = jnp.exp2(jnp.where(tri, low, -jnp.inf))
            rows = slice(hh * hd, (hh + 1) * hd)
            l_sc[hh] += jnp.sum(p_top, axis=0, keepdims=True)
            l_sc[hh, :, half:tq] += jnp.sum(p_low, axis=0, keepdims=True)
            acc_sc[hh] += _dot(vt_ref[0, rows, pl.ds(off0, half)], p_top.astype(BF16))
            acc_sc[hh, :, half:tq] += _dot(vt_ref[0, rows, pl.ds(off1, half)], p_low.astype(BF16))

    def sweep(accumulate, diagonal=None):
        def block(j, masked):
            off = pl.multiple_of(j * tq, tq)
            scores = lambda h_: _dot_nt(k_ref[0, h_ // 2, pl.ds(off, tq), :], qm_sc[h_])
            ahead = 4
            pending = [scores(h_) for h_ in range(min(ahead, hg))]
            for hh in range(hg):
                st = pending.pop(0)
                if hh + ahead < hg:
                    pending.append(scores(hh + ahead))
                if masked:
                    key = lax.broadcasted_iota(jnp.int32, st.shape, 0)
                    qry = lax.broadcasted_iota(jnp.int32, st.shape, 1)
                    st = jnp.where(key <= qry, st, -jnp.inf)
                accumulate(hh, st, off)

        def body(j, carry):
            block(j, False)
            return carry

        lax.fori_loop(first_ref[pl.program_id(0), i], i, body, 0)
        if diagonal is None:
            block(i, True)
        else:
            diagonal()

    direct = direct_ref[pl.program_id(0), i] != 0

    @pl.when(direct)
    def _():
        sweep(accumulate_direct, diagonal_direct)

    @pl.when(jnp.logical_not(direct))
    def _():
        m_sc[...] = jnp.full_like(m_sc, -jnp.inf)
        sweep(accumulate_online)

    for hh in range(hg):
        o_ref[0, hh * hd:(hh + 1) * hd, :] = (acc_sc[hh] / l_sc[hh]).astype(o_ref.dtype)


def _fox_attention(q, k, vt, first_block, direct_ok, tq, hg):
    b, pairs, s, _ = q.shape
    assert hg % 2 == 0
    groups = 2 * pairs // hg
    rows = hg * FOX_HEAD_DIM
    grid_spec = pltpu.PrefetchScalarGridSpec(
        num_scalar_prefetch=2,
        grid=(b, groups, s // tq),
        in_specs=[pl.BlockSpec((1, hg // 2, tq, PAIR_LANES), lambda bi, pi, qi, *_: (bi, pi, qi, 0)),
                  pl.BlockSpec((1, hg // 2, s, PAIR_LANES), lambda bi, pi, qi, *_: (bi, pi, 0, 0)),
                  pl.BlockSpec((1, rows, s), lambda bi, pi, qi, *_: (bi, pi, 0))],
        out_specs=pl.BlockSpec((1, rows, tq), lambda bi, pi, qi, *_: (bi, pi, qi)),
        scratch_shapes=[pltpu.VMEM((hg, tq, PAIR_LANES), BF16),
                        pltpu.VMEM((hg, 1, tq), F32), pltpu.VMEM((hg, 1, tq), F32),
                        pltpu.VMEM((hg, FOX_HEAD_DIM, tq), F32)])
    return pl.pallas_call(
        functools.partial(_fox_kernel, tq=tq, hg=hg),
        grid_spec=grid_spec,
        out_shape=jax.ShapeDtypeStruct((b, groups * rows, s), BF16),
        compiler_params=pltpu.CompilerParams(dimension_semantics=("arbitrary",) * 3,
                                             vmem_limit_bytes=VMEM_LIMIT),
        name="fox_attention",
    )(first_block, direct_ok, q, k, vt)


def _hgrn_chunk_stages(q_ref, k_ref, i_ref, st_sc, b_sc, o_sc, tc):
    c_len = HGRN_CHUNK
    n_chunks = tc // c_len
    t_idx = lax.broadcasted_iota(jnp.int32, (c_len, c_len), 0)
    s_idx = lax.broadcasted_iota(jnp.int32, (c_len, c_len), 1)
    causal = s_idx <= t_idx
    units = [(c, h) for c in range(n_chunks) for h in range(HGRN_HEADS)]
    rows = lambda c: slice(c * c_len, (c + 1) * c_len)
    lanes = lambda h: slice(h * HGRN_DIM, (h + 1) * HGRN_DIM)
    qe, ke, kl, decay, attn, intra, d_state, state = ({} for _ in range(8))

    def decay_operands():
        for c, h in units:
            b = b_sc[rows(c), lanes(h)]
            kk = k_ref[0, rows(c), lanes(h)]
            b_last = b[c_len - 1:c_len, :]
            qe[c, h] = (q_ref[0, rows(c), lanes(h)] * jnp.exp(b)).astype(BF16)
            ke[c, h] = (kk * jnp.exp(-b)).astype(BF16)
            kl[c, h] = (kk * jnp.exp(b_last - b)).astype(BF16)
            decay[c, h] = jnp.exp(b_last)

    def scores():
        for u in units:
            attn[u] = jnp.where(causal, _dot_nt(qe[u], ke[u]), 0.0).astype(BF16)

    def products():
        for c, h in units:
            intra[c, h] = _dot(attn[c, h], i_ref[0, rows(c), lanes(h)])
            d_state[c, h] = _dot_tn(i_ref[0, rows(c), lanes(h)], kl[c, h])

    def recurrence():
        for h in range(HGRN_HEADS):
            st = st_sc[h]
            for c in range(n_chunks):
                state[c, h] = st.astype(BF16)
                st = st * decay[c, h] + d_state[c, h]
            st_sc[h] = st

    def outputs():
        for c, h in units:
            o_sc[rows(c), lanes(h)] = intra[c, h] + _dot_nt(qe[c, h], state[c, h])

    return [decay_operands, scores, products, recurrence, outputs]


def _hgrn_step_path(q_ref, k_ref, i_ref, st_sc, b_sc, o_sc, tc):
    row8 = lax.broadcasted_iota(jnp.int32, (8, HGRN_DIM), 0)
    half = HGRN_STEP // 2

    def step(u, carry):
        base = pl.multiple_of(u * HGRN_STEP, HGRN_STEP)
        for h in range(HGRN_HEADS):
            ls = slice(h * HGRN_DIM, (h + 1) * HGRN_DIM)
            rows = pl.ds(base, HGRN_STEP)
            bq = b_sc[rows, ls]
            qq = q_ref[0, rows, ls]
            kk = k_ref[0, rows, ls]
            ii = i_ref[0, rows, ls].astype(F32)
            b_last = bq[HGRN_STEP - 1:HGRN_STEP, :]
            st = st_sc[h]
            inter = _dot_nt((qq * jnp.exp(bq)).astype(BF16), st.astype(BF16))
            k_dec = (kk * jnp.exp(b_last - bq)).astype(BF16)
            st_sc[h] = st * jnp.exp(b_last) + _dot_tn(ii.astype(BF16), k_dec)
            out = [inter[0:half], inter[half:HGRN_STEP]]
            qv = [qq[0:half], qq[half:HGRN_STEP]]
            bv = [bq[0:half], bq[half:HGRN_STEP]]
            for s_ in range(HGRN_STEP):
                k_s, b_s, i_s = kk[s_:s_ + 1, :], bq[s_:s_ + 1, :], ii[s_:s_ + 1, :]
                for v_ in range(s_ // half, 2):
                    diff = bv[v_] - b_s
                    if s_ > v_ * half:
                        diff = jnp.where(row8 + v_ * half >= s_, diff, -jnp.inf)
                    a = jnp.sum(qv[v_] * k_s * jnp.exp(diff), axis=-1, keepdims=True)
                    out[v_] = out[v_] + a * i_s
            o_sc[pl.ds(base, half), ls] = out[0]
            o_sc[pl.ds(base + half, half), ls] = out[1]
        return carry

    lax.fori_loop(0, tc // HGRN_STEP, step, 0)


def _block_tri(n, blk):
    return jnp.asarray(np.kron(np.eye(n // blk, dtype=np.float32), np.tril(np.ones((blk, blk), np.float32))), BF16)


def _memkv_kernel(mem_ref, g_ref, wkv_ref, wq_ref, wo_ref, sm_ref, om_ref):
    d = mem_ref.shape[-1]
    m = mem_ref.shape[1]
    hd = d // X_HEADS
    inv = 1.0 / math.sqrt(hd)
    kv = _dot(_rms(mem_ref[0], g_ref[...]).astype(BF16), wkv_ref[...].astype(BF16))
    for h in range(X_HEADS):
        cs = slice(h * hd, (h + 1) * hd)
        k_h = kv[:, cs].astype(BF16)
        v_h = kv[:, d + h * hd:d + (h + 1) * hd].astype(BF16)
        sm_ref[0, :, h * m:(h + 1) * m] = (_dot_nt(wq_ref[:, cs].astype(BF16), k_h) * inv).astype(BF16)
        om_ref[0, h * m:(h + 1) * m, :] = _dot(v_h, wo_ref[cs, :].astype(BF16)).astype(BF16)


def _mem_kv(mem, norm_g, w_kv, w_xq, w_xo):
    b, m, d = mem.shape
    return pl.pallas_call(
        _memkv_kernel,
        grid=(b,),
        in_specs=[pl.BlockSpec((1, m, d), lambda bi: (bi, 0, 0)), _resident((1, d)), _resident((d, 2 * d)),
                  _resident((d, d)), _resident((d, d))],
        out_specs=[pl.BlockSpec((1, d, X_HEADS * m), lambda bi: (bi, 0, 0)),
                   pl.BlockSpec((1, X_HEADS * m, d), lambda bi: (bi, 0, 0))],
        out_shape=[jax.ShapeDtypeStruct((b, d, X_HEADS * m), BF16), jax.ShapeDtypeStruct((b, X_HEADS * m, d), BF16)],
        compiler_params=pltpu.CompilerParams(dimension_semantics=("arbitrary",), vmem_limit_bytes=VMEM_LIMIT),
        name="mem_kv",
    )(mem, norm_g.reshape(1, d), w_kv, w_xq, w_xo)


def _mix_kernel(x_ref, foxt_ref, rec_ref, wo_ref, gx_ref, sm_ref, om_ref, o_ref, h1_sc, p_sc, *, sub):
    fw = foxt_ref.shape[1]
    n_mem = sm_ref.shape[2] // X_HEADS
    n_sub = x_ref.shape[1] // sub

    def out_proj(i):
        r = slice(i * sub, (i + 1) * sub)
        h1_sc[r, :] = (x_ref[0, r, :] + _dot_tn(foxt_ref[0, :, r], wo_ref[0:fw, :].astype(BF16))
                       + _dot(rec_ref[0, r, :], wo_ref[fw:, :].astype(BF16)))

    def attend(i):
        r = slice(i * sub, (i + 1) * sub)
        s = _dot(_rms(h1_sc[r, :], gx_ref[...]).astype(BF16), sm_ref[0])
        for h in range(X_HEADS):
            cs = slice(h * n_mem, (h + 1) * n_mem)
            p = jnp.exp(s[:, cs] - jnp.max(s[:, cs], axis=-1, keepdims=True))
            p_sc[r, cs] = (p / jnp.sum(p, axis=-1, keepdims=True)).astype(BF16)

    def finish(i):
        r = slice(i * sub, (i + 1) * sub)
        o_ref[0, r, :] = h1_sc[r, :] + _dot(p_sc[r, :], om_ref[0])

    stages = (out_proj, attend, finish)
    for t in range(n_sub + len(stages) - 1):
        for k, stage in enumerate(stages):
            if 0 <= t - k < n_sub:
                stage(t - k)


def _mix(x, foxt, rec, w_out, norm_g, score_m, out_m, tm):
    b, s, d = x.shape
    fw = foxt.shape[1]
    hm = score_m.shape[2]
    tok = lambda wd: pl.BlockSpec((1, tm, wd), lambda bi, ti: (bi, ti, 0))
    return pl.pallas_call(
        functools.partial(_mix_kernel, sub=min(tm, 256)),
        grid=(b, s // tm),
        in_specs=[tok(d), pl.BlockSpec((1, fw, tm), lambda bi, ti: (bi, 0, ti)), tok(rec.shape[-1]),
                  _resident((d, d)), _resident((1, d)),
                  pl.BlockSpec((1, d, hm), lambda bi, ti: (bi, 0, 0)), pl.BlockSpec((1, hm, d), lambda bi, ti: (bi, 0, 0))],
        out_specs=tok(d),
        out_shape=jax.ShapeDtypeStruct((b, s, d), F32),
        scratch_shapes=[pltpu.VMEM((tm, d), F32), pltpu.VMEM((tm, hm), BF16)],
        compiler_params=pltpu.CompilerParams(dimension_semantics=("arbitrary", "arbitrary"),
                                             vmem_limit_bytes=VMEM_LIMIT),
        name="mix",
    )(x, foxt, rec, w_out, norm_g.reshape(1, d), score_m, out_m)


def _mlp_kernel(h_ref, g_ref, w1_ref, w2_ref, gf_ref, o_ref, acc_sc, *, fc):
    hn = _rms(h_ref[...], g_ref[...]).astype(BF16)
    n_chunks = w1_ref.shape[1] // fc
    up = lambda c: _dot(hn, w1_ref[:, c * fc:(c + 1) * fc].astype(BF16))
    u_next = up(0)
    for c in range(n_chunks):
        u = jnp.maximum(u_next, 0.0)
        if c + 1 < n_chunks:
            u_next = up(c + 1)
        part = _dot((u * u).astype(BF16), w2_ref[c * fc:(c + 1) * fc, :].astype(BF16))
        if c == 0:
            acc_sc[...] = part
        else:
            acc_sc[...] += part
    o_ref[...] = _rms(h_ref[...] + acc_sc[...], gf_ref[...])


def _mlp(h, norm_g, w1, w2, final_g, tm, fc):
    t, d = h.shape
    dff = w1.shape[1]
    return pl.pallas_call(
        functools.partial(_mlp_kernel, fc=fc),
        grid=(t // tm,),
        in_specs=[pl.BlockSpec((tm, d), lambda ti: (ti, 0)), _resident((1, d)),
                  _resident((d, dff)), _resident((dff, d)), _resident((1, d))],
        out_specs=pl.BlockSpec((tm, d), lambda ti: (ti, 0)),
        out_shape=jax.ShapeDtypeStruct((t, d), F32),
        scratch_shapes=[pltpu.VMEM((tm, d), F32)],
        compiler_params=pltpu.CompilerParams(dimension_semantics=("arbitrary",), vmem_limit_bytes=VMEM_LIMIT),
        name="mlp",
    )(h, norm_g.reshape(1, d), w1, w2, final_g.reshape(1, d))


def _tile(n, want):
    t = min(n, want)
    assert n % t == 0, (n, want)
    return t


def _tiles(b, s, d_ff):
    return dict(
        front=_tile(s, 512),
        fox_q=_tile(s, 512),
        fox_heads=FOX_HEADS,
        mix=_tile(s, 1024),
        mlp=_tile(b * s, 512),
        mlp_ff=_tile(d_ff, 1024),
    )


def kernel(x, mem, norm_mix_g, w_in, fox_f_bias, hgrn_lb_logits, hgrn_norm_g, w_out, norm_x_g, norm_mem_g,
           w_xq, w_xkv, w_xo, norm_ff_g, w1, w2, final_norm_g):
    b, s, d = x.shape
    t = _tiles(b, s, w1.shape[-1])
    h = x
    for l in range(w_in.shape[0]):
        q, k, v, stats, rec = _front(h, norm_mix_g[l], w_in[l], fox_f_bias[l], hgrn_lb_logits, hgrn_norm_g[l], l,
                                     t["front"])
        fox = _fox_attention(q, k, v, _fox_first_block(stats, t["fox_q"]), _fox_direct_ok(stats, t["fox_q"]),
                             t["fox_q"], t["fox_heads"])
        score_m, out_m = _mem_kv(mem, norm_mem_g[l], w_xkv[l], w_xq[l], w_xo[l])
        h = _mix(h, fox, rec, w_out[l], norm_x_g[l], score_m, out_m, t["mix"])
        is_last = l == w_in.shape[0] - 1
        assert is_last, "the MLP kernel fuses the final norm, so it must be the last layer"
        h = _mlp(h.reshape(b * s, d), norm_ff_g[l], w1[l], w2[l], final_norm_g, t["mlp"], t["mlp_ff"]).reshape(b, s, d)
    return h
```

```python
import functools
import math

import jax
import jax.numpy as jnp
import numpy as np
from jax import lax
from jax.experimental import pallas as pl
from jax.experimental.pallas import tpu as pltpu

EPS = 1e-6
LOG2E = math.log2(math.e)
LANES = 128
FOX_HEADS = 8
FOX_HEAD_DIM = 64
FOX_WIDTH = FOX_HEADS * FOX_HEAD_DIM
HGRN_HEADS = 4
HGRN_DIM = 128
HGRN_WIDTH = HGRN_HEADS * HGRN_DIM
X_HEADS = 4
HGRN_STEP = 16
HGRN_CHUNK = 64
HGRN_MIN_CHUNK_LOG_DECAY = -60.0
VMEM_LIMIT = 56 * 1024 * 1024

PAIR_LANES = 2 * LANES
AUG_F = 0
AUG_ONE = 3
AUG_SHIFT = 6
FCAT_ONE_LANE = 24
FCAT_SHIFT_LANE = 32
NORM_HEADROOM = 1.0 + 2.0 ** -7
FOX_SHIFT_MARGIN = 60.0
FOX_SPREAD_MAX = 120.0
STAT_BLOCK = 256
FOX_SKIP_LOG2 = 70.0

BF16 = jnp.bfloat16
F32 = jnp.float32


def _dot(a, b):
    return jnp.dot(a, b, preferred_element_type=F32)


def _dot_nt(a, b):
    return lax.dot_general(a, b, (((1,), (1,)), ((), ())), preferred_element_type=F32)


def _dot_tn(a, b):
    return lax.dot_general(a, b, (((0,), (0,)), ((), ())), preferred_element_type=F32)


def _split3(v):
    hi = v.astype(BF16)
    r1 = v - hi.astype(F32)
    mid = r1.astype(BF16)
    lo = (r1 - mid.astype(F32)).astype(BF16)
    return hi, mid, lo


def _tri_cumsum(tri, v):
    hi, mid, lo = _split3(v)
    return _dot(tri, hi) + _dot(tri, mid) + _dot(tri, lo)


def _rms(x, g):
    ms = jnp.mean(x * x, axis=-1, keepdims=True)
    return x * lax.rsqrt(ms + EPS) * g


def _resident(shape):
    return pl.BlockSpec(shape, lambda *_: (0,) * len(shape), pipeline_mode=pl.Buffered(1))


def _front_kernel(x_ref, g_ref, wt_ref, fb_ref, lbl_ref, sel_ref, tri_ref, hsum_ref, ng_ref, tri64_ref, tri16_ref,
                  q_ref, k_ref, vt_ref, stat_ref, rec_ref,
                  carry_ref, kmax_ref, hb_sc, hq_sc, hk_sc, hg_sc, hi_sc, gate_sc, st_sc, st_prev_sc, b_sc, o_sc,
                  *, layer):
    @pl.when(pl.program_id(1) == 0)
    def _():
        carry_ref[...] = jnp.zeros_like(carry_ref)
        kmax_ref[...] = jnp.zeros_like(kmax_ref)
        st_sc[...] = jnp.zeros_like(st_sc)

    tm = x_ref.shape[1]
    hb_sc[...] = _rms(x_ref[0], g_ref[...]).astype(BF16)
    lane = lax.broadcasted_iota(jnp.int32, (tm, LANES), 1)
    fw, w = FOX_WIDTH, HGRN_WIDTH
    proj = lambda r0, n: _dot_nt(hb_sc[...], wt_ref[r0:r0 + n, :].astype(BF16))

    lbl = lbl_ref[...]
    e = jnp.exp(lbl - jnp.max(lbl, axis=0, keepdims=True))
    lb = jnp.sum(e[0:layer + 1, :], axis=0, keepdims=True) / jnp.sum(e, axis=0, keepdims=True)
    seg = lambda n: proj(3 * fw + FOX_HEADS + n * w, w)
    gf = seg(1)
    gq = seg(0)
    f = lb + (1.0 - lb) * jax.nn.sigmoid(gf)
    hk_sc[0] = 1.0 - f
    hg_sc[...] = jnp.log(f)
    gi = seg(2)
    hq_sc[0] = gq * jax.nn.sigmoid(gq)
    gg = seg(3)
    cs_rows = tri64_ref.shape[0]
    for r0 in range(0, tm, cs_rows):
        b_sc[r0:r0 + cs_rows, :] = _tri_cumsum(tri64_ref[...], hg_sc[r0:r0 + cs_rows, :])
    hi_sc[0] = gi.astype(BF16)
    gate_sc[...] = (gg * jax.nn.sigmoid(gg)).astype(BF16)
    chunk_ok = jnp.min(b_sc[...]) >= HGRN_MIN_CHUNK_LOG_DECAY
    st_prev_sc[...] = st_sc[...]

    def hgrn_output():
        for h in range(HGRN_HEADS):
            ls = slice(h * HGRN_DIM, (h + 1) * HGRN_DIM)
            rec_ref[0, :, ls] = (_rms(o_sc[:, ls], ng_ref[...]) * gate_sc[:, ls]).astype(rec_ref.dtype)

    v = {}

    def fox_forget():
        z = proj(3 * fw, LANES) + fb_ref[...]
        logf = jnp.minimum(z, 0.0) - jnp.log(1.0 + jnp.exp(-jnp.abs(z)))
        logf = jnp.where(lane < FOX_HEADS, logf, 0.0)
        grp = tri_ref.shape[0]
        total = carry_ref[0:1, :]
        groups = []
        for r0 in range(0, tm, grp):
            groups.append(_tri_cumsum(tri_ref[...], logf[r0:r0 + grp, :]) + total)
            total = groups[-1][grp - 1:grp, :]
        carry_ref[...] = jnp.broadcast_to(total, carry_ref.shape)
        v["fsc"] = jnp.concatenate(groups, axis=0) * LOG2E

    def fox_qk():
        q_scale = LOG2E / math.sqrt(FOX_HEAD_DIM)
        v["qk"] = jnp.concatenate([(proj(0, fw) * q_scale).astype(BF16), proj(fw, fw).astype(BF16)],
                                  axis=1)

    def fox_bounds():
        qkf = v["qk"].astype(F32)
        norm2 = _dot((qkf * qkf).astype(BF16), hsum_ref[...]) * NORM_HEADROOM
        norm = jnp.sqrt(norm2)
        to_q_lanes = lambda a: pltpu.roll(a, LANES - FOX_HEADS, 1)
        k_run = jnp.maximum(kmax_ref[...], jnp.max(norm, axis=0, keepdims=True))
        kmax_ref[...] = k_run
        k_run_q = to_q_lanes(k_run)[0:1, :]
        v["shift"] = jnp.where(lane < FOX_HEADS, norm * k_run_q - FOX_SHIFT_MARGIN, 0.0)
        v["spread"] = norm * (k_run_q + to_q_lanes(norm))
        v["norm2"] = norm2

    def fox_operands():
        parts = lambda a: [p.astype(F32) for p in _split3(a)]
        f_parts, s_parts = parts(v["fsc"]), parts(-v["shift"])
        fcat = jnp.where(lane == FCAT_ONE_LANE, 1.0, 0.0)
        for j in range(3):
            fcat = fcat + (pltpu.roll(f_parts[j], j * FOX_HEADS, 1) if j else f_parts[j])
            fcat = fcat + pltpu.roll(s_parts[j], FCAT_SHIFT_LANE + j * FOX_HEADS, 1)
        aug = _dot(fcat.astype(BF16), sel_ref[...]).astype(BF16)
        qk, pairs = v["qk"], FOX_HEADS // 2
        for p in range(pairs):
            q_ref[0, p, :, 0:LANES] = qk[:, p * LANES:(p + 1) * LANES]
            q_ref[0, p, :, LANES:PAIR_LANES] = aug[:, p * LANES:(p + 1) * LANES]
            k_ref[0, p, :, 0:LANES] = qk[:, (pairs + p) * LANES:(pairs + p + 1) * LANES]
            k_ref[0, p, :, LANES:PAIR_LANES] = aug[:, (pairs + p) * LANES:(pairs + p + 1) * LANES]

    def fox_values():
        vt_ref[0] = _dot_nt(wt_ref[2 * fw:3 * fw, :].astype(BF16), hb_sc[...]).astype(BF16)

    def fox_stats():
        for sb in range(tm // STAT_BLOCK):
            rows = slice(sb * STAT_BLOCK, (sb + 1) * STAT_BLOCK)
            stat_ref[0, sb, 0:1, :] = jnp.max(v["norm2"][rows, :], axis=0, keepdims=True)
            stat_ref[0, sb, 1:2, :] = v["fsc"][rows, :][0:1, :]
            stat_ref[0, sb, 2:3, :] = v["fsc"][rows, :][STAT_BLOCK - 1:STAT_BLOCK, :]
            stat_ref[0, sb, 3:4, :] = jnp.max(v["spread"][rows, :], axis=0, keepdims=True)
            stat_ref[0, sb, 4:8, :] = jnp.zeros((4, LANES), F32)

    fox_stages = [fox_forget, fox_qk, fox_bounds, fox_operands, fox_values, fox_stats]

    hgrn_stages = _hgrn_chunk_stages(hq_sc, hk_sc, hi_sc, st_sc, b_sc, o_sc, tm) + [hgrn_output]
    for a, b in zip(hgrn_stages, fox_stages):
        a()
        b()

    @pl.when(jnp.logical_not(chunk_ok))
    def _():
        st_sc[...] = st_prev_sc[...]
        for r0 in range(0, tm, cs_rows):
            b_sc[r0:r0 + cs_rows, :] = _tri_cumsum(tri16_ref[...], hg_sc[r0:r0 + cs_rows, :])
        _hgrn_step_path(hq_sc, hk_sc, hi_sc, st_sc, b_sc, o_sc, tm)
        hgrn_output()


def _front(x, norm_g, w_in, fox_f_bias, lb_logits, hgrn_norm_g, layer, tm):
    b, s, d = x.shape
    fw, hw = FOX_WIDTH, HGRN_WIDTH
    assert w_in.shape == (d, 3 * fw + FOX_HEADS + 4 * hw)
    wt = w_in.T
    fb = jnp.pad(fox_f_bias.reshape(1, FOX_HEADS), ((0, 0), (0, LANES - FOX_HEADS)))

    sel = np.zeros((LANES, 2 * fw), np.float32)
    for h in range(FOX_HEADS):
        col = h * FOX_HEAD_DIM
        for j in range(3):
            sel[j * FOX_HEADS + h, col + AUG_F + j] = 1.0
            sel[FCAT_ONE_LANE, col + AUG_ONE + j] = 1.0
            sel[FCAT_ONE_LANE, fw + col + AUG_F + j] = 1.0
            sel[j * FOX_HEADS + h, fw + col + AUG_ONE + j] = -1.0
            sel[FCAT_SHIFT_LANE + j * FOX_HEADS + h, col + AUG_SHIFT + j] = 1.0
            sel[FCAT_ONE_LANE, fw + col + AUG_SHIFT + j] = 1.0
    sel = jnp.asarray(sel, BF16)
    fg = min(tm, LANES)
    tri = jnp.asarray(np.tril(np.ones((fg, fg), np.float32)), BF16)
    hsum = np.zeros((2 * fw, LANES), np.float32)
    hsum[np.arange(2 * fw), np.arange(2 * fw) // FOX_HEAD_DIM] = 1.0
    hsum = jnp.asarray(hsum, BF16)

    cs_rows = min(tm, 256)
    tri64, tri16 = _block_tri(cs_rows, HGRN_CHUNK), _block_tri(cs_rows, HGRN_STEP)

    const = _resident
    pairs = FOX_HEADS // 2
    head_out = jax.ShapeDtypeStruct((b, pairs, s, PAIR_LANES), BF16)
    head_spec = pl.BlockSpec((1, pairs, tm, PAIR_LANES), lambda bi, ti: (bi, 0, ti, 0))
    vm = pltpu.VMEM
    return pl.pallas_call(
        functools.partial(_front_kernel, layer=layer),
        grid=(b, s // tm),
        in_specs=[pl.BlockSpec((1, tm, d), lambda bi, ti: (bi, ti, 0)),
                  const((1, d)), const(wt.shape), const(fb.shape), const(lb_logits.shape), const(sel.shape),
                  const(tri.shape), const(hsum.shape), const((1, HGRN_DIM)), const(tri64.shape), const(tri16.shape)],
        out_specs=[head_spec, head_spec, pl.BlockSpec((1, fw, tm), lambda bi, ti: (bi, 0, ti)),
                   pl.BlockSpec((1, tm // STAT_BLOCK, 8, LANES), lambda bi, ti: (bi, ti, 0, 0)),
                   pl.BlockSpec((1, tm, hw), lambda bi, ti: (bi, ti, 0))],
        out_shape=[head_out, head_out, jax.ShapeDtypeStruct((b, fw, s), BF16),
                   jax.ShapeDtypeStruct((b, s // STAT_BLOCK, 8, LANES), F32),
                   jax.ShapeDtypeStruct((b, s, hw), BF16)],
        scratch_shapes=[vm((8, LANES), F32), vm((8, LANES), F32), vm((tm, d), BF16),
                        vm((1, tm, hw), F32), vm((1, tm, hw), F32), vm((tm, hw), F32),
                        vm((1, tm, hw), BF16), vm((tm, hw), BF16),
                        vm((HGRN_HEADS, HGRN_DIM, HGRN_DIM), F32), vm((HGRN_HEADS, HGRN_DIM, HGRN_DIM), F32),
                        vm((tm, hw), F32), vm((tm, hw), F32)],
        compiler_params=pltpu.CompilerParams(dimension_semantics=("arbitrary", "arbitrary"),
                                             vmem_limit_bytes=VMEM_LIMIT),
        name="front",
    )(x, norm_g.reshape(1, d), wt, fb, lb_logits, sel, tri, hsum, hgrn_norm_g.reshape(1, HGRN_DIM), tri64, tri16)


def _fox_first_block(stats, tq):
    b, nk = stats.shape[0], stats.shape[1]
    assert tq % STAT_BLOCK == 0, (tq, STAT_BLOCK)
    r = tq // STAT_BLOCK
    nq = nk // r
    h = FOX_HEADS
    kn = jnp.sqrt(stats[:, :, 0, h:2 * h])
    f_last = stats[:, :, 2, 0:h]
    by_query = stats.reshape(b, nq, r, 8, LANES)
    qn = jnp.sqrt(jnp.max(by_query[:, :, :, 0, 0:h], axis=2))
    kn_own = jnp.sqrt(jnp.max(by_query[:, :, :, 0, h:2 * h], axis=2))
    f_first = by_query[:, :, 0, 1, 0:h]
    bound = (qn[:, :, None, :] * (kn[:, None, :, :] + kn_own[:, :, None, :])
             + f_first[:, :, None, :] - f_last[:, None, :, :])
    key_blk = jnp.arange(nk)
    diag_start = jnp.arange(nq) * r
    need = (jnp.any(jnp.logical_not(bound <= -FOX_SKIP_LOG2), axis=-1)
            & (key_blk[None, None, :] < diag_start[None, :, None]))
    first = jnp.min(jnp.where(need, key_blk[None, None, :], nk), axis=-1)
    return jnp.minimum(first, diag_start[None, :]).astype(jnp.int32)


def _fox_direct_ok(stats, tq):
    b, nsb = stats.shape[0], stats.shape[1]
    spread = jnp.max(stats[:, :, 3, 0:FOX_HEADS].reshape(b, nsb * STAT_BLOCK // tq, -1), axis=-1)
    return (spread <= FOX_SPREAD_MAX).astype(jnp.int32)


def _fox_kernel(first_ref, direct_ref, q_ref, k_ref, vt_ref, o_ref, qm_sc, m_sc, l_sc, acc_sc, *, tq, hg):
    i = pl.program_id(2)
    hd = FOX_HEAD_DIM
    l_sc[...] = jnp.zeros_like(l_sc)
    acc_sc[...] = jnp.zeros_like(acc_sc)
    lane = lax.broadcasted_iota(jnp.int32, (tq, PAIR_LANES), 1)
    for hh in range(hg):
        own = (lane // hd) % 2 == hh % 2
        qm_sc[hh] = jnp.where(own, q_ref[0, hh // 2], jnp.zeros((), BF16))

    def accumulate_online(hh, st, off):
        m_prev = m_sc[hh]
        m_new = jnp.maximum(m_prev, jnp.max(st, axis=0, keepdims=True))
        alpha = jnp.exp2(m_prev - m_new)
        p = jnp.exp2(st - m_new)
        l_sc[hh] = alpha * l_sc[hh] + jnp.sum(p, axis=0, keepdims=True)
        vt = vt_ref[0, hh * hd:(hh + 1) * hd, pl.ds(off, st.shape[0])]
        acc_sc[hh] = alpha * acc_sc[hh] + _dot(vt, p.astype(BF16))
        m_sc[hh] = m_new

    def accumulate_direct(hh, st, off):
        p = jnp.exp2(st)
        l_sc[hh] += jnp.sum(p, axis=0, keepdims=True)
        acc_sc[hh] += _dot(vt_ref[0, hh * hd:(hh + 1) * hd, pl.ds(off, st.shape[0])], p.astype(BF16))

    def diagonal_direct():
        half = tq // 2
        off0 = pl.multiple_of(i * tq, tq)
        off1 = pl.multiple_of(i * tq + half, half)
        tri = (lax.broadcasted_iota(jnp.int32, (half, half), 0) <= lax.broadcasted_iota(jnp.int32, (half, half), 1))

        def scores(h_):
            top = _dot_nt(k_ref[0, h_ // 2, pl.ds(off0, half), :], qm_sc[h_])
            low = _dot_nt(k_ref[0, h_ // 2, pl.ds(off1, half), :], qm_sc[h_, half:tq, :])
            return top, low

        ahead = 2
        pending = [scores(h_) for h_ in range(min(ahead, hg))]
        for hh in range(hg):
            top, low = pending.pop(0)
            if hh + ahead < hg:
                pending.append(scores(hh + ahead))
            p_top = jnp.exp2(jnp.concatenate([jnp.where(tri, top[:, 0:half], -jnp.inf), top[:, half:tq]], axis=1))
            p_low = jnp.exp2(jnp.where(tri, low, -jnp.inf))
            rows = slice(hh * hd, (hh + 1) * hd)
            l_sc[hh] += jnp.sum(p_top, axis=0, keepdims=True)
            l_sc[hh, :, half:tq] += jnp.sum(p_low, axis=0, keepdims=True)
            acc_sc[hh] += _dot(vt_ref[0, rows, pl.ds(off0, half)], p_top.astype(BF16))
            acc_sc[hh, :, half:tq] += _dot(vt_ref[0, rows, pl.ds(off1, half)], p_low.astype(BF16))

    def sweep(accumulate, diagonal=None):
        def block(j, keys, masked):
            off = pl.multiple_of(j * keys, keys)
            scores = lambda h_: _dot_nt(k_ref[0, h_ // 2, pl.ds(off, keys), :], qm_sc[h_])
            ahead = 4
            pending = [scores(h_) for h_ in range(min(ahead, hg))]
            for hh in range(hg):
                st = pending.pop(0)
                if hh + ahead < hg:
                    pending.append(scores(hh + ahead))
                if masked:
                    key = lax.broadcasted_iota(jnp.int32, st.shape, 0)
                    qry = lax.broadcasted_iota(jnp.int32, st.shape, 1)
                    st = jnp.where(key <= qry, st, -jnp.inf)
                accumulate(hh, st, off)

        def body(j, carry):
            block(j, STAT_BLOCK, False)
            return carry

        lax.fori_loop(first_ref[pl.program_id(0), i], i * (tq // STAT_BLOCK), body, 0)
        if diagonal is None:
            block(i, tq, True)
        else:
            diagonal()

    direct = direct_ref[pl.program_id(0), i] != 0

    @pl.when(direct)
    def _():
        sweep(accumulate_direct, diagonal_direct)

    @pl.when(jnp.logical_not(direct))
    def _():
        m_sc[...] = jnp.full_like(m_sc, -jnp.inf)
        sweep(accumulate_online)

    for hh in range(hg):
        o_ref[0, hh * hd:(hh + 1) * hd, :] = (acc_sc[hh] / l_sc[hh]).astype(o_ref.dtype)


def _fox_attention(q, k, vt, first_block, direct_ok, tq, hg):
    b, pairs, s, _ = q.shape
    assert hg % 2 == 0
    groups = 2 * pairs // hg
    rows = hg * FOX_HEAD_DIM
    grid_spec = pltpu.PrefetchScalarGridSpec(
        num_scalar_prefetch=2,
        grid=(b, groups, s // tq),
        in_specs=[pl.BlockSpec((1, hg // 2, tq, PAIR_LANES), lambda bi, pi, qi, *_: (bi, pi, qi, 0)),
                  pl.BlockSpec((1, hg // 2, s, PAIR_LANES), lambda bi, pi, qi, *_: (bi, pi, 0, 0)),
                  pl.BlockSpec((1, rows, s), lambda bi, pi, qi, *_: (bi, pi, 0))],
        out_specs=pl.BlockSpec((1, rows, tq), lambda bi, pi, qi, *_: (bi, pi, qi)),
        scratch_shapes=[pltpu.VMEM((hg, tq, PAIR_LANES), BF16),
                        pltpu.VMEM((hg, 1, tq), F32), pltpu.VMEM((hg, 1, tq), F32),
                        pltpu.VMEM((hg, FOX_HEAD_DIM, tq), F32)])
    return pl.pallas_call(
        functools.partial(_fox_kernel, tq=tq, hg=hg),
        grid_spec=grid_spec,
        out_shape=jax.ShapeDtypeStruct((b, groups * rows, s), BF16),
        compiler_params=pltpu.CompilerParams(dimension_semantics=("arbitrary",) * 3,
                                             vmem_limit_bytes=VMEM_LIMIT),
        name="fox_attention",
    )(first_block, direct_ok, q, k, vt)


def _hgrn_chunk_stages(q_ref, k_ref, i_ref, st_sc, b_sc, o_sc, tc):
    c_len = HGRN_CHUNK
    n_chunks = tc // c_len
    t_idx = lax.broadcasted_iota(jnp.int32, (c_len, c_len), 0)
    s_idx = lax.broadcasted_iota(jnp.int32, (c_len, c_len), 1)
    causal = s_idx <= t_idx
    units = [(c, h) for c in range(n_chunks) for h in range(HGRN_HEADS)]
    rows = lambda c: slice(c * c_len, (c + 1) * c_len)
    lanes = lambda h: slice(h * HGRN_DIM, (h + 1) * HGRN_DIM)
    qe, ke, kl, decay, attn, intra, d_state, state = ({} for _ in range(8))

    def decay_operands():
        for c, h in units:
            b = b_sc[rows(c), lanes(h)]
            kk = k_ref[0, rows(c), lanes(h)]
            b_last = b[c_len - 1:c_len, :]
            qe[c, h] = (q_ref[0, rows(c), lanes(h)] * jnp.exp(b)).astype(BF16)
            ke[c, h] = (kk * jnp.exp(-b)).astype(BF16)
            kl[c, h] = (kk * jnp.exp(b_last - b)).astype(BF16)
            decay[c, h] = jnp.exp(b_last)

    def scores():
        for u in units:
            attn[u] = jnp.where(causal, _dot_nt(qe[u], ke[u]), 0.0).astype(BF16)

    def products():
        for c, h in units:
            intra[c, h] = _dot(attn[c, h], i_ref[0, rows(c), lanes(h)])
            d_state[c, h] = _dot_tn(i_ref[0, rows(c), lanes(h)], kl[c, h])

    def recurrence():
        for h in range(HGRN_HEADS):
            st = st_sc[h]
            for c in range(n_chunks):
                state[c, h] = st.astype(BF16)
                st = st * decay[c, h] + d_state[c, h]
            st_sc[h] = st

    def outputs():
        for c, h in units:
            o_sc[rows(c), lanes(h)] = intra[c, h] + _dot_nt(qe[c, h], state[c, h])

    return [decay_operands, scores, products, recurrence, outputs]


def _hgrn_step_path(q_ref, k_ref, i_ref, st_sc, b_sc, o_sc, tc):
    row8 = lax.broadcasted_iota(jnp.int32, (8, HGRN_DIM), 0)
    half = HGRN_STEP // 2

    def step(u, carry):
        base = pl.multiple_of(u * HGRN_STEP, HGRN_STEP)
        for h in range(HGRN_HEADS):
            ls = slice(h * HGRN_DIM, (h + 1) * HGRN_DIM)
            rows = pl.ds(base, HGRN_STEP)
            bq = b_sc[rows, ls]
            qq = q_ref[0, rows, ls]
            kk = k_ref[0, rows, ls]
            ii = i_ref[0, rows, ls].astype(F32)
            b_last = bq[HGRN_STEP - 1:HGRN_STEP, :]
            st = st_sc[h]
            inter = _dot_nt((qq * jnp.exp(bq)).astype(BF16), st.astype(BF16))
            k_dec = (kk * jnp.exp(b_last - bq)).astype(BF16)
            st_sc[h] = st * jnp.exp(b_last) + _dot_tn(ii.astype(BF16), k_dec)
            out = [inter[0:half], inter[half:HGRN_STEP]]
            qv = [qq[0:half], qq[half:HGRN_STEP]]
            bv = [bq[0:half], bq[half:HGRN_STEP]]
            for s_ in range(HGRN_STEP):
                k_s, b_s, i_s = kk[s_:s_ + 1, :], bq[s_:s_ + 1, :], ii[s_:s_ + 1, :]
                for v_ in range(s_ // half, 2):
                    diff = bv[v_] - b_s
                    if s_ > v_ * half:
                        diff = jnp.where(row8 + v_ * half >= s_, diff, -jnp.inf)
                    a = jnp.sum(qv[v_] * k_s * jnp.exp(diff), axis=-1, keepdims=True)
                    out[v_] = out[v_] + a * i_s
            o_sc[pl.ds(base, half), ls] = out[0]
            o_sc[pl.ds(base + half, half), ls] = out[1]
        return carry

    lax.fori_loop(0, tc // HGRN_STEP, step, 0)


def _block_tri(n, blk):
    return jnp.asarray(np.kron(np.eye(n // blk, dtype=np.float32), np.tril(np.ones((blk, blk), np.float32))), BF16)


def _memkv_kernel(mem_ref, g_ref, wkv_ref, wq_ref, wo_ref, sm_ref, om_ref):
    d = mem_ref.shape[-1]
    m = mem_ref.shape[1]
    hd = d // X_HEADS
    inv = 1.0 / math.sqrt(hd)
    kv = _dot(_rms(mem_ref[0], g_ref[...]).astype(BF16), wkv_ref[...].astype(BF16))
    for h in range(X_HEADS):
        cs = slice(h * hd, (h + 1) * hd)
        k_h = kv[:, cs].astype(BF16)
        v_h = kv[:, d + h * hd:d + (h + 1) * hd].astype(BF16)
        sm_ref[0, :, h * m:(h + 1) * m] = (_dot_nt(wq_ref[:, cs].astype(BF16), k_h) * inv).astype(BF16)
        om_ref[0, h * m:(h + 1) * m, :] = _dot(v_h, wo_ref[cs, :].astype(BF16)).astype(BF16)


def _mem_kv(mem, norm_g, w_kv, w_xq, w_xo):
    b, m, d = mem.shape
    return pl.pallas_call(
        _memkv_kernel,
        grid=(b,),
        in_specs=[pl.BlockSpec((1, m, d), lambda bi: (bi, 0, 0)), _resident((1, d)), _resident((d, 2 * d)),
                  _resident((d, d)), _resident((d, d))],
        out_specs=[pl.BlockSpec((1, d, X_HEADS * m), lambda bi: (bi, 0, 0)),
                   pl.BlockSpec((1, X_HEADS * m, d), lambda bi: (bi, 0, 0))],
        out_shape=[jax.ShapeDtypeStruct((b, d, X_HEADS * m), BF16), jax.ShapeDtypeStruct((b, X_HEADS * m, d), BF16)],
        compiler_params=pltpu.CompilerParams(dimension_semantics=("arbitrary",), vmem_limit_bytes=VMEM_LIMIT),
        name="mem_kv",
    )(mem, norm_g.reshape(1, d), w_kv, w_xq, w_xo)


def _mix_kernel(x_ref, foxt_ref, rec_ref, wo_ref, gx_ref, sm_ref, om_ref, o_ref, h1_sc, p_sc, *, sub):
    fw = foxt_ref.shape[1]
    n_mem = sm_ref.shape[2] // X_HEADS
    n_sub = x_ref.shape[1] // sub

    def out_proj(i):
        r = slice(i * sub, (i + 1) * sub)
        h1_sc[r, :] = (x_ref[0, r, :] + _dot_tn(foxt_ref[0, :, r], wo_ref[0:fw, :].astype(BF16))
                       + _dot(rec_ref[0, r, :], wo_ref[fw:, :].astype(BF16)))

    def attend(i):
        r = slice(i * sub, (i + 1) * sub)
        s = _dot(_rms(h1_sc[r, :], gx_ref[...]).astype(BF16), sm_ref[0])
        for h in range(X_HEADS):
            cs = slice(h * n_mem, (h + 1) * n_mem)
            p = jnp.exp(s[:, cs] - jnp.max(s[:, cs], axis=-1, keepdims=True))
            p_sc[r, cs] = (p / jnp.sum(p, axis=-1, keepdims=True)).astype(BF16)

    def finish(i):
        r = slice(i * sub, (i + 1) * sub)
        o_ref[0, r, :] = h1_sc[r, :] + _dot(p_sc[r, :], om_ref[0])

    stages = (out_proj, attend, finish)
    for t in range(n_sub + len(stages) - 1):
        for k, stage in enumerate(stages):
            if 0 <= t - k < n_sub:
                stage(t - k)


def _mix(x, foxt, rec, w_out, norm_g, score_m, out_m, tm):
    b, s, d = x.shape
    fw = foxt.shape[1]
    hm = score_m.shape[2]
    tok = lambda wd: pl.BlockSpec((1, tm, wd), lambda bi, ti: (bi, ti, 0))
    return pl.pallas_call(
        functools.partial(_mix_kernel, sub=min(tm, 256)),
        grid=(b, s // tm),
        in_specs=[tok(d), pl.BlockSpec((1, fw, tm), lambda bi, ti: (bi, 0, ti)), tok(rec.shape[-1]),
                  _resident((d, d)), _resident((1, d)),
                  pl.BlockSpec((1, d, hm), lambda bi, ti: (bi, 0, 0)), pl.BlockSpec((1, hm, d), lambda bi, ti: (bi, 0, 0))],
        out_specs=tok(d),
        out_shape=jax.ShapeDtypeStruct((b, s, d), F32),
        scratch_shapes=[pltpu.VMEM((tm, d), F32), pltpu.VMEM((tm, hm), BF16)],
        compiler_params=pltpu.CompilerParams(dimension_semantics=("arbitrary", "arbitrary"),
                                             vmem_limit_bytes=VMEM_LIMIT),
        name="mix",
    )(x, foxt, rec, w_out, norm_g.reshape(1, d), score_m, out_m)


def _mlp_kernel(h_ref, g_ref, w1_ref, w2_ref, gf_ref, o_ref, acc_sc, *, fc):
    hn = _rms(h_ref[...], g_ref[...]).astype(BF16)
    n_chunks = w1_ref.shape[1] // fc
    up = lambda c: _dot(hn, w1_ref[:, c * fc:(c + 1) * fc].astype(BF16))
    u_next = up(0)
    for c in range(n_chunks):
        u = jnp.maximum(u_next, 0.0)
        if c + 1 < n_chunks:
            u_next = up(c + 1)
        part = _dot((u * u).astype(BF16), w2_ref[c * fc:(c + 1) * fc, :].astype(BF16))
        if c == 0:
            acc_sc[...] = part
        else:
            acc_sc[...] += part
    o_ref[...] = _rms(h_ref[...] + acc_sc[...], gf_ref[...])


def _mlp(h, norm_g, w1, w2, final_g, tm, fc):
    t, d = h.shape
    dff = w1.shape[1]
    return pl.pallas_call(
        functools.partial(_mlp_kernel, fc=fc),
        grid=(t // tm,),
        in_specs=[pl.BlockSpec((tm, d), lambda ti: (ti, 0)), _resident((1, d)),
                  _resident((d, dff)), _resident((dff, d)), _resident((1, d))],
        out_specs=pl.BlockSpec((tm, d), lambda ti: (ti, 0)),
        out_shape=jax.ShapeDtypeStruct((t, d), F32),
        scratch_shapes=[pltpu.VMEM((tm, d), F32)],
        compiler_params=pltpu.CompilerParams(dimension_semantics=("arbitrary",), vmem_limit_bytes=VMEM_LIMIT),
        name="mlp",
    )(h, norm_g.reshape(1, d), w1, w2, final_g.reshape(1, d))


def _tile(n, want):
    t = min(n, want)
    assert n % t == 0, (n, want)
    return t


def _tiles(b, s, d_ff):
    return dict(
        front=_tile(s, 512),
        fox_q=_tile(s, 512),
        fox_heads=FOX_HEADS,
        mix=_tile(s, 1024),
        mlp=_tile(b * s, 512),
        mlp_ff=_tile(d_ff, 1024),
    )


def kernel(x, mem, norm_mix_g, w_in, fox_f_bias, hgrn_lb_logits, hgrn_norm_g, w_out, norm_x_g, norm_mem_g,
           w_xq, w_xkv, w_xo, norm_ff_g, w1, w2, final_norm_g):
    b, s, d = x.shape
    t = _tiles(b, s, w1.shape[-1])
    h = x
    for l in range(w_in.shape[0]):
        q, k, v, stats, rec = _front(h, norm_mix_g[l], w_in[l], fox_f_bias[l], hgrn_lb_logits, hgrn_norm_g[l], l,
                                     t["front"])
        fox = _fox_attention(q, k, v, _fox_first_block(stats, t["fox_q"]), _fox_direct_ok(stats, t["fox_q"]),
                             t["fox_q"], t["fox_heads"])
        score_m, out_m = _mem_kv(mem, norm_mem_g[l], w_xkv[l], w_xq[l], w_xo[l])
        h = _mix(h, fox, rec, w_out[l], norm_x_g[l], score_m, out_m, t["mix"])
        is_last = l == w_in.shape[0] - 1
        assert is_last, "the MLP kernel fuses the final norm, so it must be the last layer"
        h = _mlp(h.reshape(b * s, d), norm_ff_g[l], w1[l], w2[l], final_norm_g, t["mlp"], t["mlp_ff"]).reshape(b, s, d)
    return h
```

```python
import functools
import math

import jax
import jax.numpy as jnp
import numpy as np
from jax import lax
from jax.experimental import pallas as pl
from jax.experimental.pallas import tpu as pltpu

EPS = 1e-6
LOG2E = math.log2(math.e)
LANES = 128
FOX_HEADS = 8
FOX_HEAD_DIM = 64
FOX_WIDTH = FOX_HEADS * FOX_HEAD_DIM
HGRN_HEADS = 4
HGRN_DIM = 128
HGRN_WIDTH = HGRN_HEADS * HGRN_DIM
X_HEADS = 4
HGRN_STEP = 16
HGRN_CHUNK = 64
HGRN_MIN_CHUNK_LOG_DECAY = -60.0
VMEM_LIMIT = 56 * 1024 * 1024

PAIR_LANES = 2 * LANES
AUG_F = 0
AUG_ONE = 3
AUG_SHIFT = 6
FCAT_ONE_LANE = 24
FCAT_SHIFT_LANE = 32
NORM_HEADROOM = 1.0 + 2.0 ** -7
FOX_SHIFT_MARGIN = 60.0
FOX_SPREAD_MAX = 120.0
STAT_BLOCK = 256
FOX_SKIP_LOG2 = 70.0

BF16 = jnp.bfloat16
F32 = jnp.float32


def _dot(a, b):
    return jnp.dot(a, b, preferred_element_type=F32)


def _dot_nt(a, b):
    return lax.dot_general(a, b, (((1,), (1,)), ((), ())), preferred_element_type=F32)


def _dot_tn(a, b):
    return lax.dot_general(a, b, (((0,), (0,)), ((), ())), preferred_element_type=F32)


def _split3(v):
    hi = v.astype(BF16)
    r1 = v - hi.astype(F32)
    mid = r1.astype(BF16)
    lo = (r1 - mid.astype(F32)).astype(BF16)
    return hi, mid, lo


def _tri_cumsum(tri, v):
    hi, mid, lo = _split3(v)
    return _dot(tri, hi) + _dot(tri, mid) + _dot(tri, lo)


def _rms(x, g):
    ms = jnp.mean(x * x, axis=-1, keepdims=True)
    return x * lax.rsqrt(ms + EPS) * g


def _resident(shape):
    return pl.BlockSpec(shape, lambda *_: (0,) * len(shape), pipeline_mode=pl.Buffered(1))


def _front_kernel(x_ref, g_ref, wt_ref, fb_ref, lbl_ref, sel_ref, tri_ref, hsum_ref, ng_ref, tri64_ref, tri16_ref,
                  q_ref, k_ref, vt_ref, stat_ref, rec_ref,
                  carry_ref, kmax_ref, hb_sc, hq_sc, hk_sc, hg_sc, hi_sc, gate_sc, st_sc, st_prev_sc, b_sc, o_sc,
                  *, layer):
    @pl.when(pl.program_id(1) == 0)
    def _():
        carry_ref[...] = jnp.zeros_like(carry_ref)
        kmax_ref[...] = jnp.zeros_like(kmax_ref)
        st_sc[...] = jnp.zeros_like(st_sc)

    tm = x_ref.shape[1]
    hb_sc[...] = _rms(x_ref[0], g_ref[...]).astype(BF16)
    lane = lax.broadcasted_iota(jnp.int32, (tm, LANES), 1)
    fw, w = FOX_WIDTH, HGRN_WIDTH
    proj = lambda r0, n: _dot_nt(hb_sc[...], wt_ref[r0:r0 + n, :].astype(BF16))

    lbl = lbl_ref[...]
    e = jnp.exp(lbl - jnp.max(lbl, axis=0, keepdims=True))
    lb = jnp.sum(e[0:layer + 1, :], axis=0, keepdims=True) / jnp.sum(e, axis=0, keepdims=True)
    seg = lambda n: proj(3 * fw + FOX_HEADS + n * w, w)
    gf = seg(1)
    gq = seg(0)
    f = lb + (1.0 - lb) * jax.nn.sigmoid(gf)
    hk_sc[0] = 1.0 - f
    hg_sc[...] = jnp.log(f)
    gi = seg(2)
    hq_sc[0] = gq * jax.nn.sigmoid(gq)
    gg = seg(3)
    cs_rows = tri64_ref.shape[0]
    for r0 in range(0, tm, cs_rows):
        b_sc[r0:r0 + cs_rows, :] = _tri_cumsum(tri64_ref[...], hg_sc[r0:r0 + cs_rows, :])
    hi_sc[0] = gi.astype(BF16)
    gate_sc[...] = (gg * jax.nn.sigmoid(gg)).astype(BF16)
    chunk_ok = jnp.min(b_sc[...]) >= HGRN_MIN_CHUNK_LOG_DECAY
    st_prev_sc[...] = st_sc[...]

    def hgrn_output():
        for h in range(HGRN_HEADS):
            ls = slice(h * HGRN_DIM, (h + 1) * HGRN_DIM)
            rec_ref[0, :, ls] = (_rms(o_sc[:, ls], ng_ref[...]) * gate_sc[:, ls]).astype(rec_ref.dtype)

    v = {}

    def fox_forget():
        z = proj(3 * fw, LANES) + fb_ref[...]
        logf = jnp.minimum(z, 0.0) - jnp.log(1.0 + jnp.exp(-jnp.abs(z)))
        logf = jnp.where(lane < FOX_HEADS, logf, 0.0)
        grp = tri_ref.shape[0]
        total = carry_ref[0:1, :]
        groups = []
        for r0 in range(0, tm, grp):
            groups.append(_tri_cumsum(tri_ref[...], logf[r0:r0 + grp, :]) + total)
            total = groups[-1][grp - 1:grp, :]
        carry_ref[...] = jnp.broadcast_to(total, carry_ref.shape)
        v["fsc"] = jnp.concatenate(groups, axis=0) * LOG2E

    def fox_qk():
        q_scale = LOG2E / math.sqrt(FOX_HEAD_DIM)
        v["qk"] = jnp.concatenate([(proj(0, fw) * q_scale).astype(BF16), proj(fw, fw).astype(BF16)],
                                  axis=1)

    def fox_bounds():
        qkf = v["qk"].astype(F32)
        norm2 = _dot((qkf * qkf).astype(BF16), hsum_ref[...]) * NORM_HEADROOM
        norm = jnp.sqrt(norm2)
        to_q_lanes = lambda a: pltpu.roll(a, LANES - FOX_HEADS, 1)
        k_run = jnp.maximum(kmax_ref[...], jnp.max(norm, axis=0, keepdims=True))
        kmax_ref[...] = k_run
        k_run_q = to_q_lanes(k_run)[0:1, :]
        v["shift"] = jnp.where(lane < FOX_HEADS, norm * k_run_q - FOX_SHIFT_MARGIN, 0.0)
        v["spread"] = norm * (k_run_q + to_q_lanes(norm))
        v["norm2"] = norm2

    def fox_operands():
        parts = lambda a: [p.astype(F32) for p in _split3(a)]
        f_parts, s_parts = parts(v["fsc"]), parts(-v["shift"])
        fcat = jnp.where(lane == FCAT_ONE_LANE, 1.0, 0.0)
        for j in range(3):
            fcat = fcat + (pltpu.roll(f_parts[j], j * FOX_HEADS, 1) if j else f_parts[j])
            fcat = fcat + pltpu.roll(s_parts[j], FCAT_SHIFT_LANE + j * FOX_HEADS, 1)
        aug = _dot(fcat.astype(BF16), sel_ref[...]).astype(BF16)
        qk, pairs = v["qk"], FOX_HEADS // 2
        for p in range(pairs):
            q_ref[0, p, :, 0:LANES] = qk[:, p * LANES:(p + 1) * LANES]
            q_ref[0, p, :, LANES:PAIR_LANES] = aug[:, p * LANES:(p + 1) * LANES]
            k_ref[0, p, :, 0:LANES] = qk[:, (pairs + p) * LANES:(pairs + p + 1) * LANES]
            k_ref[0, p, :, LANES:PAIR_LANES] = aug[:, (pairs + p) * LANES:(pairs + p + 1) * LANES]

    def fox_values():
        vt_ref[0] = _dot_nt(wt_ref[2 * fw:3 * fw, :].astype(BF16), hb_sc[...]).astype(BF16)

    def fox_stats():
        for sb in range(tm // STAT_BLOCK):
            rows = slice(sb * STAT_BLOCK, (sb + 1) * STAT_BLOCK)
            stat_ref[0, sb, 0:1, :] = jnp.max(v["norm2"][rows, :], axis=0, keepdims=True)
            stat_ref[0, sb, 1:2, :] = v["fsc"][rows, :][0:1, :]
            stat_ref[0, sb, 2:3, :] = v["fsc"][rows, :][STAT_BLOCK - 1:STAT_BLOCK, :]
            stat_ref[0, sb, 3:4, :] = jnp.max(v["spread"][rows, :], axis=0, keepdims=True)
            stat_ref[0, sb, 4:8, :] = jnp.zeros((4, LANES), F32)

    fox_stages = [fox_forget, fox_qk, fox_bounds, fox_operands, fox_values, fox_stats]

    hgrn_stages = _hgrn_chunk_stages(hq_sc, hk_sc, hi_sc, st_sc, b_sc, o_sc, tm) + [hgrn_output]
    for a, b in zip(hgrn_stages, fox_stages):
        a()
        b()

    @pl.when(jnp.logical_not(chunk_ok))
    def _():
        st_sc[...] = st_prev_sc[...]
        for r0 in range(0, tm, cs_rows):
            b_sc[r0:r0 + cs_rows, :] = _tri_cumsum(tri16_ref[...], hg_sc[r0:r0 + cs_rows, :])
        _hgrn_step_path(hq_sc, hk_sc, hi_sc, st_sc, b_sc, o_sc, tm)
        hgrn_output()


def _front(x, norm_g, w_in, fox_f_bias, lb_logits, hgrn_norm_g, layer, tm):
    b, s, d = x.shape
    fw, hw = FOX_WIDTH, HGRN_WIDTH
    assert w_in.shape == (d, 3 * fw + FOX_HEADS + 4 * hw)
    wt = w_in.T
    fb = jnp.pad(fox_f_bias.reshape(1, FOX_HEADS), ((0, 0), (0, LANES - FOX_HEADS)))

    sel = np.zeros((LANES, 2 * fw), np.float32)
    for h in range(FOX_HEADS):
        col = h * FOX_HEAD_DIM
        for j in range(3):
            sel[j * FOX_HEADS + h, col + AUG_F + j] = 1.0
            sel[FCAT_ONE_LANE, col + AUG_ONE + j] = 1.0
            sel[FCAT_ONE_LANE, fw + col + AUG_F + j] = 1.0
            sel[j * FOX_HEADS + h, fw + col + AUG_ONE + j] = -1.0
            sel[FCAT_SHIFT_LANE + j * FOX_HEADS + h, col + AUG_SHIFT + j] = 1.0
            sel[FCAT_ONE_LANE, fw + col + AUG_SHIFT + j] = 1.0
    sel = jnp.asarray(sel, BF16)
    fg = min(tm, LANES)
    tri = jnp.asarray(np.tril(np.ones((fg, fg), np.float32)), BF16)
    hsum = np.zeros((2 * fw, LANES), np.float32)
    hsum[np.arange(2 * fw), np.arange(2 * fw) // FOX_HEAD_DIM] = 1.0
    hsum = jnp.asarray(hsum, BF16)

    cs_rows = min(tm, 256)
    tri64, tri16 = _block_tri(cs_rows, HGRN_CHUNK), _block_tri(cs_rows, HGRN_STEP)

    const = _resident
    pairs = FOX_HEADS // 2
    head_out = jax.ShapeDtypeStruct((b, pairs, s, PAIR_LANES), BF16)
    head_spec = pl.BlockSpec((1, pairs, tm, PAIR_LANES), lambda bi, ti: (bi, 0, ti, 0))
    vm = pltpu.VMEM
    return pl.pallas_call(
        functools.partial(_front_kernel, layer=layer),
        grid=(b, s // tm),
        in_specs=[pl.BlockSpec((1, tm, d), lambda bi, ti: (bi, ti, 0)),
                  const((1, d)), const(wt.shape), const(fb.shape), const(lb_logits.shape), const(sel.shape),
                  const(tri.shape), const(hsum.shape), const((1, HGRN_DIM)), const(tri64.shape), const(tri16.shape)],
        out_specs=[head_spec, head_spec, pl.BlockSpec((1, fw, tm), lambda bi, ti: (bi, 0, ti)),
                   pl.BlockSpec((1, tm // STAT_BLOCK, 8, LANES), lambda bi, ti: (bi, ti, 0, 0)),
                   pl.BlockSpec((1, tm, hw), lambda bi, ti: (bi, ti, 0))],
        out_shape=[head_out, head_out, jax.ShapeDtypeStruct((b, fw, s), BF16),
                   jax.ShapeDtypeStruct((b, s // STAT_BLOCK, 8, LANES), F32),
                   jax.ShapeDtypeStruct((b, s, hw), BF16)],
        scratch_shapes=[vm((8, LANES), F32), vm((8, LANES), F32), vm((tm, d), BF16),
                        vm((1, tm, hw), F32), vm((1, tm, hw), F32), vm((tm, hw), F32),
                        vm((1, tm, hw), BF16), vm((tm, hw), BF16),
                        vm((HGRN_HEADS, HGRN_DIM, HGRN_DIM), F32), vm((HGRN_HEADS, HGRN_DIM, HGRN_DIM), F32),
                        vm((tm, hw), F32), vm((tm, hw), F32)],
        compiler_params=pltpu.CompilerParams(dimension_semantics=("arbitrary", "arbitrary"),
                                             vmem_limit_bytes=VMEM_LIMIT),
        name="front",
    )(x, norm_g.reshape(1, d), wt, fb, lb_logits, sel, tri, hsum, hgrn_norm_g.reshape(1, HGRN_DIM), tri64, tri16)


def _fox_first_block(stats, tq):
    b, nk = stats.shape[0], stats.shape[1]
    assert tq % STAT_BLOCK == 0, (tq, STAT_BLOCK)
    r = tq // STAT_BLOCK
    nq = nk // r
    h = FOX_HEADS
    kn = jnp.sqrt(stats[:, :, 0, h:2 * h])
    f_last = stats[:, :, 2, 0:h]
    by_query = stats.reshape(b, nq, r, 8, LANES)
    qn = jnp.sqrt(jnp.max(by_query[:, :, :, 0, 0:h], axis=2))
    kn_own = jnp.sqrt(jnp.max(by_query[:, :, :, 0, h:2 * h], axis=2))
    f_first = by_query[:, :, 0, 1, 0:h]
    bound = (qn[:, :, None, :] * (kn[:, None, :, :] + kn_own[:, :, None, :])
             + f_first[:, :, None, :] - f_last[:, None, :, :])
    key_blk = jnp.arange(nk)
    diag_start = jnp.arange(nq) * r
    need = (jnp.any(jnp.logical_not(bound <= -FOX_SKIP_LOG2), axis=-1)
            & (key_blk[None, None, :] < diag_start[None, :, None]))
    first = jnp.min(jnp.where(need, key_blk[None, None, :], nk), axis=-1)
    return jnp.minimum(first, diag_start[None, :]).astype(jnp.int32)


def _fox_direct_ok(stats, tq):
    b, nsb = stats.shape[0], stats.shape[1]
    spread = jnp.max(stats[:, :, 3, 0:FOX_HEADS].reshape(b, nsb * STAT_BLOCK // tq, -1), axis=-1)
    return (spread <= FOX_SPREAD_MAX).astype(jnp.int32)


def _fox_kernel(first_ref, direct_ref, q_ref, k_ref, vt_ref, o_ref, qm_sc, m_sc, l_sc, acc_sc, *, tq, hg):
    i = pl.program_id(2)
    hd = FOX_HEAD_DIM
    l_sc[...] = jnp.zeros_like(l_sc)
    acc_sc[...] = jnp.zeros_like(acc_sc)
    lane = lax.broadcasted_iota(jnp.int32, (tq, PAIR_LANES), 1)
    for hh in range(hg):
        own = (lane // hd) % 2 == hh % 2
        qm_sc[hh] = jnp.where(own, q_ref[0, hh // 2], jnp.zeros((), BF16))

    def accumulate_online(hh, st, off):
        m_prev = m_sc[hh]
        m_new = jnp.maximum(m_prev, jnp.max(st, axis=0, keepdims=True))
        alpha = jnp.exp2(m_prev - m_new)
        p = jnp.exp2(st - m_new)
        l_sc[hh] = alpha * l_sc[hh] + jnp.sum(p, axis=0, keepdims=True)
        vt = vt_ref[0, hh * hd:(hh + 1) * hd, pl.ds(off, st.shape[0])]
        acc_sc[hh] = alpha * acc_sc[hh] + _dot(vt, p.astype(BF16))
        m_sc[hh] = m_new

    def accumulate_direct(hh, st, off):
        p = jnp.exp2(st)
        l_sc[hh] += jnp.sum(p, axis=0, keepdims=True)
        acc_sc[hh] += _dot(vt_ref[0, hh * hd:(hh + 1) * hd, pl.ds(off, st.shape[0])], p.astype(BF16))

    def diagonal_direct():
        half = tq // 2
        off0 = pl.multiple_of(i * tq, tq)
        off1 = pl.multiple_of(i * tq + half, half)
        tri = (lax.broadcasted_iota(jnp.int32, (half, half), 0) <= lax.broadcasted_iota(jnp.int32, (half, half), 1))

        def scores(h_):
            top = _dot_nt(k_ref[0, h_ // 2, pl.ds(off0, half), :], qm_sc[h_])
            low = _dot_nt(k_ref[0, h_ // 2, pl.ds(off1, half), :], qm_sc[h_, half:tq, :])
            return top, low

        ahead = 2
        pending = [scores(h_) for h_ in range(min(ahead, hg))]
        for hh in range(hg):
            top, low = pending.pop(0)
            if hh + ahead < hg:
                pending.append(scores(hh + ahead))
            p_top = jnp.exp2(jnp.concatenate([jnp.where(tri, top[:, 0:half], -jnp.inf), top[:, half:tq]], axis=1))
            p_low = jnp.exp2(jnp.where(tri, low, -jnp.inf))
            rows = slice(hh * hd, (hh + 1) * hd)
            l_sc[hh] += jnp.sum(p_top, axis=0, keepdims=True)
            l_sc[hh, :, half:tq] += jnp.sum(p_low, axis=0, keepdims=True)
            acc_sc[hh] += _dot(vt_ref[0, rows, pl.ds(off0, half)], p_top.astype(BF16))
            acc_sc[hh, :, half:tq] += _dot(vt_ref[0, rows, pl.ds(off1, half)], p_low.astype(BF16))

    def sweep(accumulate, diagonal=None):
        def block(j, keys, masked):
            off = pl.multiple_of(j * keys, keys)
            scores = lambda h_: _dot_nt(k_ref[0, h_ // 2, pl.ds(off, keys), :], qm_sc[h_])
            ahead = 4
            pending = [scores(h_) for h_ in range(min(ahead, hg))]
            for hh in range(hg):
                st = pending.pop(0)
                if hh + ahead < hg:
                    pending.append(scores(hh + ahead))
                if masked:
                    key = lax.broadcasted_iota(jnp.int32, st.shape, 0)
                    qry = lax.broadcasted_iota(jnp.int32, st.shape, 1)
                    st = jnp.where(key <= qry, st, -jnp.inf)
                accumulate(hh, st, off)

        def body(j, carry):
            block(j, STAT_BLOCK, False)
            return carry

        lax.fori_loop(first_ref[pl.program_id(0), i], i * (tq // STAT_BLOCK), body, 0)
        if diagonal is None:
            block(i, tq, True)
        else:
            diagonal()

    direct = direct_ref[pl.program_id(0), i] != 0

    @pl.when(direct)
    def _():
        sweep(accumulate_direct, diagonal_direct)

    @pl.when(jnp.logical_not(direct))
    def _():
        m_sc[...] = jnp.full_like(m_sc, -jnp.inf)
        sweep(accumulate_online)

    for hh in range(hg):
        o_ref[0, hh * hd:(hh + 1) * hd, :] = (acc_sc[hh] / l_sc[hh]).astype(o_ref.dtype)


def _fox_attention(q, k, vt, first_block, direct_ok, tq, hg):
    b, pairs, s, _ = q.shape
    assert hg % 2 == 0
    groups = 2 * pairs // hg
    rows = hg * FOX_HEAD_DIM
    grid_spec = pltpu.PrefetchScalarGridSpec(
        num_scalar_prefetch=2,
        grid=(b, groups, s // tq),
        in_specs=[pl.BlockSpec((1, hg // 2, tq, PAIR_LANES), lambda bi, pi, qi, *_: (bi, pi, qi, 0)),
                  pl.BlockSpec((1, hg // 2, s, PAIR_LANES), lambda bi, pi, qi, *_: (bi, pi, 0, 0)),
                  pl.BlockSpec((1, rows, s), lambda bi, pi, qi, *_: (bi, pi, 0))],
        out_specs=pl.BlockSpec((1, rows, tq), lambda bi, pi, qi, *_: (bi, pi, qi)),
        scratch_shapes=[pltpu.VMEM((hg, tq, PAIR_LANES), BF16),
                        pltpu.VMEM((hg, 1, tq), F32), pltpu.VMEM((hg, 1, tq), F32),
                        pltpu.VMEM((hg, FOX_HEAD_DIM, tq), F32)])
    return pl.pallas_call(
        functools.partial(_fox_kernel, tq=tq, hg=hg),
        grid_spec=grid_spec,
        out_shape=jax.ShapeDtypeStruct((b, groups * rows, s), BF16),
        compiler_params=pltpu.CompilerParams(dimension_semantics=("arbitrary",) * 3,
                                             vmem_limit_bytes=VMEM_LIMIT),
        name="fox_attention",
    )(first_block, direct_ok, q, k, vt)


def _hgrn_chunk_stages(q_ref, k_ref, i_ref, st_sc, b_sc, o_sc, tc):
    c_len = HGRN_CHUNK
    n_chunks = tc // c_len
    t_idx = lax.broadcasted_iota(jnp.int32, (c_len, c_len), 0)
    s_idx = lax.broadcasted_iota(jnp.int32, (c_len, c_len), 1)
    causal = s_idx <= t_idx
    units = [(c, h) for c in range(n_chunks) for h in range(HGRN_HEADS)]
    rows = lambda c: slice(c * c_len, (c + 1) * c_len)
    lanes = lambda h: slice(h * HGRN_DIM, (h + 1) * HGRN_DIM)
    qe, ke, kl, decay, attn, intra, d_state, state = ({} for _ in range(8))

    def decay_operands():
        for c, h in units:
            b = b_sc[rows(c), lanes(h)]
            kk = k_ref[0, rows(c), lanes(h)]
            b_last = b[c_len - 1:c_len, :]
            qe[c, h] = (q_ref[0, rows(c), lanes(h)] * jnp.exp(b)).astype(BF16)
            ke[c, h] = (kk * jnp.exp(-b)).astype(BF16)
            kl[c, h] = (kk * jnp.exp(b_last - b)).astype(BF16)
            decay[c, h] = jnp.exp(b_last)

    def scores():
        for u in units:
            attn[u] = jnp.where(causal, _dot_nt(qe[u], ke[u]), 0.0).astype(BF16)

    def products():
        for c, h in units:
            intra[c, h] = _dot(attn[c, h], i_ref[0, rows(c), lanes(h)])
            d_state[c, h] = _dot_tn(i_ref[0, rows(c), lanes(h)], kl[c, h])

    def recurrence():
        for h in range(HGRN_HEADS):
            st = st_sc[h]
            for c in range(n_chunks):
                state[c, h] = st.astype(BF16)
                st = st * decay[c, h] + d_state[c, h]
            st_sc[h] = st

    def outputs():
        for c, h in units:
            o_sc[rows(c), lanes(h)] = intra[c, h] + _dot_nt(qe[c, h], state[c, h])

    return [decay_operands, scores, products, recurrence, outputs]


def _hgrn_step_path(q_ref, k_ref, i_ref, st_sc, b_sc, o_sc, tc):
    row8 = lax.broadcasted_iota(jnp.int32, (8, HGRN_DIM), 0)
    half = HGRN_STEP // 2

    def step(u, carry):
        base = pl.multiple_of(u * HGRN_STEP, HGRN_STEP)
        for h in range(HGRN_HEADS):
            ls = slice(h * HGRN_DIM, (h + 1) * HGRN_DIM)
            rows = pl.ds(base, HGRN_STEP)
            bq = b_sc[rows, ls]
            qq = q_ref[0, rows, ls]
            kk = k_ref[0, rows, ls]
            ii = i_ref[0, rows, ls].astype(F32)
            b_last = bq[HGRN_STEP - 1:HGRN_STEP, :]
            st = st_sc[h]
            inter = _dot_nt((qq * jnp.exp(bq)).astype(BF16), st.astype(BF16))
            k_dec = (kk * jnp.exp(b_last - bq)).astype(BF16)
            st_sc[h] = st * jnp.exp(b_last) + _dot_tn(ii.astype(BF16), k_dec)
            out = [inter[0:half], inter[half:HGRN_STEP]]
            qv = [qq[0:half], qq[half:HGRN_STEP]]
            bv = [bq[0:half], bq[half:HGRN_STEP]]
            for s_ in range(HGRN_STEP):
                k_s, b_s, i_s = kk[s_:s_ + 1, :], bq[s_:s_ + 1, :], ii[s_:s_ + 1, :]
                for v_ in range(s_ // half, 2):
                    diff = bv[v_] - b_s
                    if s_ > v_ * half:
                        diff = jnp.where(row8 + v_ * half >= s_, diff, -jnp.inf)
                    a = jnp.sum(qv[v_] * k_s * jnp.exp(diff), axis=-1, keepdims=True)
                    out[v_] = out[v_] + a * i_s
            o_sc[pl.ds(base, half), ls] = out[0]
            o_sc[pl.ds(base + half, half), ls] = out[1]
        return carry

    lax.fori_loop(0, tc // HGRN_STEP, step, 0)


def _block_tri(n, blk):
    return jnp.asarray(np.kron(np.eye(n // blk, dtype=np.float32), np.tril(np.ones((blk, blk), np.float32))), BF16)


def _memkv_kernel(mem_ref, g_ref, wkv_ref, wq_ref, wo_ref, sm_ref, om_ref):
    d = mem_ref.shape[-1]
    m = mem_ref.shape[1]
    hd = d // X_HEADS
    inv = 1.0 / math.sqrt(hd)
    kv = _dot(_rms(mem_ref[0], g_ref[...]).astype(BF16), wkv_ref[...].astype(BF16))
    for h in range(X_HEADS):
        cs = slice(h * hd, (h + 1) * hd)
        k_h = kv[:, cs].astype(BF16)
        v_h = kv[:, d + h * hd:d + (h + 1) * hd].astype(BF16)
        sm_ref[0, :, h * m:(h + 1) * m] = (_dot_nt(wq_ref[:, cs].astype(BF16), k_h) * inv).astype(BF16)
        om_ref[0, h * m:(h + 1) * m, :] = _dot(v_h, wo_ref[cs, :].astype(BF16)).astype(BF16)


def _mem_kv(mem, norm_g, w_kv, w_xq, w_xo):
    b, m, d = mem.shape
    return pl.pallas_call(
        _memkv_kernel,
        grid=(b,),
        in_specs=[pl.BlockSpec((1, m, d), lambda bi: (bi, 0, 0)), _resident((1, d)), _resident((d, 2 * d)),
                  _resident((d, d)), _resident((d, d))],
        out_specs=[pl.BlockSpec((1, d, X_HEADS * m), lambda bi: (bi, 0, 0)),
                   pl.BlockSpec((1, X_HEADS * m, d), lambda bi: (bi, 0, 0))],
        out_shape=[jax.ShapeDtypeStruct((b, d, X_HEADS * m), BF16), jax.ShapeDtypeStruct((b, X_HEADS * m, d), BF16)],
        compiler_params=pltpu.CompilerParams(dimension_semantics=("arbitrary",), vmem_limit_bytes=VMEM_LIMIT),
        name="mem_kv",
    )(mem, norm_g.reshape(1, d), w_kv, w_xq, w_xo)


def _mix_kernel(x_ref, foxt_ref, rec_ref, wo_ref, gx_ref, sm_ref, om_ref, w1_ref, w2_ref,
                o_ref, w1b_ref, w2b_ref, h1_sc, p_sc, *, sub):
    w1b_ref[...] = w1_ref[...].astype(BF16)
    w2b_ref[...] = w2_ref[...].astype(BF16)
    fw = foxt_ref.shape[1]
    n_mem = sm_ref.shape[2] // X_HEADS
    n_sub = x_ref.shape[1] // sub

    def out_proj(i):
        r = slice(i * sub, (i + 1) * sub)
        h1_sc[r, :] = (x_ref[0, r, :] + _dot_tn(foxt_ref[0, :, r], wo_ref[0:fw, :].astype(BF16))
                       + _dot(rec_ref[0, r, :], wo_ref[fw:, :].astype(BF16)))

    def attend(i):
        r = slice(i * sub, (i + 1) * sub)
        s = _dot(_rms(h1_sc[r, :], gx_ref[...]).astype(BF16), sm_ref[0])
        for h in range(X_HEADS):
            cs = slice(h * n_mem, (h + 1) * n_mem)
            p = jnp.exp(s[:, cs] - jnp.max(s[:, cs], axis=-1, keepdims=True))
            p_sc[r, cs] = (p / jnp.sum(p, axis=-1, keepdims=True)).astype(BF16)

    def finish(i):
        r = slice(i * sub, (i + 1) * sub)
        o_ref[0, r, :] = h1_sc[r, :] + _dot(p_sc[r, :], om_ref[0])

    stages = (out_proj, attend, finish)
    for t in range(n_sub + len(stages) - 1):
        for k, stage in enumerate(stages):
            if 0 <= t - k < n_sub:
                stage(t - k)


def _mix(x, foxt, rec, w_out, norm_g, score_m, out_m, w1, w2, tm):
    b, s, d = x.shape
    fw = foxt.shape[1]
    hm = score_m.shape[2]
    nt = s // tm
    r1, r2 = _tile(w1.shape[0], w1.shape[0] // (b * nt)), _tile(w2.shape[0], w2.shape[0] // (b * nt))
    assert r1 * b * nt == w1.shape[0] and r2 * b * nt == w2.shape[0] and r1 % 16 == 0 and r2 % 16 == 0
    tok = lambda wd: pl.BlockSpec((1, tm, wd), lambda bi, ti: (bi, ti, 0))
    slab = lambda rows, cols: pl.BlockSpec((rows, cols), lambda bi, ti: (bi * nt + ti, 0))
    return pl.pallas_call(
        functools.partial(_mix_kernel, sub=min(tm, 256)),
        grid=(b, nt),
        in_specs=[tok(d), pl.BlockSpec((1, fw, tm), lambda bi, ti: (bi, 0, ti)), tok(rec.shape[-1]),
                  _resident((d, d)), _resident((1, d)),
                  pl.BlockSpec((1, d, hm), lambda bi, ti: (bi, 0, 0)), pl.BlockSpec((1, hm, d), lambda bi, ti: (bi, 0, 0)),
                  slab(r1, w1.shape[1]), slab(r2, w2.shape[1])],
        out_specs=[tok(d), slab(r1, w1.shape[1]), slab(r2, w2.shape[1])],
        out_shape=[jax.ShapeDtypeStruct((b, s, d), F32), jax.ShapeDtypeStruct(w1.shape, BF16),
                   jax.ShapeDtypeStruct(w2.shape, BF16)],
        scratch_shapes=[pltpu.VMEM((tm, d), F32), pltpu.VMEM((tm, hm), BF16)],
        compiler_params=pltpu.CompilerParams(dimension_semantics=("arbitrary", "arbitrary"),
                                             vmem_limit_bytes=VMEM_LIMIT),
        name="mix",
    )(x, foxt, rec, w_out, norm_g.reshape(1, d), score_m, out_m, w1, w2)


def _mlp_kernel(h_ref, g_ref, w1_ref, w2_ref, gf_ref, o_ref, acc_sc, *, fc):
    hn = _rms(h_ref[...], g_ref[...]).astype(BF16)
    n_chunks = w1_ref.shape[1] // fc
    up = lambda c: _dot(hn, w1_ref[:, c * fc:(c + 1) * fc])
    u_next = up(0)
    for c in range(n_chunks):
        u = jnp.maximum(u_next, 0.0)
        if c + 1 < n_chunks:
            u_next = up(c + 1)
        part = _dot((u * u).astype(BF16), w2_ref[c * fc:(c + 1) * fc, :])
        if c == 0:
            acc_sc[...] = part
        else:
            acc_sc[...] += part
    o_ref[...] = _rms(h_ref[...] + acc_sc[...], gf_ref[...])


def _mlp(h, norm_g, w1, w2, final_g, tm, fc):
    t, d = h.shape
    dff = w1.shape[1]
    return pl.pallas_call(
        functools.partial(_mlp_kernel, fc=fc),
        grid=(t // tm,),
        in_specs=[pl.BlockSpec((tm, d), lambda ti: (ti, 0)), _resident((1, d)),
                  _resident((d, dff)), _resident((dff, d)), _resident((1, d))],
        out_specs=pl.BlockSpec((tm, d), lambda ti: (ti, 0)),
        out_shape=jax.ShapeDtypeStruct((t, d), F32),
        scratch_shapes=[pltpu.VMEM((tm, d), F32)],
        compiler_params=pltpu.CompilerParams(dimension_semantics=("arbitrary",), vmem_limit_bytes=VMEM_LIMIT),
        name="mlp",
    )(h, norm_g.reshape(1, d), w1, w2, final_g.reshape(1, d))


def _tile(n, want):
    t = min(n, want)
    assert n % t == 0, (n, want)
    return t


def _tiles(b, s, d_ff):
    return dict(
        front=_tile(s, 512),
        fox_q=_tile(s, 512),
        fox_heads=FOX_HEADS,
        mix=_tile(s, 1024),
        mlp=_tile(b * s, 1024),
        mlp_ff=_tile(d_ff, 1024),
    )


def kernel(x, mem, norm_mix_g, w_in, fox_f_bias, hgrn_lb_logits, hgrn_norm_g, w_out, norm_x_g, norm_mem_g,
           w_xq, w_xkv, w_xo, norm_ff_g, w1, w2, final_norm_g):
    b, s, d = x.shape
    t = _tiles(b, s, w1.shape[-1])
    h = x
    for l in range(w_in.shape[0]):
        q, k, v, stats, rec = _front(h, norm_mix_g[l], w_in[l], fox_f_bias[l], hgrn_lb_logits, hgrn_norm_g[l], l,
                                     t["front"])
        fox = _fox_attention(q, k, v, _fox_first_block(stats, t["fox_q"]), _fox_direct_ok(stats, t["fox_q"]),
                             t["fox_q"], t["fox_heads"])
        score_m, out_m = _mem_kv(mem, norm_mem_g[l], w_xkv[l], w_xq[l], w_xo[l])
        h, w1_b, w2_b = _mix(h, fox, rec, w_out[l], norm_x_g[l], score_m, out_m, w1[l], w2[l], t["mix"])
        is_last = l == w_in.shape[0] - 1
        assert is_last, "the MLP kernel fuses the final norm, so it must be the last layer"
        h = _mlp(h.reshape(b * s, d), norm_ff_g[l], w1_b, w2_b, final_norm_g, t["mlp"], t["mlp_ff"]).reshape(b, s, d)
    return h
```

```python
import functools
import math

import jax
import jax.numpy as jnp
import numpy as np
from jax import lax
from jax.experimental import pallas as pl
from jax.experimental.pallas import tpu as pltpu

EPS = 1e-6
LOG2E = math.log2(math.e)
LANES = 128
FOX_HEADS = 8
FOX_HEAD_DIM = 64
FOX_WIDTH = FOX_HEADS * FOX_HEAD_DIM
HGRN_HEADS = 4
HGRN_DIM = 128
HGRN_WIDTH = HGRN_HEADS * HGRN_DIM
X_HEADS = 4
HGRN_STEP = 16
HGRN_CHUNK = 64
HGRN_MIN_CHUNK_LOG_DECAY = -60.0
VMEM_LIMIT = 56 * 1024 * 1024

PAIR_LANES = 2 * LANES
AUG_F = 0
AUG_ONE = 3
AUG_SHIFT = 6
FCAT_ONE_LANE = 24
FCAT_SHIFT_LANE = 32
NORM_HEADROOM = 1.0 + 2.0 ** -7
FOX_SHIFT_MARGIN = 60.0
FOX_SPREAD_MAX = 120.0
STAT_BLOCK = 256
FOX_SKIP_LOG2 = 70.0

BF16 = jnp.bfloat16
F32 = jnp.float32


def _dot(a, b):
    return jnp.dot(a, b, preferred_element_type=F32)


def _dot_nt(a, b):
    return lax.dot_general(a, b, (((1,), (1,)), ((), ())), preferred_element_type=F32)


def _dot_tn(a, b):
    return lax.dot_general(a, b, (((0,), (0,)), ((), ())), preferred_element_type=F32)


def _split3(v):
    hi = v.astype(BF16)
    r1 = v - hi.astype(F32)
    mid = r1.astype(BF16)
    lo = (r1 - mid.astype(F32)).astype(BF16)
    return hi, mid, lo


def _tri_cumsum(tri, v):
    hi, mid, lo = _split3(v)
    return _dot(tri, hi) + _dot(tri, mid) + _dot(tri, lo)


def _rms(x, g):
    ms = jnp.mean(x * x, axis=-1, keepdims=True)
    return x * lax.rsqrt(ms + EPS) * g


def _resident(shape):
    return pl.BlockSpec(shape, lambda *_: (0,) * len(shape), pipeline_mode=pl.Buffered(1))


def _front_kernel(x_ref, g_ref, wt_ref, fb_ref, lbl_ref, sel_ref, tri_ref, hsum_ref, ng_ref, tri64_ref, tri16_ref,
                  q_ref, k_ref, vt_ref, stat_ref, rec_ref,
                  carry_ref, kmax_ref, hb_sc, hq_sc, hk_sc, hg_sc, hi_sc, gate_sc, st_sc, st_prev_sc, b_sc, o_sc,
                  *, layer):
    @pl.when(pl.program_id(1) == 0)
    def _():
        carry_ref[...] = jnp.zeros_like(carry_ref)
        kmax_ref[...] = jnp.zeros_like(kmax_ref)
        st_sc[...] = jnp.zeros_like(st_sc)

    tm = x_ref.shape[1]
    hb_sc[...] = _rms(x_ref[0], g_ref[...]).astype(BF16)
    lane = lax.broadcasted_iota(jnp.int32, (tm, LANES), 1)
    fw, w = FOX_WIDTH, HGRN_WIDTH
    proj = lambda r0, n: _dot_nt(hb_sc[...], wt_ref[r0:r0 + n, :].astype(BF16))

    lbl = lbl_ref[...]
    e = jnp.exp(lbl - jnp.max(lbl, axis=0, keepdims=True))
    lb = jnp.sum(e[0:layer + 1, :], axis=0, keepdims=True) / jnp.sum(e, axis=0, keepdims=True)
    seg = lambda n: proj(3 * fw + FOX_HEADS + n * w, w)
    gf = seg(1)
    gq = seg(0)
    f = lb + (1.0 - lb) * jax.nn.sigmoid(gf)
    hk_sc[0] = 1.0 - f
    hg_sc[...] = jnp.log(f)
    gi = seg(2)
    hq_sc[0] = gq * jax.nn.sigmoid(gq)
    gg = seg(3)
    cs_rows = tri64_ref.shape[0]
    for r0 in range(0, tm, cs_rows):
        b_sc[r0:r0 + cs_rows, :] = _tri_cumsum(tri64_ref[...], hg_sc[r0:r0 + cs_rows, :])
    hi_sc[0] = gi.astype(BF16)
    gate_sc[...] = (gg * jax.nn.sigmoid(gg)).astype(BF16)
    chunk_ok = jnp.min(b_sc[...]) >= HGRN_MIN_CHUNK_LOG_DECAY
    st_prev_sc[...] = st_sc[...]

    def hgrn_output():
        for h in range(HGRN_HEADS):
            ls = slice(h * HGRN_DIM, (h + 1) * HGRN_DIM)
            rec_ref[0, :, ls] = (_rms(o_sc[:, ls], ng_ref[...]) * gate_sc[:, ls]).astype(rec_ref.dtype)

    v = {}

    def fox_forget():
        z = proj(3 * fw, LANES) + fb_ref[...]
        logf = jnp.minimum(z, 0.0) - jnp.log(1.0 + jnp.exp(-jnp.abs(z)))
        logf = jnp.where(lane < FOX_HEADS, logf, 0.0)
        grp = tri_ref.shape[0]
        total = carry_ref[0:1, :]
        groups = []
        for r0 in range(0, tm, grp):
            groups.append(_tri_cumsum(tri_ref[...], logf[r0:r0 + grp, :]) + total)
            total = groups[-1][grp - 1:grp, :]
        carry_ref[...] = jnp.broadcast_to(total, carry_ref.shape)
        v["fsc"] = jnp.concatenate(groups, axis=0) * LOG2E

    def fox_qk():
        q_scale = LOG2E / math.sqrt(FOX_HEAD_DIM)
        v["qk"] = jnp.concatenate([(proj(0, fw) * q_scale).astype(BF16), proj(fw, fw).astype(BF16)],
                                  axis=1)

    def fox_bounds():
        qkf = v["qk"].astype(F32)
        norm2 = _dot((qkf * qkf).astype(BF16), hsum_ref[...]) * NORM_HEADROOM
        norm = jnp.sqrt(norm2)
        to_q_lanes = lambda a: pltpu.roll(a, LANES - FOX_HEADS, 1)
        k_run = jnp.maximum(kmax_ref[...], jnp.max(norm, axis=0, keepdims=True))
        kmax_ref[...] = k_run
        k_run_q = to_q_lanes(k_run)[0:1, :]
        v["shift"] = jnp.where(lane < FOX_HEADS, norm * k_run_q - FOX_SHIFT_MARGIN, 0.0)
        v["spread"] = norm * (k_run_q + to_q_lanes(norm))
        v["norm2"] = norm2

    def fox_operands():
        parts = lambda a: [p.astype(F32) for p in _split3(a)]
        f_parts, s_parts = parts(v["fsc"]), parts(-v["shift"])
        fcat = jnp.where(lane == FCAT_ONE_LANE, 1.0, 0.0)
        for j in range(3):
            fcat = fcat + (pltpu.roll(f_parts[j], j * FOX_HEADS, 1) if j else f_parts[j])
            fcat = fcat + pltpu.roll(s_parts[j], FCAT_SHIFT_LANE + j * FOX_HEADS, 1)
        aug = _dot(fcat.astype(BF16), sel_ref[...]).astype(BF16)
        qk, pairs = v["qk"], FOX_HEADS // 2
        for p in range(pairs):
            q_ref[0, p, :, 0:LANES] = qk[:, p * LANES:(p + 1) * LANES]
            q_ref[0, p, :, LANES:PAIR_LANES] = aug[:, p * LANES:(p + 1) * LANES]
            k_ref[0, p, :, 0:LANES] = qk[:, (pairs + p) * LANES:(pairs + p + 1) * LANES]
            k_ref[0, p, :, LANES:PAIR_LANES] = aug[:, (pairs + p) * LANES:(pairs + p + 1) * LANES]

    def fox_values():
        vt_ref[0] = _dot_nt(wt_ref[2 * fw:3 * fw, :].astype(BF16), hb_sc[...]).astype(BF16)

    def fox_stats():
        for sb in range(tm // STAT_BLOCK):
            rows = slice(sb * STAT_BLOCK, (sb + 1) * STAT_BLOCK)
            stat_ref[0, sb, 0:1, :] = jnp.max(v["norm2"][rows, :], axis=0, keepdims=True)
            stat_ref[0, sb, 1:2, :] = v["fsc"][rows, :][0:1, :]
            stat_ref[0, sb, 2:3, :] = v["fsc"][rows, :][STAT_BLOCK - 1:STAT_BLOCK, :]
            stat_ref[0, sb, 3:4, :] = jnp.max(v["spread"][rows, :], axis=0, keepdims=True)
            stat_ref[0, sb, 4:8, :] = jnp.zeros((4, LANES), F32)

    fox_stages = [fox_forget, fox_qk, fox_bounds, fox_operands, fox_values, fox_stats]

    hgrn_stages = _hgrn_chunk_stages(hq_sc, hk_sc, hi_sc, st_sc, b_sc, o_sc, tm) + [hgrn_output]
    for a, b in zip(hgrn_stages, fox_stages):
        a()
        b()

    @pl.when(jnp.logical_not(chunk_ok))
    def _():
        st_sc[...] = st_prev_sc[...]
        for r0 in range(0, tm, cs_rows):
            b_sc[r0:r0 + cs_rows, :] = _tri_cumsum(tri16_ref[...], hg_sc[r0:r0 + cs_rows, :])
        _hgrn_step_path(hq_sc, hk_sc, hi_sc, st_sc, b_sc, o_sc, tm)
        hgrn_output()


def _front(x, norm_g, w_in, fox_f_bias, lb_logits, hgrn_norm_g, layer, tm):
    b, s, d = x.shape
    fw, hw = FOX_WIDTH, HGRN_WIDTH
    assert w_in.shape == (d, 3 * fw + FOX_HEADS + 4 * hw)
    wt = w_in.T
    fb = jnp.pad(fox_f_bias.reshape(1, FOX_HEADS), ((0, 0), (0, LANES - FOX_HEADS)))

    sel = np.zeros((LANES, 2 * fw), np.float32)
    for h in range(FOX_HEADS):
        col = h * FOX_HEAD_DIM
        for j in range(3):
            sel[j * FOX_HEADS + h, col + AUG_F + j] = 1.0
            sel[FCAT_ONE_LANE, col + AUG_ONE + j] = 1.0
            sel[FCAT_ONE_LANE, fw + col + AUG_F + j] = 1.0
            sel[j * FOX_HEADS + h, fw + col + AUG_ONE + j] = -1.0
            sel[FCAT_SHIFT_LANE + j * FOX_HEADS + h, col + AUG_SHIFT + j] = 1.0
            sel[FCAT_ONE_LANE, fw + col + AUG_SHIFT + j] = 1.0
    sel = jnp.asarray(sel, BF16)
    fg = min(tm, LANES)
    tri = jnp.asarray(np.tril(np.ones((fg, fg), np.float32)), BF16)
    hsum = np.zeros((2 * fw, LANES), np.float32)
    hsum[np.arange(2 * fw), np.arange(2 * fw) // FOX_HEAD_DIM] = 1.0
    hsum = jnp.asarray(hsum, BF16)

    cs_rows = min(tm, 256)
    tri64, tri16 = _block_tri(cs_rows, HGRN_CHUNK), _block_tri(cs_rows, HGRN_STEP)

    const = _resident
    pairs = FOX_HEADS // 2
    head_out = jax.ShapeDtypeStruct((b, pairs, s, PAIR_LANES), BF16)
    head_spec = pl.BlockSpec((1, pairs, tm, PAIR_LANES), lambda bi, ti: (bi, 0, ti, 0))
    vm = pltpu.VMEM
    return pl.pallas_call(
        functools.partial(_front_kernel, layer=layer),
        grid=(b, s // tm),
        in_specs=[pl.BlockSpec((1, tm, d), lambda bi, ti: (bi, ti, 0)),
                  const((1, d)), const(wt.shape), const(fb.shape), const(lb_logits.shape), const(sel.shape),
                  const(tri.shape), const(hsum.shape), const((1, HGRN_DIM)), const(tri64.shape), const(tri16.shape)],
        out_specs=[head_spec, head_spec, pl.BlockSpec((1, fw, tm), lambda bi, ti: (bi, 0, ti)),
                   pl.BlockSpec((1, tm // STAT_BLOCK, 8, LANES), lambda bi, ti: (bi, ti, 0, 0)),
                   pl.BlockSpec((1, tm, hw), lambda bi, ti: (bi, ti, 0))],
        out_shape=[head_out, head_out, jax.ShapeDtypeStruct((b, fw, s), BF16),
                   jax.ShapeDtypeStruct((b, s // STAT_BLOCK, 8, LANES), F32),
                   jax.ShapeDtypeStruct((b, s, hw), BF16)],
        scratch_shapes=[vm((8, LANES), F32), vm((8, LANES), F32), vm((tm, d), BF16),
                        vm((1, tm, hw), F32), vm((1, tm, hw), F32), vm((tm, hw), F32),
                        vm((1, tm, hw), BF16), vm((tm, hw), BF16),
                        vm((HGRN_HEADS, HGRN_DIM, HGRN_DIM), F32), vm((HGRN_HEADS, HGRN_DIM, HGRN_DIM), F32),
                        vm((tm, hw), F32), vm((tm, hw), F32)],
        compiler_params=pltpu.CompilerParams(dimension_semantics=("arbitrary", "arbitrary"),
                                             vmem_limit_bytes=VMEM_LIMIT),
        name="front",
    )(x, norm_g.reshape(1, d), wt, fb, lb_logits, sel, tri, hsum, hgrn_norm_g.reshape(1, HGRN_DIM), tri64, tri16)


def _fox_first_block(stats, tq):
    b, nk = stats.shape[0], stats.shape[1]
    assert tq % STAT_BLOCK == 0, (tq, STAT_BLOCK)
    r = tq // STAT_BLOCK
    nq = nk // r
    h = FOX_HEADS
    kn = jnp.sqrt(stats[:, :, 0, h:2 * h])
    f_last = stats[:, :, 2, 0:h]
    by_query = stats.reshape(b, nq, r, 8, LANES)
    qn = jnp.sqrt(jnp.max(by_query[:, :, :, 0, 0:h], axis=2))
    kn_own = jnp.sqrt(jnp.max(by_query[:, :, :, 0, h:2 * h], axis=2))
    f_first = by_query[:, :, 0, 1, 0:h]
    bound = (qn[:, :, None, :] * (kn[:, None, :, :] + kn_own[:, :, None, :])
             + f_first[:, :, None, :] - f_last[:, None, :, :])
    key_blk = jnp.arange(nk)
    diag_start = jnp.arange(nq) * r
    need = (jnp.any(jnp.logical_not(bound <= -FOX_SKIP_LOG2), axis=-1)
            & (key_blk[None, None, :] < diag_start[None, :, None]))
    first = jnp.min(jnp.where(need, key_blk[None, None, :], nk), axis=-1)
    return jnp.minimum(first, diag_start[None, :]).astype(jnp.int32)


def _fox_direct_ok(stats, tq):
    b, nsb = stats.shape[0], stats.shape[1]
    spread = jnp.max(stats[:, :, 3, 0:FOX_HEADS].reshape(b, nsb * STAT_BLOCK // tq, -1), axis=-1)
    return (spread <= FOX_SPREAD_MAX).astype(jnp.int32)


def _fox_kernel(first_ref, direct_ref, q_ref, k_ref, vt_ref, o_ref, qm_sc, m_sc, l_sc, acc_sc, *, tq, hg):
    i = pl.program_id(2)
    hd = FOX_HEAD_DIM
    l_sc[...] = jnp.zeros_like(l_sc)
    acc_sc[...] = jnp.zeros_like(acc_sc)
    lane = lax.broadcasted_iota(jnp.int32, (tq, PAIR_LANES), 1)
    for hh in range(hg):
        own = (lane // hd) % 2 == hh % 2
        qm_sc[hh] = jnp.where(own, q_ref[0, hh // 2], jnp.zeros((), BF16))

    def accumulate_online(hh, st, off):
        m_prev = m_sc[hh]
        m_new = jnp.maximum(m_prev, jnp.max(st, axis=0, keepdims=True))
        alpha = jnp.exp2(m_prev - m_new)
        p = jnp.exp2(st - m_new)
        l_sc[hh] = alpha * l_sc[hh] + jnp.sum(p, axis=0, keepdims=True)
        vt = vt_ref[0, hh * hd:(hh + 1) * hd, pl.ds(off, st.shape[0])]
        acc_sc[hh] = alpha * acc_sc[hh] + _dot(vt, p.astype(BF16))
        m_sc[hh] = m_new

    def accumulate_direct(hh, st, off):
        p = jnp.exp2(st)
        l_sc[hh] += jnp.sum(p, axis=0, keepdims=True)
        acc_sc[hh] += _dot(vt_ref[0, hh * hd:(hh + 1) * hd, pl.ds(off, st.shape[0])], p.astype(BF16))

    def diagonal_direct():
        half = tq // 2
        off0 = pl.multiple_of(i * tq, tq)
        off1 = pl.multiple_of(i * tq + half, half)
        tri = (lax.broadcasted_iota(jnp.int32, (half, half), 0) <= lax.broadcasted_iota(jnp.int32, (half, half), 1))

        def scores(h_):
            top = _dot_nt(k_ref[0, h_ // 2, pl.ds(off0, half), :], qm_sc[h_])
            low = _dot_nt(k_ref[0, h_ // 2, pl.ds(off1, half), :], qm_sc[h_, half:tq, :])
            return top, low

        ahead = 2
        pending = [scores(h_) for h_ in range(min(ahead, hg))]
        for hh in range(hg):
            top, low = pending.pop(0)
            if hh + ahead < hg:
                pending.append(scores(hh + ahead))
            p_top = jnp.exp2(jnp.concatenate([jnp.where(tri, top[:, 0:half], -jnp.inf), top[:, half:tq]], axis=1))
            p_low = jnp.exp2(jnp.where(tri, low, -jnp.inf))
            rows = slice(hh * hd, (hh + 1) * hd)
            l_sc[hh] += jnp.sum(p_top, axis=0, keepdims=True)
            l_sc[hh, :, half:tq] += jnp.sum(p_low, axis=0, keepdims=True)
            acc_sc[hh] += _dot(vt_ref[0, rows, pl.ds(off0, half)], p_top.astype(BF16))
            acc_sc[hh, :, half:tq] += _dot(vt_ref[0, rows, pl.ds(off1, half)], p_low.astype(BF16))

    def sweep(accumulate, diagonal=None):
        def block(j, keys, masked):
            off = pl.multiple_of(j * keys, keys)
            scores = lambda h_: _dot_nt(k_ref[0, h_ // 2, pl.ds(off, keys), :], qm_sc[h_])
            ahead = 4
            pending = [scores(h_) for h_ in range(min(ahead, hg))]
            for hh in range(hg):
                st = pending.pop(0)
                if hh + ahead < hg:
                    pending.append(scores(hh + ahead))
                if masked:
                    key = lax.broadcasted_iota(jnp.int32, st.shape, 0)
                    qry = lax.broadcasted_iota(jnp.int32, st.shape, 1)
                    st = jnp.where(key <= qry, st, -jnp.inf)
                accumulate(hh, st, off)

        def body(j, carry):
            block(j, STAT_BLOCK, False)
            return carry

        lax.fori_loop(first_ref[pl.program_id(0), i], i * (tq // STAT_BLOCK), body, 0)
        if diagonal is None:
            block(i, tq, True)
        else:
            diagonal()

    direct = direct_ref[pl.program_id(0), i] != 0

    @pl.when(direct)
    def _():
        sweep(accumulate_direct, diagonal_direct)

    @pl.when(jnp.logical_not(direct))
    def _():
        m_sc[...] = jnp.full_like(m_sc, -jnp.inf)
        sweep(accumulate_online)

    for hh in range(hg):
        o_ref[0, hh * hd:(hh + 1) * hd, :] = (acc_sc[hh] / l_sc[hh]).astype(o_ref.dtype)


def _fox_attention(q, k, vt, first_block, direct_ok, tq, hg):
    b, pairs, s, _ = q.shape
    assert hg % 2 == 0
    groups = 2 * pairs // hg
    rows = hg * FOX_HEAD_DIM
    grid_spec = pltpu.PrefetchScalarGridSpec(
        num_scalar_prefetch=2,
        grid=(b, groups, s // tq),
        in_specs=[pl.BlockSpec((1, hg // 2, tq, PAIR_LANES), lambda bi, pi, qi, *_: (bi, pi, qi, 0)),
                  pl.BlockSpec((1, hg // 2, s, PAIR_LANES), lambda bi, pi, qi, *_: (bi, pi, 0, 0)),
                  pl.BlockSpec((1, rows, s), lambda bi, pi, qi, *_: (bi, pi, 0))],
        out_specs=pl.BlockSpec((1, rows, tq), lambda bi, pi, qi, *_: (bi, pi, qi)),
        scratch_shapes=[pltpu.VMEM((hg, tq, PAIR_LANES), BF16),
                        pltpu.VMEM((hg, 1, tq), F32), pltpu.VMEM((hg, 1, tq), F32),
                        pltpu.VMEM((hg, FOX_HEAD_DIM, tq), F32)])
    return pl.pallas_call(
        functools.partial(_fox_kernel, tq=tq, hg=hg),
        grid_spec=grid_spec,
        out_shape=jax.ShapeDtypeStruct((b, groups * rows, s), BF16),
        compiler_params=pltpu.CompilerParams(dimension_semantics=("arbitrary",) * 3,
                                             vmem_limit_bytes=VMEM_LIMIT),
        name="fox_attention",
    )(first_block, direct_ok, q, k, vt)


def _hgrn_chunk_stages(q_ref, k_ref, i_ref, st_sc, b_sc, o_sc, tc):
    c_len = HGRN_CHUNK
    n_chunks = tc // c_len
    t_idx = lax.broadcasted_iota(jnp.int32, (c_len, c_len), 0)
    s_idx = lax.broadcasted_iota(jnp.int32, (c_len, c_len), 1)
    causal = s_idx <= t_idx
    units = [(c, h) for c in range(n_chunks) for h in range(HGRN_HEADS)]
    rows = lambda c: slice(c * c_len, (c + 1) * c_len)
    lanes = lambda h: slice(h * HGRN_DIM, (h + 1) * HGRN_DIM)
    qe, ke, kl, decay, attn, intra, d_state, state = ({} for _ in range(8))

    def decay_operands():
        for c, h in units:
            b = b_sc[rows(c), lanes(h)]
            kk = k_ref[0, rows(c), lanes(h)]
            b_last = b[c_len - 1:c_len, :]
            qe[c, h] = (q_ref[0, rows(c), lanes(h)] * jnp.exp(b)).astype(BF16)
            ke[c, h] = (kk * jnp.exp(-b)).astype(BF16)
            kl[c, h] = (kk * jnp.exp(b_last - b)).astype(BF16)
            decay[c, h] = jnp.exp(b_last)

    def scores():
        for u in units:
            attn[u] = jnp.where(causal, _dot_nt(qe[u], ke[u]), 0.0).astype(BF16)

    def products():
        for c, h in units:
            intra[c, h] = _dot(attn[c, h], i_ref[0, rows(c), lanes(h)])
            d_state[c, h] = _dot_tn(i_ref[0, rows(c), lanes(h)], kl[c, h])

    def recurrence():
        for h in range(HGRN_HEADS):
            st = st_sc[h]
            for c in range(n_chunks):
                state[c, h] = st.astype(BF16)
                st = st * decay[c, h] + d_state[c, h]
            st_sc[h] = st

    def outputs():
        for c, h in units:
            o_sc[rows(c), lanes(h)] = intra[c, h] + _dot_nt(qe[c, h], state[c, h])

    return [decay_operands, scores, products, recurrence, outputs]


def _hgrn_step_path(q_ref, k_ref, i_ref, st_sc, b_sc, o_sc, tc):
    row8 = lax.broadcasted_iota(jnp.int32, (8, HGRN_DIM), 0)
    half = HGRN_STEP // 2

    def step(u, carry):
        base = pl.multiple_of(u * HGRN_STEP, HGRN_STEP)
        for h in range(HGRN_HEADS):
            ls = slice(h * HGRN_DIM, (h + 1) * HGRN_DIM)
            rows = pl.ds(base, HGRN_STEP)
            bq = b_sc[rows, ls]
            qq = q_ref[0, rows, ls]
            kk = k_ref[0, rows, ls]
            ii = i_ref[0, rows, ls].astype(F32)
            b_last = bq[HGRN_STEP - 1:HGRN_STEP, :]
            st = st_sc[h]
            inter = _dot_nt((qq * jnp.exp(bq)).astype(BF16), st.astype(BF16))
            k_dec = (kk * jnp.exp(b_last - bq)).astype(BF16)
            st_sc[h] = st * jnp.exp(b_last) + _dot_tn(ii.astype(BF16), k_dec)
            out = [inter[0:half], inter[half:HGRN_STEP]]
            qv = [qq[0:half], qq[half:HGRN_STEP]]
            bv = [bq[0:half], bq[half:HGRN_STEP]]
            for s_ in range(HGRN_STEP):
                k_s, b_s, i_s = kk[s_:s_ + 1, :], bq[s_:s_ + 1, :], ii[s_:s_ + 1, :]
                for v_ in range(s_ // half, 2):
                    diff = bv[v_] - b_s
                    if s_ > v_ * half:
                        diff = jnp.where(row8 + v_ * half >= s_, diff, -jnp.inf)
                    a = jnp.sum(qv[v_] * k_s * jnp.exp(diff), axis=-1, keepdims=True)
                    out[v_] = out[v_] + a * i_s
            o_sc[pl.ds(base, half), ls] = out[0]
            o_sc[pl.ds(base + half, half), ls] = out[1]
        return carry

    lax.fori_loop(0, tc // HGRN_STEP, step, 0)


def _block_tri(n, blk):
    return jnp.asarray(np.kron(np.eye(n // blk, dtype=np.float32), np.tril(np.ones((blk, blk), np.float32))), BF16)


def _memkv_kernel(mem_ref, g_ref, wkv_ref, wq_ref, wo_ref, sm_ref, om_ref):
    d = mem_ref.shape[-1]
    m = mem_ref.shape[1]
    hd = d // X_HEADS
    inv = 1.0 / math.sqrt(hd)
    kv = _dot(_rms(mem_ref[0], g_ref[...]).astype(BF16), wkv_ref[...].astype(BF16))
    for h in range(X_HEADS):
        cs = slice(h * hd, (h + 1) * hd)
        k_h = kv[:, cs].astype(BF16)
        v_h = kv[:, d + h * hd:d + (h + 1) * hd].astype(BF16)
        sm_ref[0, :, h * m:(h + 1) * m] = (_dot_nt(wq_ref[:, cs].astype(BF16), k_h) * inv).astype(BF16)
        om_ref[0, h * m:(h + 1) * m, :] = _dot(v_h, wo_ref[cs, :].astype(BF16)).astype(BF16)


def _mem_kv(mem, norm_g, w_kv, w_xq, w_xo):
    b, m, d = mem.shape
    return pl.pallas_call(
        _memkv_kernel,
        grid=(b,),
        in_specs=[pl.BlockSpec((1, m, d), lambda bi: (bi, 0, 0)), _resident((1, d)), _resident((d, 2 * d)),
                  _resident((d, d)), _resident((d, d))],
        out_specs=[pl.BlockSpec((1, d, X_HEADS * m), lambda bi: (bi, 0, 0)),
                   pl.BlockSpec((1, X_HEADS * m, d), lambda bi: (bi, 0, 0))],
        out_shape=[jax.ShapeDtypeStruct((b, d, X_HEADS * m), BF16), jax.ShapeDtypeStruct((b, X_HEADS * m, d), BF16)],
        compiler_params=pltpu.CompilerParams(dimension_semantics=("arbitrary",), vmem_limit_bytes=VMEM_LIMIT),
        name="mem_kv",
    )(mem, norm_g.reshape(1, d), w_kv, w_xq, w_xo)


def _mix_kernel(x_ref, foxt_ref, rec_ref, wo_ref, gx_ref, sm_ref, om_ref, o_ref, h1_sc, p_sc, *, sub):
    fw = foxt_ref.shape[1]
    n_mem = sm_ref.shape[2] // X_HEADS
    n_sub = x_ref.shape[1] // sub

    def out_proj(i):
        r = slice(i * sub, (i + 1) * sub)
        h1_sc[r, :] = (x_ref[0, r, :] + _dot_tn(foxt_ref[0, :, r], wo_ref[0:fw, :].astype(BF16))
                       + _dot(rec_ref[0, r, :], wo_ref[fw:, :].astype(BF16)))

    def attend(i):
        r = slice(i * sub, (i + 1) * sub)
        s = _dot(_rms(h1_sc[r, :], gx_ref[...]).astype(BF16), sm_ref[0])
        for h in range(X_HEADS):
            cs = slice(h * n_mem, (h + 1) * n_mem)
            p = jnp.exp(s[:, cs] - jnp.max(s[:, cs], axis=-1, keepdims=True))
            p_sc[r, cs] = (p / jnp.sum(p, axis=-1, keepdims=True)).astype(BF16)

    def finish(i):
        r = slice(i * sub, (i + 1) * sub)
        o_ref[0, r, :] = h1_sc[r, :] + _dot(p_sc[r, :], om_ref[0])

    stages = (out_proj, attend, finish)
    for t in range(n_sub + len(stages) - 1):
        for k, stage in enumerate(stages):
            if 0 <= t - k < n_sub:
                stage(t - k)


def _mix(x, foxt, rec, w_out, norm_g, score_m, out_m, tm):
    b, s, d = x.shape
    fw = foxt.shape[1]
    hm = score_m.shape[2]
    tok = lambda wd: pl.BlockSpec((1, tm, wd), lambda bi, ti: (bi, ti, 0))
    return pl.pallas_call(
        functools.partial(_mix_kernel, sub=min(tm, 256)),
        grid=(b, s // tm),
        in_specs=[tok(d), pl.BlockSpec((1, fw, tm), lambda bi, ti: (bi, 0, ti)), tok(rec.shape[-1]),
                  _resident((d, d)), _resident((1, d)),
                  pl.BlockSpec((1, d, hm), lambda bi, ti: (bi, 0, 0)), pl.BlockSpec((1, hm, d), lambda bi, ti: (bi, 0, 0))],
        out_specs=tok(d),
        out_shape=jax.ShapeDtypeStruct((b, s, d), F32),
        scratch_shapes=[pltpu.VMEM((tm, d), F32), pltpu.VMEM((tm, hm), BF16)],
        compiler_params=pltpu.CompilerParams(dimension_semantics=("arbitrary", "arbitrary"),
                                             vmem_limit_bytes=VMEM_LIMIT),
        name="mix",
    )(x, foxt, rec, w_out, norm_g.reshape(1, d), score_m, out_m)


def _mlp_kernel(h_ref, g_ref, w1_ref, w2_ref, gf_ref, o_ref, acc_sc, *, fc):
    hn = _rms(h_ref[...], g_ref[...]).astype(BF16)
    n_chunks = w1_ref.shape[1] // fc
    up = lambda c: _dot(hn, w1_ref[:, c * fc:(c + 1) * fc].astype(BF16))
    u_next = up(0)
    for c in range(n_chunks):
        u = jnp.maximum(u_next, 0.0)
        if c + 1 < n_chunks:
            u_next = up(c + 1)
        part = _dot((u * u).astype(BF16), w2_ref[c * fc:(c + 1) * fc, :].astype(BF16))
        if c == 0:
            acc_sc[...] = part
        else:
            acc_sc[...] += part
    o_ref[...] = _rms(h_ref[...] + acc_sc[...], gf_ref[...])


def _mlp(h, norm_g, w1, w2, final_g, tm, fc):
    t, d = h.shape
    dff = w1.shape[1]
    return pl.pallas_call(
        functools.partial(_mlp_kernel, fc=fc),
        grid=(t // tm,),
        in_specs=[pl.BlockSpec((tm, d), lambda ti: (ti, 0)), _resident((1, d)),
                  _resident((d, dff)), _resident((dff, d)), _resident((1, d))],
        out_specs=pl.BlockSpec((tm, d), lambda ti: (ti, 0)),
        out_shape=jax.ShapeDtypeStruct((t, d), F32),
        scratch_shapes=[pltpu.VMEM((tm, d), F32)],
        compiler_params=pltpu.CompilerParams(dimension_semantics=("arbitrary",), vmem_limit_bytes=VMEM_LIMIT),
        name="mlp",
    )(h, norm_g.reshape(1, d), w1, w2, final_g.reshape(1, d))


def _tile(n, want):
    t = min(n, want)
    assert n % t == 0, (n, want)
    return t


def _tiles(b, s, d_ff):
    return dict(
        front=_tile(s, 512),
        fox_q=_tile(s, 512),
        fox_heads=FOX_HEADS,
        mix=_tile(s, 1024),
        mlp=_tile(b * s, 512),
        mlp_ff=_tile(d_ff, 1024),
    )


def kernel(x, mem, norm_mix_g, w_in, fox_f_bias, hgrn_lb_logits, hgrn_norm_g, w_out, norm_x_g, norm_mem_g,
           w_xq, w_xkv, w_xo, norm_ff_g, w1, w2, final_norm_g):
    b, s, d = x.shape
    t = _tiles(b, s, w1.shape[-1])
    h = x
    for l in range(w_in.shape[0]):
        q, k, v, stats, rec = _front(h, norm_mix_g[l], w_in[l], fox_f_bias[l], hgrn_lb_logits, hgrn_norm_g[l], l,
                                     t["front"])
        fox = _fox_attention(q, k, v, _fox_first_block(stats, t["fox_q"]), _fox_direct_ok(stats, t["fox_q"]),
                             t["fox_q"], t["fox_heads"])
        score_m, out_m = _mem_kv(mem, norm_mem_g[l], w_xkv[l], w_xq[l], w_xo[l])
        h = _mix(h, fox, rec, w_out[l], norm_x_g[l], score_m, out_m, t["mix"])
        is_last = l == w_in.shape[0] - 1
        assert is_last, "the MLP kernel fuses the final norm, so it must be the last layer"
        h = _mlp(h.reshape(b * s, d), norm_ff_g[l], w1[l], w2[l], final_norm_g, t["mlp"], t["mlp_ff"]).reshape(b, s, d)
    return h
```

```python
import functools
import math

import jax
import jax.numpy as jnp
import numpy as np
from jax import lax
from jax.experimental import pallas as pl
from jax.experimental.pallas import tpu as pltpu

EPS = 1e-6
LOG2E = math.log2(math.e)
LANES = 128
FOX_HEADS = 8
FOX_HEAD_DIM = 64
FOX_WIDTH = FOX_HEADS * FOX_HEAD_DIM
HGRN_HEADS = 4
HGRN_DIM = 128
HGRN_WIDTH = HGRN_HEADS * HGRN_DIM
X_HEADS = 4
HGRN_STEP = 16
HGRN_CHUNK = 64
HGRN_MIN_CHUNK_LOG_DECAY = -60.0
VMEM_LIMIT = 56 * 1024 * 1024

PAIR_LANES = 2 * LANES
AUG_LANES = LANES // FOX_HEADS
AUG_F = 0
AUG_ONE = 3
AUG_SHIFT = 6
FCAT_ONE_LANE = 24
FCAT_SHIFT_LANE = 32
NORM_HEADROOM = 1.0 + 2.0 ** -7
FOX_SHIFT_MARGIN = 60.0
FOX_SPREAD_MAX = 120.0
STAT_BLOCK = 256
FOX_SKIP_LOG2 = 70.0

BF16 = jnp.bfloat16
F32 = jnp.float32


def _dot(a, b):
    return jnp.dot(a, b, preferred_element_type=F32)


def _dot_nt(a, b):
    return lax.dot_general(a, b, (((1,), (1,)), ((), ())), preferred_element_type=F32)


def _dot_tn(a, b):
    return lax.dot_general(a, b, (((0,), (0,)), ((), ())), preferred_element_type=F32)


def _split3(v):
    hi = v.astype(BF16)
    r1 = v - hi.astype(F32)
    mid = r1.astype(BF16)
    lo = (r1 - mid.astype(F32)).astype(BF16)
    return hi, mid, lo


def _tri_cumsum(tri, v):
    hi, mid, lo = _split3(v)
    return _dot(tri, hi) + _dot(tri, mid) + _dot(tri, lo)


def _rms(x, g):
    ms = jnp.mean(x * x, axis=-1, keepdims=True)
    return x * lax.rsqrt(ms + EPS) * g


def _resident(shape):
    return pl.BlockSpec(shape, lambda *_: (0,) * len(shape), pipeline_mode=pl.Buffered(1))


def _front_kernel(x_ref, g_ref, wt_ref, fb_ref, lbl_ref, sel_ref, tri_ref, ng_ref, tri64_ref, tri16_ref,
                  q_ref, k_ref, vt_ref, stat_ref, rec_ref,
                  carry_ref, kmax_ref, hb_sc, hq_sc, hk_sc, hg_sc, hi_sc, gate_sc, st_sc, st_prev_sc, b_sc, o_sc,
                  *, layer):
    @pl.when(pl.program_id(1) == 0)
    def _():
        carry_ref[...] = jnp.zeros_like(carry_ref)
        kmax_ref[...] = jnp.zeros_like(kmax_ref)
        st_sc[...] = jnp.zeros_like(st_sc)

    tm = x_ref.shape[1]
    hb_sc[...] = _rms(x_ref[0], g_ref[...]).astype(BF16)
    lane = lax.broadcasted_iota(jnp.int32, (tm, LANES), 1)
    fw, w = FOX_WIDTH, HGRN_WIDTH
    proj = lambda r0, n: _dot_nt(hb_sc[...], wt_ref[r0:r0 + n, :].astype(BF16))

    lbl = lbl_ref[...]
    e = jnp.exp(lbl - jnp.max(lbl, axis=0, keepdims=True))
    lb = jnp.sum(e[0:layer + 1, :], axis=0, keepdims=True) / jnp.sum(e, axis=0, keepdims=True)
    seg = lambda n: proj(3 * fw + FOX_HEADS + n * w, w)
    gf = seg(1)
    gq = seg(0)
    f = lb + (1.0 - lb) * jax.nn.sigmoid(gf)
    hk_sc[0] = 1.0 - f
    hg_sc[...] = jnp.log(f)
    gi = seg(2)
    hq_sc[0] = gq * jax.nn.sigmoid(gq)
    gg = seg(3)
    cs_rows = tri64_ref.shape[0]
    for r0 in range(0, tm, cs_rows):
        b_sc[r0:r0 + cs_rows, :] = _tri_cumsum(tri64_ref[...], hg_sc[r0:r0 + cs_rows, :])
    hi_sc[0] = gi.astype(BF16)
    gate_sc[...] = (gg * jax.nn.sigmoid(gg)).astype(BF16)
    chunk_ok = jnp.min(b_sc[...]) >= HGRN_MIN_CHUNK_LOG_DECAY
    st_prev_sc[...] = st_sc[...]

    def hgrn_output():
        for h in range(HGRN_HEADS):
            ls = slice(h * HGRN_DIM, (h + 1) * HGRN_DIM)
            rec_ref[0, :, ls] = (_rms(o_sc[:, ls], ng_ref[...]) * gate_sc[:, ls]).astype(rec_ref.dtype)

    v = {}

    def fox_forget():
        z = proj(3 * fw, LANES) + fb_ref[...]
        logf = jnp.minimum(z, 0.0) - jnp.log(1.0 + jnp.exp(-jnp.abs(z)))
        logf = jnp.where(lane < FOX_HEADS, logf, 0.0)
        grp = tri_ref.shape[0]
        total = carry_ref[0:1, :]
        groups = []
        for r0 in range(0, tm, grp):
            groups.append(_tri_cumsum(tri_ref[...], logf[r0:r0 + grp, :]) + total)
            total = groups[-1][grp - 1:grp, :]
        carry_ref[...] = jnp.broadcast_to(total, carry_ref.shape)
        v["fsc"] = jnp.concatenate(groups, axis=0) * LOG2E

    def fox_qk():
        q_scale = LOG2E / math.sqrt(FOX_HEAD_DIM)
        v["qk"] = jnp.concatenate([(proj(0, fw) * q_scale).astype(BF16), proj(fw, fw).astype(BF16)],
                                  axis=1)

    def fox_bounds():
        qkf = v["qk"].astype(F32)
        sq = qkf * qkf
        norm2 = jnp.zeros((tm, LANES), F32)
        for c in range(2 * fw // LANES):
            blk = sq[:, c * LANES:(c + 1) * LANES]
            first = jnp.sum(jnp.where(lane < FOX_HEAD_DIM, blk, 0.0), axis=-1, keepdims=True)
            second = jnp.sum(jnp.where(lane >= FOX_HEAD_DIM, blk, 0.0), axis=-1, keepdims=True)
            norm2 = jnp.where(lane == 2 * c, first, jnp.where(lane == 2 * c + 1, second, norm2))
        norm2 = norm2 * NORM_HEADROOM
        norm = jnp.sqrt(norm2)
        to_q_lanes = lambda a: pltpu.roll(a, LANES - FOX_HEADS, 1)
        k_run = jnp.maximum(kmax_ref[...], jnp.max(norm, axis=0, keepdims=True))
        kmax_ref[...] = k_run
        k_run_q = to_q_lanes(k_run)[0:1, :]
        v["shift"] = jnp.where(lane < FOX_HEADS, norm * k_run_q - FOX_SHIFT_MARGIN, 0.0)
        v["spread"] = norm * (k_run_q + to_q_lanes(norm))
        v["norm2"] = norm2

    def fox_operands():
        parts = lambda a: [p.astype(F32) for p in _split3(a)]
        f_parts, s_parts = parts(v["fsc"]), parts(-v["shift"])
        fcat = jnp.where(lane == FCAT_ONE_LANE, 1.0, 0.0)
        for j in range(3):
            fcat = fcat + (pltpu.roll(f_parts[j], j * FOX_HEADS, 1) if j else f_parts[j])
            fcat = fcat + pltpu.roll(s_parts[j], FCAT_SHIFT_LANE + j * FOX_HEADS, 1)
        aug = _dot(fcat.astype(BF16), sel_ref[...]).astype(BF16)
        qk, pairs = v["qk"], FOX_HEADS // 2
        for p in range(pairs):
            q_ref[0, p, :, 0:LANES] = qk[:, p * LANES:(p + 1) * LANES]
            q_ref[0, p, :, LANES:PAIR_LANES] = aug[:, 0:LANES]
            k_ref[0, p, :, 0:LANES] = qk[:, (pairs + p) * LANES:(pairs + p + 1) * LANES]
            k_ref[0, p, :, LANES:PAIR_LANES] = aug[:, LANES:PAIR_LANES]

    def fox_values():
        vt_ref[0] = _dot_nt(wt_ref[2 * fw:3 * fw, :].astype(BF16), hb_sc[...]).astype(BF16)

    def fox_stats():
        for sb in range(tm // STAT_BLOCK):
            rows = slice(sb * STAT_BLOCK, (sb + 1) * STAT_BLOCK)
            stat_ref[0, sb, 0:1, :] = jnp.max(v["norm2"][rows, :], axis=0, keepdims=True)
            stat_ref[0, sb, 1:2, :] = v["fsc"][rows, :][0:1, :]
            stat_ref[0, sb, 2:3, :] = v["fsc"][rows, :][STAT_BLOCK - 1:STAT_BLOCK, :]
            stat_ref[0, sb, 3:4, :] = jnp.max(v["spread"][rows, :], axis=0, keepdims=True)
            stat_ref[0, sb, 4:8, :] = jnp.zeros((4, LANES), F32)

    fox_stages = [fox_forget, fox_qk, fox_bounds, fox_operands, fox_values, fox_stats]

    hgrn_stages = _hgrn_chunk_stages(hq_sc, hk_sc, hi_sc, st_sc, b_sc, o_sc, tm) + [hgrn_output]
    for a, b in zip(hgrn_stages, fox_stages):
        a()
        b()

    @pl.when(jnp.logical_not(chunk_ok))
    def _():
        st_sc[...] = st_prev_sc[...]
        for r0 in range(0, tm, cs_rows):
            b_sc[r0:r0 + cs_rows, :] = _tri_cumsum(tri16_ref[...], hg_sc[r0:r0 + cs_rows, :])
        _hgrn_step_path(hq_sc, hk_sc, hi_sc, st_sc, b_sc, o_sc, tm)
        hgrn_output()


def _front(x, norm_g, w_in, fox_f_bias, lb_logits, hgrn_norm_g, layer, tm):
    b, s, d = x.shape
    fw, hw = FOX_WIDTH, HGRN_WIDTH
    assert w_in.shape == (d, 3 * fw + FOX_HEADS + 4 * hw)
    wt = w_in.T
    fb = jnp.pad(fox_f_bias.reshape(1, FOX_HEADS), ((0, 0), (0, LANES - FOX_HEADS)))

    sel = np.zeros((LANES, 2 * LANES), np.float32)
    for h in range(FOX_HEADS):
        col = h * AUG_LANES
        for j in range(3):
            sel[j * FOX_HEADS + h, col + AUG_F + j] = 1.0
            sel[FCAT_ONE_LANE, col + AUG_ONE + j] = 1.0
            sel[FCAT_ONE_LANE, LANES + col + AUG_F + j] = 1.0
            sel[j * FOX_HEADS + h, LANES + col + AUG_ONE + j] = -1.0
            sel[FCAT_SHIFT_LANE + j * FOX_HEADS + h, col + AUG_SHIFT + j] = 1.0
            sel[FCAT_ONE_LANE, LANES + col + AUG_SHIFT + j] = 1.0
    sel = jnp.asarray(sel, BF16)
    fg = min(tm, LANES)
    tri = jnp.asarray(np.tril(np.ones((fg, fg), np.float32)), BF16)

    cs_rows = min(tm, 256)
    tri64, tri16 = _block_tri(cs_rows, HGRN_CHUNK), _block_tri(cs_rows, HGRN_STEP)

    const = _resident
    pairs = FOX_HEADS // 2
    head_out = jax.ShapeDtypeStruct((b, pairs, s, PAIR_LANES), BF16)
    head_spec = pl.BlockSpec((1, pairs, tm, PAIR_LANES), lambda bi, ti: (bi, 0, ti, 0))
    vm = pltpu.VMEM
    return pl.pallas_call(
        functools.partial(_front_kernel, layer=layer),
        grid=(b, s // tm),
        in_specs=[pl.BlockSpec((1, tm, d), lambda bi, ti: (bi, ti, 0)),
                  const((1, d)), const(wt.shape), const(fb.shape), const(lb_logits.shape), const(sel.shape),
                  const(tri.shape), const((1, HGRN_DIM)), const(tri64.shape), const(tri16.shape)],
        out_specs=[head_spec, head_spec, pl.BlockSpec((1, fw, tm), lambda bi, ti: (bi, 0, ti)),
                   pl.BlockSpec((1, tm // STAT_BLOCK, 8, LANES), lambda bi, ti: (bi, ti, 0, 0)),
                   pl.BlockSpec((1, tm, hw), lambda bi, ti: (bi, ti, 0))],
        out_shape=[head_out, head_out, jax.ShapeDtypeStruct((b, fw, s), BF16),
                   jax.ShapeDtypeStruct((b, s // STAT_BLOCK, 8, LANES), F32),
                   jax.ShapeDtypeStruct((b, s, hw), BF16)],
        scratch_shapes=[vm((8, LANES), F32), vm((8, LANES), F32), vm((tm, d), BF16),
                        vm((1, tm, hw), F32), vm((1, tm, hw), F32), vm((tm, hw), F32),
                        vm((1, tm, hw), BF16), vm((tm, hw), BF16),
                        vm((HGRN_HEADS, HGRN_DIM, HGRN_DIM), F32), vm((HGRN_HEADS, HGRN_DIM, HGRN_DIM), F32),
                        vm((tm, hw), F32), vm((tm, hw), F32)],
        compiler_params=pltpu.CompilerParams(dimension_semantics=("arbitrary", "arbitrary"),
                                             vmem_limit_bytes=VMEM_LIMIT),
        name="front",
    )(x, norm_g.reshape(1, d), wt, fb, lb_logits, sel, tri, hgrn_norm_g.reshape(1, HGRN_DIM), tri64, tri16)


def _fox_first_block(stats, tq):
    b, nk = stats.shape[0], stats.shape[1]
    assert tq % STAT_BLOCK == 0, (tq, STAT_BLOCK)
    r = tq // STAT_BLOCK
    nq = nk // r
    h = FOX_HEADS
    kn = jnp.sqrt(stats[:, :, 0, h:2 * h])
    f_last = stats[:, :, 2, 0:h]
    by_query = stats.reshape(b, nq, r, 8, LANES)
    qn = jnp.sqrt(jnp.max(by_query[:, :, :, 0, 0:h], axis=2))
    kn_own = jnp.sqrt(jnp.max(by_query[:, :, :, 0, h:2 * h], axis=2))
    f_first = by_query[:, :, 0, 1, 0:h]
    bound = (qn[:, :, None, :] * (kn[:, None, :, :] + kn_own[:, :, None, :])
             + f_first[:, :, None, :] - f_last[:, None, :, :])
    key_blk = jnp.arange(nk)
    diag_start = jnp.arange(nq) * r
    need = (jnp.any(jnp.logical_not(bound <= -FOX_SKIP_LOG2), axis=-1)
            & (key_blk[None, None, :] < diag_start[None, :, None]))
    first = jnp.min(jnp.where(need, key_blk[None, None, :], nk), axis=-1)
    return jnp.minimum(first, diag_start[None, :]).astype(jnp.int32)


def _fox_direct_ok(stats, tq):
    b, nsb = stats.shape[0], stats.shape[1]
    spread = jnp.max(stats[:, :, 3, 0:FOX_HEADS].reshape(b, nsb * STAT_BLOCK // tq, -1), axis=-1)
    return (spread <= FOX_SPREAD_MAX).astype(jnp.int32)


def _fox_kernel(first_ref, direct_ref, q_ref, k_ref, vt_ref, o_ref, qm_sc, m_sc, l_sc, acc_sc, *, tq, hg):
    i = pl.program_id(2)
    hd = FOX_HEAD_DIM
    l_sc[...] = jnp.zeros_like(l_sc)
    acc_sc[...] = jnp.zeros_like(acc_sc)
    lane = lax.broadcasted_iota(jnp.int32, (tq, PAIR_LANES), 1)
    owner = jnp.where(lane < LANES, lane // hd, 2 + (lane - LANES) // AUG_LANES)
    for hh in range(hg):
        head = pl.program_id(1) * hg + hh
        own = (owner == hh % 2) | (owner == 2 + head)
        qm_sc[hh] = jnp.where(own, q_ref[0, hh // 2], jnp.zeros((), BF16))

    def accumulate_online(hh, st, off):
        m_prev = m_sc[hh]
        m_new = jnp.maximum(m_prev, jnp.max(st, axis=0, keepdims=True))
        alpha = jnp.exp2(m_prev - m_new)
        p = jnp.exp2(st - m_new)
        l_sc[hh] = alpha * l_sc[hh] + jnp.sum(p, axis=0, keepdims=True)
        vt = vt_ref[0, hh * hd:(hh + 1) * hd, pl.ds(off, st.shape[0])]
        acc_sc[hh] = alpha * acc_sc[hh] + _dot(vt, p.astype(BF16))
        m_sc[hh] = m_new

    def accumulate_direct(hh, st, off):
        p = jnp.exp2(st)
        l_sc[hh] += jnp.sum(p, axis=0, keepdims=True)
        acc_sc[hh] += _dot(vt_ref[0, hh * hd:(hh + 1) * hd, pl.ds(off, st.shape[0])], p.astype(BF16))

    def diagonal_direct():
        half = tq // 2
        off0 = pl.multiple_of(i * tq, tq)
        off1 = pl.multiple_of(i * tq + half, half)
        tri = (lax.broadcasted_iota(jnp.int32, (half, half), 0) <= lax.broadcasted_iota(jnp.int32, (half, half), 1))

        def scores(h_):
            top = _dot_nt(k_ref[0, h_ // 2, pl.ds(off0, half), :], qm_sc[h_])
            low = _dot_nt(k_ref[0, h_ // 2, pl.ds(off1, half), :], qm_sc[h_, half:tq, :])
            return top, low

        ahead = 2
        pending = [scores(h_) for h_ in range(min(ahead, hg))]
        for hh in range(hg):
            top, low = pending.pop(0)
            if hh + ahead < hg:
                pending.append(scores(hh + ahead))
            p_top = jnp.exp2(jnp.concatenate([jnp.where(tri, top[:, 0:half], -jnp.inf), top[:, half:tq]], axis=1))
            p_low = jnp.exp2(jnp.where(tri, low, -jnp.inf))
            rows = slice(hh * hd, (hh + 1) * hd)
            l_sc[hh] += jnp.sum(p_top, axis=0, keepdims=True)
            l_sc[hh, :, half:tq] += jnp.sum(p_low, axis=0, keepdims=True)
            acc_sc[hh] += _dot(vt_ref[0, rows, pl.ds(off0, half)], p_top.astype(BF16))
            acc_sc[hh, :, half:tq] += _dot(vt_ref[0, rows, pl.ds(off1, half)], p_low.astype(BF16))

    def sweep(accumulate, diagonal=None):
        def block(j, keys, masked):
            off = pl.multiple_of(j * keys, keys)
            scores = lambda h_: _dot_nt(k_ref[0, h_ // 2, pl.ds(off, keys), :], qm_sc[h_])
            ahead = 4
            pending = [scores(h_) for h_ in range(min(ahead, hg))]
            for hh in range(hg):
                st = pending.pop(0)
                if hh + ahead < hg:
                    pending.append(scores(hh + ahead))
                if masked:
                    key = lax.broadcasted_iota(jnp.int32, st.shape, 0)
                    qry = lax.broadcasted_iota(jnp.int32, st.shape, 1)
                    st = jnp.where(key <= qry, st, -jnp.inf)
                accumulate(hh, st, off)

        def body(j, carry):
            block(j, STAT_BLOCK, False)
            return carry

        lax.fori_loop(first_ref[pl.program_id(0), i], i * (tq // STAT_BLOCK), body, 0)
        if diagonal is None:
            block(i, tq, True)
        else:
            diagonal()

    direct = direct_ref[pl.program_id(0), i] != 0

    @pl.when(direct)
    def _():
        sweep(accumulate_direct, diagonal_direct)

    @pl.when(jnp.logical_not(direct))
    def _():
        m_sc[...] = jnp.full_like(m_sc, -jnp.inf)
        sweep(accumulate_online)

    for hh in range(hg):
        o_ref[0, hh * hd:(hh + 1) * hd, :] = (acc_sc[hh] / l_sc[hh]).astype(o_ref.dtype)


def _fox_attention(q, k, vt, first_block, direct_ok, tq, hg):
    b, pairs, s, _ = q.shape
    assert hg % 2 == 0
    groups = 2 * pairs // hg
    rows = hg * FOX_HEAD_DIM
    grid_spec = pltpu.PrefetchScalarGridSpec(
        num_scalar_prefetch=2,
        grid=(b, groups, s // tq),
        in_specs=[pl.BlockSpec((1, hg // 2, tq, PAIR_LANES), lambda bi, pi, qi, *_: (bi, pi, qi, 0)),
                  pl.BlockSpec((1, hg // 2, s, PAIR_LANES), lambda bi, pi, qi, *_: (bi, pi, 0, 0)),
                  pl.BlockSpec((1, rows, s), lambda bi, pi, qi, *_: (bi, pi, 0))],
        out_specs=pl.BlockSpec((1, rows, tq), lambda bi, pi, qi, *_: (bi, pi, qi)),
        scratch_shapes=[pltpu.VMEM((hg, tq, PAIR_LANES), BF16),
                        pltpu.VMEM((hg, 1, tq), F32), pltpu.VMEM((hg, 1, tq), F32),
                        pltpu.VMEM((hg, FOX_HEAD_DIM, tq), F32)])
    return pl.pallas_call(
        functools.partial(_fox_kernel, tq=tq, hg=hg),
        grid_spec=grid_spec,
        out_shape=jax.ShapeDtypeStruct((b, groups * rows, s), BF16),
        compiler_params=pltpu.CompilerParams(dimension_semantics=("arbitrary",) * 3,
                                             vmem_limit_bytes=VMEM_LIMIT),
        name="fox_attention",
    )(first_block, direct_ok, q, k, vt)


def _hgrn_chunk_stages(q_ref, k_ref, i_ref, st_sc, b_sc, o_sc, tc):
    c_len = HGRN_CHUNK
    n_chunks = tc // c_len
    t_idx = lax.broadcasted_iota(jnp.int32, (c_len, c_len), 0)
    s_idx = lax.broadcasted_iota(jnp.int32, (c_len, c_len), 1)
    causal = s_idx <= t_idx
    units = [(c, h) for c in range(n_chunks) for h in range(HGRN_HEADS)]
    rows = lambda c: slice(c * c_len, (c + 1) * c_len)
    lanes = lambda h: slice(h * HGRN_DIM, (h + 1) * HGRN_DIM)
    qe, ke, kl, decay, attn, intra, d_state, state = ({} for _ in range(8))

    def decay_operands():
        for c, h in units:
            b = b_sc[rows(c), lanes(h)]
            kk = k_ref[0, rows(c), lanes(h)]
            b_last = b[c_len - 1:c_len, :]
            qe[c, h] = (q_ref[0, rows(c), lanes(h)] * jnp.exp(b)).astype(BF16)
            ke[c, h] = (kk * jnp.exp(-b)).astype(BF16)
            kl[c, h] = (kk * jnp.exp(b_last - b)).astype(BF16)
            decay[c, h] = jnp.exp(b_last)

    def scores():
        for u in units:
            attn[u] = jnp.where(causal, _dot_nt(qe[u], ke[u]), 0.0).astype(BF16)

    def products():
        for c, h in units:
            intra[c, h] = _dot(attn[c, h], i_ref[0, rows(c), lanes(h)])
            d_state[c, h] = _dot_tn(i_ref[0, rows(c), lanes(h)], kl[c, h])

    def recurrence():
        for h in range(HGRN_HEADS):
            st = st_sc[h]
            for c in range(n_chunks):
                state[c, h] = st.astype(BF16)
                st = st * decay[c, h] + d_state[c, h]
            st_sc[h] = st

    def outputs():
        for c, h in units:
            o_sc[rows(c), lanes(h)] = intra[c, h] + _dot_nt(qe[c, h], state[c, h])

    return [decay_operands, scores, products, recurrence, outputs]


def _hgrn_step_path(q_ref, k_ref, i_ref, st_sc, b_sc, o_sc, tc):
    row8 = lax.broadcasted_iota(jnp.int32, (8, HGRN_DIM), 0)
    half = HGRN_STEP // 2

    def step(u, carry):
        base = pl.multiple_of(u * HGRN_STEP, HGRN_STEP)
        for h in range(HGRN_HEADS):
            ls = slice(h * HGRN_DIM, (h + 1) * HGRN_DIM)
            rows = pl.ds(base, HGRN_STEP)
            bq = b_sc[rows, ls]
            qq = q_ref[0, rows, ls]
            kk = k_ref[0, rows, ls]
            ii = i_ref[0, rows, ls].astype(F32)
            b_last = bq[HGRN_STEP - 1:HGRN_STEP, :]
            st = st_sc[h]
            inter = _dot_nt((qq * jnp.exp(bq)).astype(BF16), st.astype(BF16))
            k_dec = (kk * jnp.exp(b_last - bq)).astype(BF16)
            st_sc[h] = st * jnp.exp(b_last) + _dot_tn(ii.astype(BF16), k_dec)
            out = [inter[0:half], inter[half:HGRN_STEP]]
            qv = [qq[0:half], qq[half:HGRN_STEP]]
            bv = [bq[0:half], bq[half:HGRN_STEP]]
            for s_ in range(HGRN_STEP):
                k_s, b_s, i_s = kk[s_:s_ + 1, :], bq[s_:s_ + 1, :], ii[s_:s_ + 1, :]
                for v_ in range(s_ // half, 2):
                    diff = bv[v_] - b_s
                    if s_ > v_ * half:
                        diff = jnp.where(row8 + v_ * half >= s_, diff, -jnp.inf)
                    a = jnp.sum(qv[v_] * k_s * jnp.exp(diff), axis=-1, keepdims=True)
                    out[v_] = out[v_] + a * i_s
            o_sc[pl.ds(base, half), ls] = out[0]
            o_sc[pl.ds(base + half, half), ls] = out[1]
        return carry

    lax.fori_loop(0, tc // HGRN_STEP, step, 0)


def _block_tri(n, blk):
    return jnp.asarray(np.kron(np.eye(n // blk, dtype=np.float32), np.tril(np.ones((blk, blk), np.float32))), BF16)


def _memkv_kernel(mem_ref, g_ref, wkv_ref, wq_ref, wo_ref, sm_ref, om_ref):
    d = mem_ref.shape[-1]
    m = mem_ref.shape[1]
    hd = d // X_HEADS
    inv = 1.0 / math.sqrt(hd)
    kv = _dot(_rms(mem_ref[0], g_ref[...]).astype(BF16), wkv_ref[...].astype(BF16))
    for h in range(X_HEADS):
        cs = slice(h * hd, (h + 1) * hd)
        k_h = kv[:, cs].astype(BF16)
        v_h = kv[:, d + h * hd:d + (h + 1) * hd].astype(BF16)
        sm_ref[0, :, h * m:(h + 1) * m] = (_dot_nt(wq_ref[:, cs].astype(BF16), k_h) * inv).astype(BF16)
        om_ref[0, h * m:(h + 1) * m, :] = _dot(v_h, wo_ref[cs, :].astype(BF16)).astype(BF16)


def _mem_kv(mem, norm_g, w_kv, w_xq, w_xo):
    b, m, d = mem.shape
    return pl.pallas_call(
        _memkv_kernel,
        grid=(b,),
        in_specs=[pl.BlockSpec((1, m, d), lambda bi: (bi, 0, 0)), _resident((1, d)), _resident((d, 2 * d)),
                  _resident((d, d)), _resident((d, d))],
        out_specs=[pl.BlockSpec((1, d, X_HEADS * m), lambda bi: (bi, 0, 0)),
                   pl.BlockSpec((1, X_HEADS * m, d), lambda bi: (bi, 0, 0))],
        out_shape=[jax.ShapeDtypeStruct((b, d, X_HEADS * m), BF16), jax.ShapeDtypeStruct((b, X_HEADS * m, d), BF16)],
        compiler_params=pltpu.CompilerParams(dimension_semantics=("arbitrary",), vmem_limit_bytes=VMEM_LIMIT),
        name="mem_kv",
    )(mem, norm_g.reshape(1, d), w_kv, w_xq, w_xo)


def _mix_kernel(x_ref, foxt_ref, rec_ref, wo_ref, gx_ref, sm_ref, om_ref, o_ref, h1_sc, p_sc, *, sub):
    fw = foxt_ref.shape[1]
    n_mem = sm_ref.shape[2] // X_HEADS
    n_sub = x_ref.shape[1] // sub

    def out_proj(i):
        r = slice(i * sub, (i + 1) * sub)
        h1_sc[r, :] = (x_ref[0, r, :] + _dot_tn(foxt_ref[0, :, r], wo_ref[0:fw, :].astype(BF16))
                       + _dot(rec_ref[0, r, :], wo_ref[fw:, :].astype(BF16)))

    def attend(i):
        r = slice(i * sub, (i + 1) * sub)
        s = _dot(_rms(h1_sc[r, :], gx_ref[...]).astype(BF16), sm_ref[0])
        for h in range(X_HEADS):
            cs = slice(h * n_mem, (h + 1) * n_mem)
            p = jnp.exp(s[:, cs] - jnp.max(s[:, cs], axis=-1, keepdims=True))
            p_sc[r, cs] = (p / jnp.sum(p, axis=-1, keepdims=True)).astype(BF16)

    def finish(i):
        r = slice(i * sub, (i + 1) * sub)
        o_ref[0, r, :] = h1_sc[r, :] + _dot(p_sc[r, :], om_ref[0])

    stages = (out_proj, attend, finish)
    for t in range(n_sub + len(stages) - 1):
        for k, stage in enumerate(stages):
            if 0 <= t - k < n_sub:
                stage(t - k)


def _mix(x, foxt, rec, w_out, norm_g, score_m, out_m, tm):
    b, s, d = x.shape
    fw = foxt.shape[1]
    hm = score_m.shape[2]
    tok = lambda wd: pl.BlockSpec((1, tm, wd), lambda bi, ti: (bi, ti, 0))
    return pl.pallas_call(
        functools.partial(_mix_kernel, sub=min(tm, 256)),
        grid=(b, s // tm),
        in_specs=[tok(d), pl.BlockSpec((1, fw, tm), lambda bi, ti: (bi, 0, ti)), tok(rec.shape[-1]),
                  _resident((d, d)), _resident((1, d)),
                  pl.BlockSpec((1, d, hm), lambda bi, ti: (bi, 0, 0)), pl.BlockSpec((1, hm, d), lambda bi, ti: (bi, 0, 0))],
        out_specs=tok(d),
        out_shape=jax.ShapeDtypeStruct((b, s, d), F32),
        scratch_shapes=[pltpu.VMEM((tm, d), F32), pltpu.VMEM((tm, hm), BF16)],
        compiler_params=pltpu.CompilerParams(dimension_semantics=("arbitrary", "arbitrary"),
                                             vmem_limit_bytes=VMEM_LIMIT),
        name="mix",
    )(x, foxt, rec, w_out, norm_g.reshape(1, d), score_m, out_m)


def _mlp_kernel(h_ref, g_ref, w1_ref, w2_ref, gf_ref, o_ref, acc_sc, *, fc):
    hn = _rms(h_ref[...], g_ref[...]).astype(BF16)
    n_chunks = w1_ref.shape[1] // fc
    up = lambda c: _dot(hn, w1_ref[:, c * fc:(c + 1) * fc].astype(BF16))
    u_next = up(0)
    for c in range(n_chunks):
        u = jnp.maximum(u_next, 0.0)
        if c + 1 < n_chunks:
            u_next = up(c + 1)
        part = _dot((u * u).astype(BF16), w2_ref[c * fc:(c + 1) * fc, :].astype(BF16))
        if c == 0:
            acc_sc[...] = part
        else:
            acc_sc[...] += part
    o_ref[...] = _rms(h_ref[...] + acc_sc[...], gf_ref[...])


def _mlp(h, norm_g, w1, w2, final_g, tm, fc):
    t, d = h.shape
    dff = w1.shape[1]
    return pl.pallas_call(
        functools.partial(_mlp_kernel, fc=fc),
        grid=(t // tm,),
        in_specs=[pl.BlockSpec((tm, d), lambda ti: (ti, 0)), _resident((1, d)),
                  _resident((d, dff)), _resident((dff, d)), _resident((1, d))],
        out_specs=pl.BlockSpec((tm, d), lambda ti: (ti, 0)),
        out_shape=jax.ShapeDtypeStruct((t, d), F32),
        scratch_shapes=[pltpu.VMEM((tm, d), F32)],
        compiler_params=pltpu.CompilerParams(dimension_semantics=("arbitrary",), vmem_limit_bytes=VMEM_LIMIT),
        name="mlp",
    )(h, norm_g.reshape(1, d), w1, w2, final_g.reshape(1, d))


def _tile(n, want):
    t = min(n, want)
    assert n % t == 0, (n, want)
    return t


def _tiles(b, s, d_ff):
    return dict(
        front=_tile(s, 512),
        fox_q=_tile(s, 512),
        fox_heads=FOX_HEADS,
        mix=_tile(s, 1024),
        mlp=_tile(b * s, 512),
        mlp_ff=_tile(d_ff, 1024),
    )


def kernel(x, mem, norm_mix_g, w_in, fox_f_bias, hgrn_lb_logits, hgrn_norm_g, w_out, norm_x_g, norm_mem_g,
           w_xq, w_xkv, w_xo, norm_ff_g, w1, w2, final_norm_g):
    b, s, d = x.shape
    t = _tiles(b, s, w1.shape[-1])
    h = x
    for l in range(w_in.shape[0]):
        q, k, v, stats, rec = _front(h, norm_mix_g[l], w_in[l], fox_f_bias[l], hgrn_lb_logits, hgrn_norm_g[l], l,
                                     t["front"])
        fox = _fox_attention(q, k, v, _fox_first_block(stats, t["fox_q"]), _fox_direct_ok(stats, t["fox_q"]),
                             t["fox_q"], t["fox_heads"])
        score_m, out_m = _mem_kv(mem, norm_mem_g[l], w_xkv[l], w_xq[l], w_xo[l])
        h = _mix(h, fox, rec, w_out[l], norm_x_g[l], score_m, out_m, t["mix"])
        is_last = l == w_in.shape[0] - 1
        assert is_last, "the MLP kernel fuses the final norm, so it must be the last layer"
        h = _mlp(h.reshape(b * s, d), norm_ff_g[l], w1[l], w2[l], final_norm_g, t["mlp"], t["mlp_ff"]).reshape(b, s, d)
    return h
```

```python
import functools
import math

import jax
import jax.numpy as jnp
import numpy as np
from jax import lax
from jax.experimental import pallas as pl
from jax.experimental.pallas import tpu as pltpu

EPS = 1e-6
LOG2E = math.log2(math.e)
LANES = 128
FOX_HEADS = 8
FOX_HEAD_DIM = 64
FOX_WIDTH = FOX_HEADS * FOX_HEAD_DIM
HGRN_HEADS = 4
HGRN_DIM = 128
HGRN_WIDTH = HGRN_HEADS * HGRN_DIM
X_HEADS = 4
HGRN_STEP = 16
HGRN_CHUNK = 64
HGRN_MIN_CHUNK_LOG_DECAY = -60.0
VMEM_LIMIT = 56 * 1024 * 1024

PAIR_LANES = 2 * LANES
AUG_LANES = LANES // FOX_HEADS
AUG_F = 0
AUG_ONE = 3
AUG_SHIFT = 6
FCAT_ONE_LANE = 24
FCAT_SHIFT_LANE = 32
NORM_HEADROOM = 1.0 + 2.0 ** -7
FOX_SHIFT_MARGIN = 60.0
FOX_SPREAD_MAX = 120.0
STAT_BLOCK = 256
FOX_SKIP_LOG2 = 70.0

BF16 = jnp.bfloat16
F32 = jnp.float32


def _dot(a, b):
    return jnp.dot(a, b, preferred_element_type=F32)


def _dot_nt(a, b):
    return lax.dot_general(a, b, (((1,), (1,)), ((), ())), preferred_element_type=F32)


def _dot_tn(a, b):
    return lax.dot_general(a, b, (((0,), (0,)), ((), ())), preferred_element_type=F32)


def _split3(v):
    hi = v.astype(BF16)
    r1 = v - hi.astype(F32)
    mid = r1.astype(BF16)
    lo = (r1 - mid.astype(F32)).astype(BF16)
    return hi, mid, lo


def _tri_cumsum(tri, v):
    hi, mid, lo = _split3(v)
    return _dot(tri, hi) + _dot(tri, mid) + _dot(tri, lo)


def _rms(x, g):
    ms = jnp.mean(x * x, axis=-1, keepdims=True)
    return x * lax.rsqrt(ms + EPS) * g


def _resident(shape):
    return pl.BlockSpec(shape, lambda *_: (0,) * len(shape), pipeline_mode=pl.Buffered(1))


def _front_kernel(x_ref, g_ref, wt_ref, fb_ref, lbl_ref, sel_ref, ng_ref, tri64_ref, tri16_ref,
                  q_ref, k_ref, vt_ref, stat_ref, rec_ref,
                  carry_ref, kmax_ref, hb_sc, hq_sc, hk_sc, hg_sc, hi_sc, gate_sc, st_sc, st_prev_sc, b_sc, o_sc,
                  *, layer):
    @pl.when(pl.program_id(1) == 0)
    def _():
        carry_ref[...] = jnp.zeros_like(carry_ref)
        kmax_ref[...] = jnp.zeros_like(kmax_ref)
        st_sc[...] = jnp.zeros_like(st_sc)

    tm = x_ref.shape[1]
    hb_sc[...] = _rms(x_ref[0], g_ref[...]).astype(BF16)
    lane = lax.broadcasted_iota(jnp.int32, (tm, LANES), 1)
    fw, w = FOX_WIDTH, HGRN_WIDTH
    proj = lambda r0, n: _dot_nt(hb_sc[...], wt_ref[r0:r0 + n, :].astype(BF16))

    lbl = lbl_ref[...]
    e = jnp.exp(lbl - jnp.max(lbl, axis=0, keepdims=True))
    lb = jnp.sum(e[0:layer + 1, :], axis=0, keepdims=True) / jnp.sum(e, axis=0, keepdims=True)
    seg = lambda n: proj(3 * fw + FOX_HEADS + n * w, w)
    gf = seg(1)
    gq = seg(0)
    f = lb + (1.0 - lb) * jax.nn.sigmoid(gf)
    hk_sc[0] = 1.0 - f
    hg_sc[...] = jnp.log(f)
    gi = seg(2)
    hq_sc[0] = gq * jax.nn.sigmoid(gq)
    gg = seg(3)
    cs_rows = tri64_ref.shape[0]
    for r0 in range(0, tm, cs_rows):
        b_sc[r0:r0 + cs_rows, :] = _tri_cumsum(tri64_ref[...], hg_sc[r0:r0 + cs_rows, :])
    hi_sc[0] = gi.astype(BF16)
    gate_sc[...] = (gg * jax.nn.sigmoid(gg)).astype(BF16)
    chunk_ok = jnp.min(b_sc[...]) >= HGRN_MIN_CHUNK_LOG_DECAY
    st_prev_sc[...] = st_sc[...]

    def hgrn_output():
        for h in range(HGRN_HEADS):
            ls = slice(h * HGRN_DIM, (h + 1) * HGRN_DIM)
            rec_ref[0, :, ls] = (_rms(o_sc[:, ls], ng_ref[...]) * gate_sc[:, ls]).astype(rec_ref.dtype)

    v = {}

    def fox_forget():
        z = proj(3 * fw, LANES) + fb_ref[...]
        logf = jnp.minimum(z, 0.0) - jnp.log(1.0 + jnp.exp(-jnp.abs(z)))
        logf = jnp.where(lane < FOX_HEADS, logf, 0.0)
        row = lax.broadcasted_iota(jnp.int32, (tm, LANES), 0)
        fcum, shift = logf, 1
        while shift < tm:
            fcum = fcum + jnp.where(row >= shift, pltpu.roll(fcum, shift, 0), 0.0)
            shift *= 2
        fcum = fcum + carry_ref[0:1, :]
        carry_ref[...] = jnp.broadcast_to(fcum[tm - 1:tm, :], carry_ref.shape)
        v["fsc"] = fcum * LOG2E

    def fox_qk():
        q_scale = LOG2E / math.sqrt(FOX_HEAD_DIM)
        v["qk"] = jnp.concatenate([(proj(0, fw) * q_scale).astype(BF16), proj(fw, fw).astype(BF16)],
                                  axis=1)

    def fox_bounds():
        qkf = v["qk"].astype(F32)
        sq = qkf * qkf
        norm2 = jnp.zeros((tm, LANES), F32)
        for c in range(2 * fw // LANES):
            blk = sq[:, c * LANES:(c + 1) * LANES]
            first = jnp.sum(jnp.where(lane < FOX_HEAD_DIM, blk, 0.0), axis=-1, keepdims=True)
            second = jnp.sum(jnp.where(lane >= FOX_HEAD_DIM, blk, 0.0), axis=-1, keepdims=True)
            norm2 = jnp.where(lane == 2 * c, first, jnp.where(lane == 2 * c + 1, second, norm2))
        norm2 = norm2 * NORM_HEADROOM
        norm = jnp.sqrt(norm2)
        to_q_lanes = lambda a: pltpu.roll(a, LANES - FOX_HEADS, 1)
        k_run = jnp.maximum(kmax_ref[...], jnp.max(norm, axis=0, keepdims=True))
        kmax_ref[...] = k_run
        k_run_q = to_q_lanes(k_run)[0:1, :]
        v["shift"] = jnp.where(lane < FOX_HEADS, norm * k_run_q - FOX_SHIFT_MARGIN, 0.0)
        v["spread"] = norm * (k_run_q + to_q_lanes(norm))
        v["norm2"] = norm2

    def fox_operands():
        parts = lambda a: [p.astype(F32) for p in _split3(a)]
        f_parts, s_parts = parts(v["fsc"]), parts(-v["shift"])
        fcat = jnp.where(lane == FCAT_ONE_LANE, 1.0, 0.0)
        for j in range(3):
            fcat = fcat + (pltpu.roll(f_parts[j], j * FOX_HEADS, 1) if j else f_parts[j])
            fcat = fcat + pltpu.roll(s_parts[j], FCAT_SHIFT_LANE + j * FOX_HEADS, 1)
        aug = _dot(fcat.astype(BF16), sel_ref[...]).astype(BF16)
        qk, pairs = v["qk"], FOX_HEADS // 2
        for p in range(pairs):
            q_ref[0, p, :, 0:LANES] = qk[:, p * LANES:(p + 1) * LANES]
            q_ref[0, p, :, LANES:PAIR_LANES] = aug[:, 0:LANES]
            k_ref[0, p, :, 0:LANES] = qk[:, (pairs + p) * LANES:(pairs + p + 1) * LANES]
            k_ref[0, p, :, LANES:PAIR_LANES] = aug[:, LANES:PAIR_LANES]

    def fox_values():
        vt_ref[0] = _dot_nt(wt_ref[2 * fw:3 * fw, :].astype(BF16), hb_sc[...]).astype(BF16)

    def fox_stats():
        for sb in range(tm // STAT_BLOCK):
            rows = slice(sb * STAT_BLOCK, (sb + 1) * STAT_BLOCK)
            stat_ref[0, sb, 0:1, :] = jnp.max(v["norm2"][rows, :], axis=0, keepdims=True)
            stat_ref[0, sb, 1:2, :] = v["fsc"][rows, :][0:1, :]
            stat_ref[0, sb, 2:3, :] = v["fsc"][rows, :][STAT_BLOCK - 1:STAT_BLOCK, :]
            stat_ref[0, sb, 3:4, :] = jnp.max(v["spread"][rows, :], axis=0, keepdims=True)
            stat_ref[0, sb, 4:8, :] = jnp.zeros((4, LANES), F32)

    fox_stages = [fox_forget, fox_qk, fox_bounds, fox_operands, fox_values, fox_stats]

    hgrn_stages = _hgrn_chunk_stages(hq_sc, hk_sc, hi_sc, st_sc, b_sc, o_sc, tm) + [hgrn_output]
    for a, b in zip(hgrn_stages, fox_stages):
        a()
        b()

    @pl.when(jnp.logical_not(chunk_ok))
    def _():
        st_sc[...] = st_prev_sc[...]
        for r0 in range(0, tm, cs_rows):
            b_sc[r0:r0 + cs_rows, :] = _tri_cumsum(tri16_ref[...], hg_sc[r0:r0 + cs_rows, :])
        _hgrn_step_path(hq_sc, hk_sc, hi_sc, st_sc, b_sc, o_sc, tm)
        hgrn_output()


def _front(x, norm_g, w_in, fox_f_bias, lb_logits, hgrn_norm_g, layer, tm):
    b, s, d = x.shape
    fw, hw = FOX_WIDTH, HGRN_WIDTH
    assert w_in.shape == (d, 3 * fw + FOX_HEADS + 4 * hw)
    wt = w_in.T
    fb = jnp.pad(fox_f_bias.reshape(1, FOX_HEADS), ((0, 0), (0, LANES - FOX_HEADS)))

    sel = np.zeros((LANES, 2 * LANES), np.float32)
    for h in range(FOX_HEADS):
        col = h * AUG_LANES
        for j in range(3):
            sel[j * FOX_HEADS + h, col + AUG_F + j] = 1.0
            sel[FCAT_ONE_LANE, col + AUG_ONE + j] = 1.0
            sel[FCAT_ONE_LANE, LANES + col + AUG_F + j] = 1.0
            sel[j * FOX_HEADS + h, LANES + col + AUG_ONE + j] = -1.0
            sel[FCAT_SHIFT_LANE + j * FOX_HEADS + h, col + AUG_SHIFT + j] = 1.0
            sel[FCAT_ONE_LANE, LANES + col + AUG_SHIFT + j] = 1.0
    sel = jnp.asarray(sel, BF16)

    cs_rows = min(tm, 256)
    tri64, tri16 = _block_tri(cs_rows, HGRN_CHUNK), _block_tri(cs_rows, HGRN_STEP)

    const = _resident
    pairs = FOX_HEADS // 2
    head_out = jax.ShapeDtypeStruct((b, pairs, s, PAIR_LANES), BF16)
    head_spec = pl.BlockSpec((1, pairs, tm, PAIR_LANES), lambda bi, ti: (bi, 0, ti, 0))
    vm = pltpu.VMEM
    return pl.pallas_call(
        functools.partial(_front_kernel, layer=layer),
        grid=(b, s // tm),
        in_specs=[pl.BlockSpec((1, tm, d), lambda bi, ti: (bi, ti, 0)),
                  const((1, d)), const(wt.shape), const(fb.shape), const(lb_logits.shape), const(sel.shape),
                  const((1, HGRN_DIM)), const(tri64.shape), const(tri16.shape)],
        out_specs=[head_spec, head_spec, pl.BlockSpec((1, fw, tm), lambda bi, ti: (bi, 0, ti)),
                   pl.BlockSpec((1, tm // STAT_BLOCK, 8, LANES), lambda bi, ti: (bi, ti, 0, 0)),
                   pl.BlockSpec((1, tm, hw), lambda bi, ti: (bi, ti, 0))],
        out_shape=[head_out, head_out, jax.ShapeDtypeStruct((b, fw, s), BF16),
                   jax.ShapeDtypeStruct((b, s // STAT_BLOCK, 8, LANES), F32),
                   jax.ShapeDtypeStruct((b, s, hw), BF16)],
        scratch_shapes=[vm((8, LANES), F32), vm((8, LANES), F32), vm((tm, d), BF16),
                        vm((1, tm, hw), F32), vm((1, tm, hw), F32), vm((tm, hw), F32),
                        vm((1, tm, hw), BF16), vm((tm, hw), BF16),
                        vm((HGRN_HEADS, HGRN_DIM, HGRN_DIM), F32), vm((HGRN_HEADS, HGRN_DIM, HGRN_DIM), F32),
                        vm((tm, hw), F32), vm((tm, hw), F32)],
        compiler_params=pltpu.CompilerParams(dimension_semantics=("arbitrary", "arbitrary"),
                                             vmem_limit_bytes=VMEM_LIMIT),
        name="front",
    )(x, norm_g.reshape(1, d), wt, fb, lb_logits, sel, hgrn_norm_g.reshape(1, HGRN_DIM), tri64, tri16)


def _fox_first_block(stats, tq):
    b, nk = stats.shape[0], stats.shape[1]
    assert tq % STAT_BLOCK == 0, (tq, STAT_BLOCK)
    r = tq // STAT_BLOCK
    nq = nk // r
    h = FOX_HEADS
    kn = jnp.sqrt(stats[:, :, 0, h:2 * h])
    f_last = stats[:, :, 2, 0:h]
    by_query = stats.reshape(b, nq, r, 8, LANES)
    qn = jnp.sqrt(jnp.max(by_query[:, :, :, 0, 0:h], axis=2))
    kn_own = jnp.sqrt(jnp.max(by_query[:, :, :, 0, h:2 * h], axis=2))
    f_first = by_query[:, :, 0, 1, 0:h]
    bound = (qn[:, :, None, :] * (kn[:, None, :, :] + kn_own[:, :, None, :])
             + f_first[:, :, None, :] - f_last[:, None, :, :])
    key_blk = jnp.arange(nk)
    diag_start = jnp.arange(nq) * r
    need = (jnp.any(jnp.logical_not(bound <= -FOX_SKIP_LOG2), axis=-1)
            & (key_blk[None, None, :] < diag_start[None, :, None]))
    first = jnp.min(jnp.where(need, key_blk[None, None, :], nk), axis=-1)
    return jnp.minimum(first, diag_start[None, :]).astype(jnp.int32)


def _fox_direct_ok(stats, tq):
    b, nsb = stats.shape[0], stats.shape[1]
    spread = jnp.max(stats[:, :, 3, 0:FOX_HEADS].reshape(b, nsb * STAT_BLOCK // tq, -1), axis=-1)
    return (spread <= FOX_SPREAD_MAX).astype(jnp.int32)


def _fox_kernel(first_ref, direct_ref, q_ref, k_ref, vt_ref, o_ref, qm_sc, m_sc, l_sc, acc_sc, *, tq, hg):
    i = pl.program_id(2)
    hd = FOX_HEAD_DIM
    l_sc[...] = jnp.zeros_like(l_sc)
    acc_sc[...] = jnp.zeros_like(acc_sc)
    lane = lax.broadcasted_iota(jnp.int32, (tq, PAIR_LANES), 1)
    owner = jnp.where(lane < LANES, lane // hd, 2 + (lane - LANES) // AUG_LANES)
    for hh in range(hg):
        head = pl.program_id(1) * hg + hh
        own = (owner == hh % 2) | (owner == 2 + head)
        qm_sc[hh] = jnp.where(own, q_ref[0, hh // 2], jnp.zeros((), BF16))

    def accumulate_online(hh, st, off):
        m_prev = m_sc[hh]
        m_new = jnp.maximum(m_prev, jnp.max(st, axis=0, keepdims=True))
        alpha = jnp.exp2(m_prev - m_new)
        p = jnp.exp2(st - m_new)
        l_sc[hh] = alpha * l_sc[hh] + jnp.sum(p, axis=0, keepdims=True)
        vt = vt_ref[0, hh * hd:(hh + 1) * hd, pl.ds(off, st.shape[0])]
        acc_sc[hh] = alpha * acc_sc[hh] + _dot(vt, p.astype(BF16))
        m_sc[hh] = m_new

    def accumulate_direct(hh, st, off):
        p = jnp.exp2(st)
        l_sc[hh] += jnp.sum(p, axis=0, keepdims=True)
        acc_sc[hh] += _dot(vt_ref[0, hh * hd:(hh + 1) * hd, pl.ds(off, st.shape[0])], p.astype(BF16))

    def diagonal_direct():
        half = tq // 2
        off0 = pl.multiple_of(i * tq, tq)
        off1 = pl.multiple_of(i * tq + half, half)
        tri = (lax.broadcasted_iota(jnp.int32, (half, half), 0) <= lax.broadcasted_iota(jnp.int32, (half, half), 1))

        def scores(h_):
            top = _dot_nt(k_ref[0, h_ // 2, pl.ds(off0, half), :], qm_sc[h_])
            low = _dot_nt(k_ref[0, h_ // 2, pl.ds(off1, half), :], qm_sc[h_, half:tq, :])
            return top, low

        ahead = 2
        pending = [scores(h_) for h_ in range(min(ahead, hg))]
        for hh in range(hg):
            top, low = pending.pop(0)
            if hh + ahead < hg:
                pending.append(scores(hh + ahead))
            p_top = jnp.exp2(jnp.concatenate([jnp.where(tri, top[:, 0:half], -jnp.inf), top[:, half:tq]], axis=1))
            p_low = jnp.exp2(jnp.where(tri, low, -jnp.inf))
            rows = slice(hh * hd, (hh + 1) * hd)
            l_sc[hh] += jnp.sum(p_top, axis=0, keepdims=True)
            l_sc[hh, :, half:tq] += jnp.sum(p_low, axis=0, keepdims=True)
            acc_sc[hh] += _dot(vt_ref[0, rows, pl.ds(off0, half)], p_top.astype(BF16))
            acc_sc[hh, :, half:tq] += _dot(vt_ref[0, rows, pl.ds(off1, half)], p_low.astype(BF16))

    def sweep(accumulate, diagonal=None):
        def block(j, keys, masked):
            off = pl.multiple_of(j * keys, keys)
            scores = lambda h_: _dot_nt(k_ref[0, h_ // 2, pl.ds(off, keys), :], qm_sc[h_])
            ahead = 4
            pending = [scores(h_) for h_ in range(min(ahead, hg))]
            for hh in range(hg):
                st = pending.pop(0)
                if hh + ahead < hg:
                    pending.append(scores(hh + ahead))
                if masked:
                    key = lax.broadcasted_iota(jnp.int32, st.shape, 0)
                    qry = lax.broadcasted_iota(jnp.int32, st.shape, 1)
                    st = jnp.where(key <= qry, st, -jnp.inf)
                accumulate(hh, st, off)

        def body(j, carry):
            block(j, STAT_BLOCK, False)
            return carry

        lax.fori_loop(first_ref[pl.program_id(0), i], i * (tq // STAT_BLOCK), body, 0)
        if diagonal is None:
            block(i, tq, True)
        else:
            diagonal()

    direct = direct_ref[pl.program_id(0), i] != 0

    @pl.when(direct)
    def _():
        sweep(accumulate_direct, diagonal_direct)

    @pl.when(jnp.logical_not(direct))
    def _():
        m_sc[...] = jnp.full_like(m_sc, -jnp.inf)
        sweep(accumulate_online)

    for hh in range(hg):
        o_ref[0, hh * hd:(hh + 1) * hd, :] = (acc_sc[hh] / l_sc[hh]).astype(o_ref.dtype)


def _fox_attention(q, k, vt, first_block, direct_ok, tq, hg):
    b, pairs, s, _ = q.shape
    assert hg % 2 == 0
    groups = 2 * pairs // hg
    rows = hg * FOX_HEAD_DIM
    grid_spec = pltpu.PrefetchScalarGridSpec(
        num_scalar_prefetch=2,
        grid=(b, groups, s // tq),
        in_specs=[pl.BlockSpec((1, hg // 2, tq, PAIR_LANES), lambda bi, pi, qi, *_: (bi, pi, qi, 0)),
                  pl.BlockSpec((1, hg // 2, s, PAIR_LANES), lambda bi, pi, qi, *_: (bi, pi, 0, 0)),
                  pl.BlockSpec((1, rows, s), lambda bi, pi, qi, *_: (bi, pi, 0))],
        out_specs=pl.BlockSpec((1, rows, tq), lambda bi, pi, qi, *_: (bi, pi, qi)),
        scratch_shapes=[pltpu.VMEM((hg, tq, PAIR_LANES), BF16),
                        pltpu.VMEM((hg, 1, tq), F32), pltpu.VMEM((hg, 1, tq), F32),
                        pltpu.VMEM((hg, FOX_HEAD_DIM, tq), F32)])
    return pl.pallas_call(
        functools.partial(_fox_kernel, tq=tq, hg=hg),
        grid_spec=grid_spec,
        out_shape=jax.ShapeDtypeStruct((b, groups * rows, s), BF16),
        compiler_params=pltpu.CompilerParams(dimension_semantics=("arbitrary",) * 3,
                                             vmem_limit_bytes=VMEM_LIMIT),
        name="fox_attention",
    )(first_block, direct_ok, q, k, vt)


def _hgrn_chunk_stages(q_ref, k_ref, i_ref, st_sc, b_sc, o_sc, tc):
    c_len = HGRN_CHUNK
    n_chunks = tc // c_len
    t_idx = lax.broadcasted_iota(jnp.int32, (c_len, c_len), 0)
    s_idx = lax.broadcasted_iota(jnp.int32, (c_len, c_len), 1)
    causal = s_idx <= t_idx
    units = [(c, h) for c in range(n_chunks) for h in range(HGRN_HEADS)]
    rows = lambda c: slice(c * c_len, (c + 1) * c_len)
    lanes = lambda h: slice(h * HGRN_DIM, (h + 1) * HGRN_DIM)
    qe, ke, kl, decay, attn, intra, d_state, state = ({} for _ in range(8))

    def decay_operands():
        for c, h in units:
            b = b_sc[rows(c), lanes(h)]
            kk = k_ref[0, rows(c), lanes(h)]
            b_last = b[c_len - 1:c_len, :]
            qe[c, h] = (q_ref[0, rows(c), lanes(h)] * jnp.exp(b)).astype(BF16)
            ke[c, h] = (kk * jnp.exp(-b)).astype(BF16)
            kl[c, h] = (kk * jnp.exp(b_last - b)).astype(BF16)
            decay[c, h] = jnp.exp(b_last)

    def scores():
        for u in units:
            attn[u] = jnp.where(causal, _dot_nt(qe[u], ke[u]), 0.0).astype(BF16)

    def products():
        for c, h in units:
            intra[c, h] = _dot(attn[c, h], i_ref[0, rows(c), lanes(h)])
            d_state[c, h] = _dot_tn(i_ref[0, rows(c), lanes(h)], kl[c, h])

    def recurrence():
        for h in range(HGRN_HEADS):
            st = st_sc[h]
            for c in range(n_chunks):
                state[c, h] = st.astype(BF16)
                st = st * decay[c, h] + d_state[c, h]
            st_sc[h] = st

    def outputs():
        for c, h in units:
            o_sc[rows(c), lanes(h)] = intra[c, h] + _dot_nt(qe[c, h], state[c, h])

    return [decay_operands, scores, products, recurrence, outputs]


def _hgrn_step_path(q_ref, k_ref, i_ref, st_sc, b_sc, o_sc, tc):
    row8 = lax.broadcasted_iota(jnp.int32, (8, HGRN_DIM), 0)
    half = HGRN_STEP // 2

    def step(u, carry):
        base = pl.multiple_of(u * HGRN_STEP, HGRN_STEP)
        for h in range(HGRN_HEADS):
            ls = slice(h * HGRN_DIM, (h + 1) * HGRN_DIM)
            rows = pl.ds(base, HGRN_STEP)
            bq = b_sc[rows, ls]
            qq = q_ref[0, rows, ls]
            kk = k_ref[0, rows, ls]
            ii = i_ref[0, rows, ls].astype(F32)
            b_last = bq[HGRN_STEP - 1:HGRN_STEP, :]
            st = st_sc[h]
            inter = _dot_nt((qq * jnp.exp(bq)).astype(BF16), st.astype(BF16))
            k_dec = (kk * jnp.exp(b_last - bq)).astype(BF16)
            st_sc[h] = st * jnp.exp(b_last) + _dot_tn(ii.astype(BF16), k_dec)
            out = [inter[0:half], inter[half:HGRN_STEP]]
            qv = [qq[0:half], qq[half:HGRN_STEP]]
            bv = [bq[0:half], bq[half:HGRN_STEP]]
            for s_ in range(HGRN_STEP):
                k_s, b_s, i_s = kk[s_:s_ + 1, :], bq[s_:s_ + 1, :], ii[s_:s_ + 1, :]
                for v_ in range(s_ // half, 2):
                    diff = bv[v_] - b_s
                    if s_ > v_ * half:
                        diff = jnp.where(row8 + v_ * half >= s_, diff, -jnp.inf)
                    a = jnp.sum(qv[v_] * k_s * jnp.exp(diff), axis=-1, keepdims=True)
                    out[v_] = out[v_] + a * i_s
            o_sc[pl.ds(base, half), ls] = out[0]
            o_sc[pl.ds(base + half, half), ls] = out[1]
        return carry

    lax.fori_loop(0, tc // HGRN_STEP, step, 0)


def _block_tri(n, blk):
    return jnp.asarray(np.kron(np.eye(n // blk, dtype=np.float32), np.tril(np.ones((blk, blk), np.float32))), BF16)


def _memkv_kernel(mem_ref, g_ref, wkv_ref, wq_ref, wo_ref, sm_ref, om_ref):
    d = mem_ref.shape[-1]
    m = mem_ref.shape[1]
    hd = d // X_HEADS
    inv = 1.0 / math.sqrt(hd)
    kv = _dot(_rms(mem_ref[0], g_ref[...]).astype(BF16), wkv_ref[...].astype(BF16))
    for h in range(X_HEADS):
        cs = slice(h * hd, (h + 1) * hd)
        k_h = kv[:, cs].astype(BF16)
        v_h = kv[:, d + h * hd:d + (h + 1) * hd].astype(BF16)
        sm_ref[0, :, h * m:(h + 1) * m] = (_dot_nt(wq_ref[:, cs].astype(BF16), k_h) * inv).astype(BF16)
        om_ref[0, h * m:(h + 1) * m, :] = _dot(v_h, wo_ref[cs, :].astype(BF16)).astype(BF16)


def _mem_kv(mem, norm_g, w_kv, w_xq, w_xo):
    b, m, d = mem.shape
    return pl.pallas_call(
        _memkv_kernel,
        grid=(b,),
        in_specs=[pl.BlockSpec((1, m, d), lambda bi: (bi, 0, 0)), _resident((1, d)), _resident((d, 2 * d)),
                  _resident((d, d)), _resident((d, d))],
        out_specs=[pl.BlockSpec((1, d, X_HEADS * m), lambda bi: (bi, 0, 0)),
                   pl.BlockSpec((1, X_HEADS * m, d), lambda bi: (bi, 0, 0))],
        out_shape=[jax.ShapeDtypeStruct((b, d, X_HEADS * m), BF16), jax.ShapeDtypeStruct((b, X_HEADS * m, d), BF16)],
        compiler_params=pltpu.CompilerParams(dimension_semantics=("arbitrary",), vmem_limit_bytes=VMEM_LIMIT),
        name="mem_kv",
    )(mem, norm_g.reshape(1, d), w_kv, w_xq, w_xo)


def _mix_kernel(x_ref, foxt_ref, rec_ref, wo_ref, gx_ref, sm_ref, om_ref, o_ref, h1_sc, p_sc, *, sub):
    fw = foxt_ref.shape[1]
    n_mem = sm_ref.shape[2] // X_HEADS
    n_sub = x_ref.shape[1] // sub

    def out_proj(i):
        r = slice(i * sub, (i + 1) * sub)
        h1_sc[r, :] = (x_ref[0, r, :] + _dot_tn(foxt_ref[0, :, r], wo_ref[0:fw, :].astype(BF16))
                       + _dot(rec_ref[0, r, :], wo_ref[fw:, :].astype(BF16)))

    def attend(i):
        r = slice(i * sub, (i + 1) * sub)
        s = _dot(_rms(h1_sc[r, :], gx_ref[...]).astype(BF16), sm_ref[0])
        for h in range(X_HEADS):
            cs = slice(h * n_mem, (h + 1) * n_mem)
            p = jnp.exp(s[:, cs] - jnp.max(s[:, cs], axis=-1, keepdims=True))
            p_sc[r, cs] = (p / jnp.sum(p, axis=-1, keepdims=True)).astype(BF16)

    def finish(i):
        r = slice(i * sub, (i + 1) * sub)
        o_ref[0, r, :] = h1_sc[r, :] + _dot(p_sc[r, :], om_ref[0])

    stages = (out_proj, attend, finish)
    for t in range(n_sub + len(stages) - 1):
        for k, stage in enumerate(stages):
            if 0 <= t - k < n_sub:
                stage(t - k)


def _mix(x, foxt, rec, w_out, norm_g, score_m, out_m, tm):
    b, s, d = x.shape
    fw = foxt.shape[1]
    hm = score_m.shape[2]
    tok = lambda wd: pl.BlockSpec((1, tm, wd), lambda bi, ti: (bi, ti, 0))
    return pl.pallas_call(
        functools.partial(_mix_kernel, sub=min(tm, 256)),
        grid=(b, s // tm),
        in_specs=[tok(d), pl.BlockSpec((1, fw, tm), lambda bi, ti: (bi, 0, ti)), tok(rec.shape[-1]),
                  _resident((d, d)), _resident((1, d)),
                  pl.BlockSpec((1, d, hm), lambda bi, ti: (bi, 0, 0)), pl.BlockSpec((1, hm, d), lambda bi, ti: (bi, 0, 0))],
        out_specs=tok(d),
        out_shape=jax.ShapeDtypeStruct((b, s, d), F32),
        scratch_shapes=[pltpu.VMEM((tm, d), F32), pltpu.VMEM((tm, hm), BF16)],
        compiler_params=pltpu.CompilerParams(dimension_semantics=("arbitrary", "arbitrary"),
                                             vmem_limit_bytes=VMEM_LIMIT),
        name="mix",
    )(x, foxt, rec, w_out, norm_g.reshape(1, d), score_m, out_m)


def _mlp_kernel(h_ref, g_ref, w1_ref, w2_ref, gf_ref, o_ref, acc_sc, *, fc):
    hn = _rms(h_ref[...], g_ref[...]).astype(BF16)
    n_chunks = w1_ref.shape[1] // fc
    up = lambda c: _dot(hn, w1_ref[:, c * fc:(c + 1) * fc].astype(BF16))
    u_next = up(0)
    for c in range(n_chunks):
        u = jnp.maximum(u_next, 0.0)
        if c + 1 < n_chunks:
            u_next = up(c + 1)
        part = _dot((u * u).astype(BF16), w2_ref[c * fc:(c + 1) * fc, :].astype(BF16))
        if c == 0:
            acc_sc[...] = part
        else:
            acc_sc[...] += part
    o_ref[...] = _rms(h_ref[...] + acc_sc[...], gf_ref[...])


def _mlp(h, norm_g, w1, w2, final_g, tm, fc):
    t, d = h.shape
    dff = w1.shape[1]
    return pl.pallas_call(
        functools.partial(_mlp_kernel, fc=fc),
        grid=(t // tm,),
        in_specs=[pl.BlockSpec((tm, d), lambda ti: (ti, 0)), _resident((1, d)),
                  _resident((d, dff)), _resident((dff, d)), _resident((1, d))],
        out_specs=pl.BlockSpec((tm, d), lambda ti: (ti, 0)),
        out_shape=jax.ShapeDtypeStruct((t, d), F32),
        scratch_shapes=[pltpu.VMEM((tm, d), F32)],
        compiler_params=pltpu.CompilerParams(dimension_semantics=("arbitrary",), vmem_limit_bytes=VMEM_LIMIT),
        name="mlp",
    )(h, norm_g.reshape(1, d), w1, w2, final_g.reshape(1, d))


def _tile(n, want):
    t = min(n, want)
    assert n % t == 0, (n, want)
    return t


def _tiles(b, s, d_ff):
    return dict(
        front=_tile(s, 512),
        fox_q=_tile(s, 512),
        fox_heads=FOX_HEADS,
        mix=_tile(s, 1024),
        mlp=_tile(b * s, 512),
        mlp_ff=_tile(d_ff, 1024),
    )


def kernel(x, mem, norm_mix_g, w_in, fox_f_bias, hgrn_lb_logits, hgrn_norm_g, w_out, norm_x_g, norm_mem_g,
           w_xq, w_xkv, w_xo, norm_ff_g, w1, w2, final_norm_g):
    b, s, d = x.shape
    t = _tiles(b, s, w1.shape[-1])
    h = x
    for l in range(w_in.shape[0]):
        q, k, v, stats, rec = _front(h, norm_mix_g[l], w_in[l], fox_f_bias[l], hgrn_lb_logits, hgrn_norm_g[l], l,
                                     t["front"])
        fox = _fox_attention(q, k, v, _fox_first_block(stats, t["fox_q"]), _fox_direct_ok(stats, t["fox_q"]),
                             t["fox_q"], t["fox_heads"])
        score_m, out_m = _mem_kv(mem, norm_mem_g[l], w_xkv[l], w_xq[l], w_xo[l])
        h = _mix(h, fox, rec, w_out[l], norm_x_g[l], score_m, out_m, t["mix"])
        is_last = l == w_in.shape[0] - 1
        assert is_last, "the MLP kernel fuses the final norm, so it must be the last layer"
        h = _mlp(h.reshape(b * s, d), norm_ff_g[l], w1[l], w2[l], final_norm_g, t["mlp"], t["mlp_ff"]).reshape(b, s, d)
    return h
```
